```python
import jax, jax.numpy as jnp
from jax import lax
import numpy as np

D_MODEL = 1024
BATCH = 16
SEQ = 256
DEPTH = 4
DEC_BATCH = 4
DEC_SEQ = 2048
PAST_LEN = 512

GRID_W = 64
A_HEADS = 4
A_HEAD_DIM = 64
A_WIDTH = A_HEADS * A_HEAD_DIM
CHUNK = 128
POOL_WINDOWS = (2, 4, 8, 16)
POOL_GROUP = 64
B_WIDTH = len(POOL_WINDOWS) * POOL_GROUP
C_HEADS = 8
QK_NOPE = 64
QK_ROPE = 32
V_DIM = 64
Q_LORA = 384
KV_LORA = 256
C_WIDTH = C_HEADS * V_DIM
MIX_WIDTH = A_WIDTH + B_WIDTH + C_WIDTH
ROPE_AXIS = QK_ROPE // 2
ROPE_THETA = 10000.0
Q_BLOCK = 128
ATTN_SCALE = (QK_NOPE + QK_ROPE) ** -0.5
OFF_U = 0
OFF_V = OFF_U + A_WIDTH
OFF_P = OFF_V + A_WIDTH
OFF_CQ = OFF_P + B_WIDTH
OFF_CKV = OFF_CQ + Q_LORA
OFF_KR = OFF_CKV + KV_LORA
PROJ_COLS = OFF_KR + QK_ROPE
N_EXPERTS = 16
EXPERT_FF = 1024
EC_CAPACITY = 2
NORM_EPS = 1e-6

kernel_name = "hybrid_diffusion_ec_mla_step"


def rmsnorm(x, g):
    xf = x.astype(jnp.float32)
    y = xf * lax.rsqrt(jnp.mean(xf * xf, axis=-1, keepdims=True) + NORM_EPS)
    return y.astype(x.dtype) * g


def axial_rope_tables(n, dtype):
    rows = n // GRID_W
    row = jnp.repeat(jnp.arange(rows), GRID_W).astype(jnp.float32)
    col = jnp.broadcast_to(jnp.arange(GRID_W), (rows, GRID_W)).reshape(-1).astype(jnp.float32)
    inv = 1.0 / (ROPE_THETA ** (jnp.arange(0, ROPE_AXIS, 2, dtype=jnp.float32) / ROPE_AXIS))
    ang_r = row[:, None] * inv[None, :]
    ang_c = col[:, None] * inv[None, :]
    return (jnp.cos(ang_r).astype(dtype), jnp.sin(ang_r).astype(dtype),
            jnp.cos(ang_c).astype(dtype), jnp.sin(ang_c).astype(dtype))


def rotate(x, cos, sin):
    h = ROPE_AXIS // 2
    x1, x2 = x[..., :h], x[..., h:]
    return jnp.concatenate([x1 * cos - x2 * sin, x1 * sin + x2 * cos], axis=-1)


def axial_rope(x, tabs):
    cos_r, sin_r, cos_c, sin_c = tabs
    return jnp.concatenate([rotate(x[..., :ROPE_AXIS], cos_r, sin_r),
                            rotate(x[..., ROPE_AXIS:], cos_c, sin_c)], axis=-1)


def chunk_mlp(u, v, w_s, b_s):
    b, n, _ = u.shape
    nc = n // CHUNK
    vc = v.reshape(b, nc, CHUNK, A_HEADS, A_HEAD_DIM)
    s = jnp.einsum('hpq,bcqhd->bcphd', w_s, vc) + b_s.T[:, :, None]
    return u * s.reshape(b, n, A_WIDTH)


def multiscale_pool(p, w_pool, pool_scale):
    b, n, _ = p.shape
    pf = p.astype(jnp.float32)
    cs = jnp.concatenate([jnp.zeros((b, 1, B_WIDTH), jnp.float32), jnp.cumsum(pf, axis=1)], axis=1)
    t = jnp.arange(n)
    outs = []
    for g, w in enumerate(POOL_WINDOWS):
        sl = slice(g * POOL_GROUP, (g + 1) * POOL_GROUP)
        left = w // 2
        right = w - 1 - left
        lo = jnp.clip(t - left, 0, n)
        hi = jnp.clip(t + right + 1, 0, n)
        seg = cs[..., sl]
        mean = (seg[:, hi] - seg[:, lo]) / (hi - lo).astype(jnp.float32)[None, :, None]
        diff = (mean - pf[..., sl]).astype(p.dtype)
        outs.append(jnp.einsum('bnc,cd->bnd', diff, w_pool[g]))
    return jnp.concatenate(outs, axis=-1) * pool_scale


def attention(q, k, v):
    b, nq, h, dq = q.shape
    nb = nq // Q_BLOCK
    qb = q.reshape(b, nb, Q_BLOCK, h, dq).transpose(1, 0, 2, 3, 4)

    def block(qi):
        s = jnp.einsum('bqhd,bkhd->bhqk', qi, k).astype(jnp.float32) * ATTN_SCALE
        pr = jax.nn.softmax(s, axis=-1).astype(v.dtype)
        return jnp.einsum('bhqk,bkhd->bqhd', pr, v)

    o = lax.map(block, qb)
    return o.transpose(1, 0, 2, 3, 4).reshape(b, nq, h * v.shape[-1])


def mla_keys_values(ckv, krope, w_kvb_l):
    b, m, _ = ckv.shape
    kv = jnp.einsum('bmr,rf->bmf', ckv, w_kvb_l).reshape(b, m, C_HEADS, QK_NOPE + V_DIM)
    k_nope, v = kv[..., :QK_NOPE], kv[..., QK_NOPE:]
    k = jnp.concatenate([k_nope, jnp.broadcast_to(krope[:, :, None, :], (b, m, C_HEADS, QK_ROPE))], axis=-1)
    return k, v


def ec_ffn(h, w_router, w1, w3, w2):
    b, n, d = h.shape
    cap = EC_CAPACITY * n // N_EXPERTS
    logits = jnp.einsum('bnd,de->bne', h, w_router).astype(jnp.float32)
    aff = jax.nn.softmax(logits, axis=-1).transpose(0, 2, 1)
    gate, idx = lax.top_k(aff, cap)
    xs = jax.vmap(lambda hb, ib: hb[ib])(h, idx)
    hid = jax.nn.silu(jnp.einsum('becd,edf->becf', xs, w1)) * jnp.einsum('becd,edf->becf', xs, w3)
    y = jnp.einsum('becf,efd->becd', hid, w2) * gate[..., None].astype(h.dtype)
    return jax.vmap(lambda ib, yb: jnp.zeros((n, d), yb.dtype).at[ib.reshape(-1)].add(yb.reshape(-1, d)))(idx, y)


def mixing(h, P, l, ctx):
    b, n, _ = h.shape
    proj = jnp.einsum('bnd,df->bnf', h, P['w_in'][l])
    u = jax.nn.gelu(proj[..., OFF_U:OFF_V])
    va = jax.nn.gelu(proj[..., OFF_V:OFF_P])
    pin = proj[..., OFF_P:OFF_CQ]
    c_q = proj[..., OFF_CQ:OFF_CKV]
    c_kv = proj[..., OFF_CKV:OFF_KR]
    k_rope = proj[..., OFF_KR:PROJ_COLS]
    out_a = chunk_mlp(u, va, P['w_s'][l], P['b_s'][l])
    out_b = multiscale_pool(pin, P['w_pool'][l], P['pool_scale'][l])
    q = jnp.einsum('bnr,rf->bnf', rmsnorm(c_q, P['q_norm_g'][l]), P['w_qb'][l])
    q = q.reshape(b, n, C_HEADS, QK_NOPE + QK_ROPE)
    ckv = rmsnorm(c_kv, P['kv_norm_g'][l])
    if ctx is None:
        all_ckv, all_krope = ckv, k_rope
    else:
        ctx_ckv, ctx_krope = ctx
        tabs = axial_rope_tables(n, h.dtype)
        q_tabs = tuple(t[:, None, :] for t in tabs)
        q = jnp.concatenate([q[..., :QK_NOPE], axial_rope(q[..., QK_NOPE:], q_tabs)], axis=-1)
        all_ckv = jnp.concatenate([ctx_ckv, ckv], axis=1)
        all_krope = jnp.concatenate([ctx_krope, axial_rope(k_rope, tabs)], axis=1)
    k, v = mla_keys_values(all_ckv, all_krope, P['w_kvb'][l])
    out_c = attention(q, k, v)
    mixed = jnp.concatenate([out_a, out_b, out_c], axis=-1)
    return jnp.einsum('bnm,md->bnd', mixed, P['w_out'][l]), ckv, k_rope


def trunk_layer(x, cond, P, l, ctx=None):
    mod = jnp.einsum('bd,df->bf', jax.nn.silu(cond), P['w_ada'][l]) + P['b_ada'][l]
    sh1, sc1, g1, sh2, sc2, g2 = jnp.split(mod[:, None, :], 6, axis=-1)
    h = rmsnorm(x, P['norm1_g'][l]) * (1 + sc1) + sh1
    mix, ckv, krope = mixing(h, P, l, ctx)
    x = x + g1 * mix
    h = rmsnorm(x, P['norm2_g'][l]) * (1 + sc2) + sh2
    x = x + g2 * ec_ffn(h, P['w_router'][l], P['w_e1'][l], P['w_e3'][l], P['w_e2'][l])
    return x, ckv, krope


def setup_inputs(seed: int = 0) -> dict:
    key = jax.random.key(seed)
    ks = jax.random.split(key, 26)
    f32 = jnp.float32

    def nrm(k, shape, scale):
        return jax.random.normal(k, shape, f32) * scale

    return {
        "x_prompt": nrm(ks[0], (BATCH, SEQ, D_MODEL), 1.0),
        "x_sample": nrm(ks[1], (DEC_BATCH, DEC_SEQ, D_MODEL), 1.0),
        "cache_ckv": nrm(ks[2], (DEC_BATCH, DEPTH, PAST_LEN, KV_LORA), 1.0),
        "cache_krope": nrm(ks[3], (DEC_BATCH, DEPTH, PAST_LEN, QK_ROPE), 1.0),
        "c": nrm(ks[4], (DEC_BATCH, D_MODEL), 1.0),
        "c_ctx": nrm(ks[5], (D_MODEL,), 1.0),
        "w_ada": nrm(ks[6], (DEPTH, D_MODEL, 6 * D_MODEL), 0.5 * D_MODEL ** -0.5),
        "b_ada": nrm(ks[7], (DEPTH, 6 * D_MODEL), 0.01),
        "norm1_g": 1.0 + nrm(ks[8], (DEPTH, D_MODEL), 0.05),
        "norm2_g": 1.0 + nrm(ks[9], (DEPTH, D_MODEL), 0.05),
        "w_in": nrm(ks[10], (DEPTH, D_MODEL, PROJ_COLS), D_MODEL ** -0.5),
        "w_s": nrm(ks[11], (DEPTH, A_HEADS, CHUNK, CHUNK), CHUNK ** -0.5),
        "b_s": 1.0 + nrm(ks[12], (DEPTH, A_HEADS, CHUNK), 0.1),
        "w_pool": nrm(ks[13], (DEPTH, len(POOL_WINDOWS), POOL_GROUP, POOL_GROUP), POOL_GROUP ** -0.5),
        "pool_scale": 1.0 + nrm(ks[14], (DEPTH, B_WIDTH), 0.1),
        "q_norm_g": 1.0 + nrm(ks[15], (DEPTH, Q_LORA), 0.05),
        "w_qb": nrm(ks[16], (DEPTH, Q_LORA, C_HEADS * (QK_NOPE + QK_ROPE)), Q_LORA ** -0.5),
        "kv_norm_g": 1.0 + nrm(ks[17], (DEPTH, KV_LORA), 0.05),
        "w_kvb": nrm(ks[18], (DEPTH, KV_LORA, C_HEADS * (QK_NOPE + V_DIM)), KV_LORA ** -0.5),
        "w_out": nrm(ks[19], (DEPTH, MIX_WIDTH, D_MODEL), MIX_WIDTH ** -0.5),
        "w_router": nrm(ks[20], (DEPTH, D_MODEL, N_EXPERTS), D_MODEL ** -0.5),
        "w_e1": nrm(ks[21], (DEPTH, N_EXPERTS, D_MODEL, EXPERT_FF), D_MODEL ** -0.5),
        "w_e3": nrm(ks[22], (DEPTH, N_EXPERTS, D_MODEL, EXPERT_FF), D_MODEL ** -0.5),
        "w_e2": nrm(ks[23], (DEPTH, N_EXPERTS, EXPERT_FF, D_MODEL), EXPERT_FF ** -0.5),
        "final_norm_g": 1.0 + nrm(ks[24], (D_MODEL,), 0.05),
    }


def reference(x_prompt, x_sample, cache_ckv, cache_krope, c, c_ctx, w_ada, b_ada, norm1_g, norm2_g,
              w_in, w_s, b_s, w_pool, pool_scale, q_norm_g, w_qb, kv_norm_g, w_kvb, w_out,
              w_router, w_e1, w_e3, w_e2, final_norm_g):
    P = {'w_ada': w_ada, 'b_ada': b_ada, 'norm1_g': norm1_g, 'norm2_g': norm2_g, 'w_in': w_in,
         'w_s': w_s, 'b_s': b_s, 'w_pool': w_pool, 'pool_scale': pool_scale, 'q_norm_g': q_norm_g,
         'w_qb': w_qb, 'kv_norm_g': kv_norm_g, 'w_kvb': w_kvb, 'w_out': w_out, 'w_router': w_router,
         'w_e1': w_e1, 'w_e3': w_e3, 'w_e2': w_e2}
    cond_ctx = c_ctx[None, :]
    xp = x_prompt
    ckvs = []
    kropes = []
    for l in range(DEPTH):
        xp, ckv_l, krope_l = trunk_layer(xp, cond_ctx, P, l)
        ckvs.append(ckv_l)
        kropes.append(krope_l)
    y_prompt = rmsnorm(xp, final_norm_g)
    new_ckv = jnp.stack(ckvs, axis=1)
    new_krope = jnp.stack(kropes, axis=1)
    xs = x_sample
    for l in range(DEPTH):
        xs, _, _ = trunk_layer(xs, c, P, l, ctx=(cache_ckv[:, l], cache_krope[:, l]))
    y_sample = rmsnorm(xs, final_norm_g)
    return (y_prompt, y_sample, new_ckv, new_krope)
```

```python
import functools

import numpy as np
import jax
import jax.numpy as jnp
from jax import lax
from jax.experimental import pallas as pl
from jax.experimental.pallas import tpu as pltpu

F32 = jnp.float32
BF16 = jnp.bfloat16

GRID_W = 64
A_HEADS = 4
A_HEAD_DIM = 64
A_WIDTH = A_HEADS * A_HEAD_DIM
CHUNK = 128
POOL_WINDOWS = (2, 4, 8, 16)
POOL_GROUP = 64
B_WIDTH = len(POOL_WINDOWS) * POOL_GROUP
C_HEADS = 8
QK_NOPE = 64
QK_ROPE = 32
V_DIM = 64
Q_LORA = 384
KV_LORA = 256
C_WIDTH = C_HEADS * V_DIM
ROPE_AXIS = QK_ROPE // 2
ROPE_THETA = 10000.0
ATTN_SCALE = (QK_NOPE + QK_ROPE) ** -0.5
N_EXPERTS = 16
EC_CAPACITY = 2
NORM_EPS = 1e-6

LANES = 128
HEAD_PAD = 128
PROJ_PAD = 1536
OFF_P = 2 * A_WIDTH
OFF_CQ = OFF_P + B_WIDTH
OFF_CKV = OFF_CQ + Q_LORA
OFF_KR = OFF_CKV + KV_LORA
POOL_HALO = 8
TOK_TILE = 256
KEY_CHUNK = 512
FF_TILE = 512
MOD_TILE = 1536
SEARCH_BITS = 31
SLOT_ALIGN = 16


def _cp(vmem_mb):
    return pltpu.CompilerParams(vmem_limit_bytes=vmem_mb * 1024 * 1024)


def _dot(a, b):
    return jnp.dot(a, b, preferred_element_type=F32)


def _dot_nt(a, b):
    return lax.dot_general(a, b, (((1,), (1,)), ((), ())), preferred_element_type=F32)


def _split2(x):
    hi = x.astype(BF16)
    lo = (x - hi.astype(F32)).astype(BF16)
    return hi, lo


def _rms(x):
    return x * lax.rsqrt(jnp.mean(x * x, axis=-1, keepdims=True) + NORM_EPS)


def _mod_kernel(c_ref, w_ref, b_ref, o_ref):
    a = jax.nn.silu(c_ref[...])
    a_hi, a_lo = _split2(a)
    w_hi, w_lo = _split2(w_ref[...])
    o_ref[...] = _dot(a_hi, w_hi) + _dot(a_lo, w_hi) + _dot(a_hi, w_lo) + b_ref[...]


def _mod_call(cond8, w_ada, b_ada):
    depth, d, n = w_ada.shape
    return pl.pallas_call(
        _mod_kernel,
        grid=(depth, n // MOD_TILE),
        in_specs=[pl.BlockSpec((8, d), lambda l, j: (0, 0)),
                  pl.BlockSpec((None, d, MOD_TILE), lambda l, j: (l, 0, j)),
                  pl.BlockSpec((None, 1, MOD_TILE), lambda l, j: (l, 0, j))],
        out_specs=pl.BlockSpec((None, 8, MOD_TILE), lambda l, j: (l, 0, j)),
        out_shape=jax.ShapeDtypeStruct((depth, 8, n), F32),
        compiler_params=_cp(40),
        name="adaln_mod",
    )(cond8, w_ada, b_ada.reshape(depth, 1, n))


def _proj_kernel(*refs, rope):
    if rope:
        (x_ref, sh_ref, sc_ref, g_ref, win_ref, qg_ref, kvg_ref, wq_ref, wkn_ref, wvt_ref,
         ck_ref, sk_ref, cr_ref, sr_ref, cc_ref, scl_ref,
         u_ref, va_ref, pin_ref, qt_ref, kh_ref, vt_ref, ckv_ref, kr_ref) = refs
    else:
        (x_ref, sh_ref, sc_ref, g_ref, win_ref, qg_ref, kvg_ref, wq_ref, wkn_ref, wvt_ref,
         u_ref, va_ref, pin_ref, qt_ref, kh_ref, vt_ref, ckv_ref, kr_ref) = refs
    x = x_ref[...]
    h = _rms(x) * g_ref[...] * (1.0 + sc_ref[...]) + sh_ref[...]
    proj = _dot(h.astype(BF16), win_ref[...])
    u_ref[...] = jax.nn.gelu(proj[:, 0:A_WIDTH])
    va_ref[...] = jax.nn.gelu(proj[:, A_WIDTH:OFF_P]).astype(BF16)
    pin_ref[...] = proj[:, OFF_P:OFF_CQ]
    cq = _rms(proj[:, OFF_CQ:OFF_CKV]) * qg_ref[...]
    ckv = _rms(proj[:, OFF_CKV:OFF_KR]) * kvg_ref[...]
    ckv_ref[...] = ckv
    kr = proj[:, OFF_KR:OFF_KR + LANES]
    kr_ref[...] = kr[:, 0:QK_ROPE]

    qt = _dot_nt(wq_ref[...], cq.astype(BF16))
    ckv_b = ckv.astype(BF16)
    kn = _dot(ckv_b, wkn_ref[...])
    vt = _dot_nt(wvt_ref[...], ckv_b)

    if rope:
        lane = lax.broadcasted_iota(jnp.int32, kr.shape, 1)
        half = ROPE_AXIS // 2
        swapped = jnp.where((lane & half) == 0,
                            pltpu.roll(kr, LANES - half, axis=1), pltpu.roll(kr, half, axis=1))
        kr = kr * ck_ref[...] + swapped * sk_ref[...]
        cr, sr, cc, scl = cr_ref[...], sr_ref[...], cc_ref[...], scl_ref[...]
    kr_shift = pltpu.roll(kr, QK_NOPE, axis=1)

    for hd in range(C_HEADS):
        r0 = hd * HEAD_PAD
        q_h = qt[r0:r0 + HEAD_PAD, :]
        if rope:
            b0 = QK_NOPE
            a, b = q_h[b0:b0 + 8, :], q_h[b0 + 8:b0 + 16, :]
            c, d = q_h[b0 + 16:b0 + 24, :], q_h[b0 + 24:b0 + 32, :]
            q_h = jnp.concatenate(
                [q_h[0:b0, :], a * cr - b * sr, a * sr + b * cr, c * cc - d * scl, c * scl + d * cc,
                 q_h[b0 + 32:, :]], axis=0)
        qt_ref[hd] = q_h.astype(BF16)
        kh_ref[hd] = (kn[:, r0:r0 + HEAD_PAD] + kr_shift).astype(BF16)
        vt_ref[hd] = vt[hd * V_DIM:(hd + 1) * V_DIM, :].astype(BF16)


def _proj_call(x, modl, row0, row_step, lw, tabs):
    nb, n, d = x.shape
    tm = TOK_TILE
    rope = tabs is not None
    tok = lambda c: pl.BlockSpec((None, tm, c), lambda b, i: (b, i, 0))
    const = lambda a: pl.BlockSpec(a.shape, lambda b, i: (0,) * a.ndim)
    modspec = lambda j: pl.BlockSpec((None, None, 1, d), lambda b, i: (row0 + b * row_step, j, 0, 0))
    ins = [x, modl, modl, lw["norm1_g"], lw["w_in"], lw["q_norm_g"], lw["kv_norm_g"],
           lw["wq_t"], lw["wkn"], lw["wv_t"]]
    in_specs = [tok(d), modspec(0), modspec(1), const(lw["norm1_g"]), const(lw["w_in"]),
                const(lw["q_norm_g"]), const(lw["kv_norm_g"]), const(lw["wq_t"]), const(lw["wkn"]),
                const(lw["wv_t"])]
    if rope:
        cos_k, sin_k, cos_r, sin_r, cos_c, sin_c = tabs
        ins += [cos_k, sin_k, cos_r, sin_r, cos_c, sin_c]
        in_specs += [pl.BlockSpec((tm, LANES), lambda b, i: (i, 0))] * 2
        in_specs += [pl.BlockSpec((8, tm), lambda b, i: (0, i))] * 4
    out_shape = [jax.ShapeDtypeStruct((nb, n, A_WIDTH), F32),
                 jax.ShapeDtypeStruct((nb, n, A_WIDTH), BF16),
                 jax.ShapeDtypeStruct((nb, n, B_WIDTH), F32),
                 jax.ShapeDtypeStruct((nb, C_HEADS, HEAD_PAD, n), BF16),
                 jax.ShapeDtypeStruct((nb, C_HEADS, n, HEAD_PAD), BF16),
                 jax.ShapeDtypeStruct((nb, C_HEADS, V_DIM, n), BF16),
                 jax.ShapeDtypeStruct((nb, n, KV_LORA), F32),
                 jax.ShapeDtypeStruct((nb, n, QK_ROPE), F32)]
    out_specs = [tok(A_WIDTH), tok(A_WIDTH), tok(B_WIDTH),
                 pl.BlockSpec((None, C_HEADS, HEAD_PAD, tm), lambda b, i: (b, 0, 0, i)),
                 pl.BlockSpec((None, C_HEADS, tm, HEAD_PAD), lambda b, i: (b, 0, i, 0)),
                 pl.BlockSpec((None, C_HEADS, V_DIM, tm), lambda b, i: (b, 0, 0, i)),
                 tok(KV_LORA), tok(QK_ROPE)]
    return pl.pallas_call(
        functools.partial(_proj_kernel, rope=rope),
        grid=(nb, n // tm), in_specs=in_specs, out_specs=out_specs, out_shape=out_shape,
        compiler_params=_cp(48), name="in_proj_rope" if rope else "in_proj",
    )(*ins)


def _kvexp_kernel(ckv_ref, kr_ref, wkn_ref, wvt_ref, place_ref, kh_ref, vt_ref):
    ckv_b = ckv_ref[...].astype(BF16)
    kn = _dot(ckv_b, wkn_ref[...])
    vt = _dot_nt(wvt_ref[...], ckv_b)
    kr_placed = _dot(kr_ref[...].astype(BF16), place_ref[...])
    for hd in range(C_HEADS):
        r0 = hd * HEAD_PAD
        kh_ref[hd] = (kn[:, r0:r0 + HEAD_PAD] + kr_placed).astype(BF16)
        vt_ref[hd] = vt[hd * V_DIM:(hd + 1) * V_DIM, :].astype(BF16)


def _kvexp_call(cache_ckv, cache_krope, wkn, wv_t):
    nb, depth, m, r = cache_ckv.shape
    place = np.zeros((QK_ROPE, HEAD_PAD), np.float32)
    place[np.arange(QK_ROPE), QK_NOPE + np.arange(QK_ROPE)] = 1.0
    place = jnp.asarray(place, BF16)
    return pl.pallas_call(
        _kvexp_kernel,
        grid=(depth, nb),
        in_specs=[pl.BlockSpec((None, None, m, r), lambda l, b: (b, l, 0, 0)),
                  pl.BlockSpec((None, None, m, QK_ROPE), lambda l, b: (b, l, 0, 0)),
                  pl.BlockSpec((None,) + wkn.shape[1:], lambda l, b: (l, 0, 0)),
                  pl.BlockSpec((None,) + wv_t.shape[1:], lambda l, b: (l, 0, 0)),
                  pl.BlockSpec(place.shape, lambda l, b: (0, 0))],
        out_specs=[pl.BlockSpec((None, None, C_HEADS, m, HEAD_PAD), lambda l, b: (l, b, 0, 0, 0)),
                   pl.BlockSpec((None, None, C_HEADS, V_DIM, m), lambda l, b: (l, b, 0, 0, 0))],
        out_shape=[jax.ShapeDtypeStruct((depth, nb, C_HEADS, m, HEAD_PAD), BF16),
                   jax.ShapeDtypeStruct((depth, nb, C_HEADS, V_DIM, m), BF16)],
        compiler_params=_cp(32), name="cache_kv_expand",
    )(cache_ckv, cache_krope, wkn, wv_t, place)


def _mix_kernel(u_ref, va_ref, pin_ref, ws_ref, bs_ref, band_ref, wp_ref, ps_ref,
                oa_ref, ob_ref, pad_ref):
    n = u_ref.shape[0]
    width = u_ref.shape[1]
    zero_rows = jnp.zeros((POOL_HALO, width), F32)
    pad_ref[0:POOL_HALO, :] = zero_rows
    pad_ref[n + POOL_HALO:n + 2 * POOL_HALO, :] = zero_rows
    pad_ref[POOL_HALO:n + POOL_HALO, :] = pin_ref[...]
    lane_grp = lax.broadcasted_iota(jnp.int32, (CHUNK, width), 1) // POOL_GROUP
    row = lax.broadcasted_iota(jnp.int32, (CHUNK, width), 0)

    def body(c, carry):
        r0 = pl.multiple_of(c * CHUNK, CHUNK)
        v = va_ref[pl.ds(r0, CHUNK), :]
        s = bs_ref[...]
        for hd in range(A_HEADS):
            s = s + jnp.where(lane_grp == hd, _dot(ws_ref[hd], v), 0.0)
        oa_ref[pl.ds(r0, CHUNK), :] = (u_ref[pl.ds(r0, CHUNK), :] * s).astype(BF16)
        win = pad_ref[pl.ds(r0, CHUNK + 2 * POOL_HALO), :]
        w_hi, w_lo = _split2(win)
        t = r0 + row
        tot = jnp.zeros((CHUNK, width), F32)
        cnt = jnp.ones((CHUNK, width), jnp.int32)
        for g, w in enumerate(POOL_WINDOWS):
            left = w // 2
            right = w - 1 - left
            sg = _dot(band_ref[g], w_hi) + _dot(band_ref[g], w_lo)
            tot = jnp.where(lane_grp == g, sg, tot)
            cg = jnp.clip(t + right + 1, 0, n) - jnp.clip(t - left, 0, n)
            cnt = jnp.where(lane_grp == g, cg, cnt)
        p = win[POOL_HALO:POOL_HALO + CHUNK, :]
        diff = (tot / cnt.astype(F32) - p).astype(BF16)
        ob_ref[pl.ds(r0, CHUNK), :] = (_dot(diff, wp_ref[...]) * ps_ref[...]).astype(BF16)
        return carry

    lax.fori_loop(0, n // CHUNK, body, 0)


def _pool_bands():
    rows = CHUNK + 2 * POOL_HALO
    band = np.zeros((len(POOL_WINDOWS), CHUNK, rows), np.float32)
    i = np.arange(CHUNK)[:, None]
    j = np.arange(rows)[None, :]
    for g, w in enumerate(POOL_WINDOWS):
        left = w // 2
        right = w - 1 - left
        band[g] = ((j >= i + POOL_HALO - left) & (j <= i + POOL_HALO + right)).astype(np.float32)
    return jnp.asarray(band, BF16)


def _mix_call(u, va, pin, lw, band):
    nb, n, w = u.shape
    seq = lambda: pl.BlockSpec((None, n, w), lambda b: (b, 0, 0))
    const = lambda a: pl.BlockSpec(a.shape, lambda b: (0,) * a.ndim)
    return pl.pallas_call(
        _mix_kernel,
        grid=(nb,),
        in_specs=[seq(), seq(), seq(), const(lw["w_s"]), const(lw["b_s"]), const(band),
                  const(lw["w_pool"]), const(lw["pool_scale"])],
        out_specs=[seq(), seq()],
        out_shape=[jax.ShapeDtypeStruct((nb, n, w), BF16)] * 2,
        scratch_shapes=[pltpu.VMEM((n + 2 * POOL_HALO, w), F32)],
        compiler_params=_cp(48), name="mixers_ab",
    )(u, va, pin, lw["w_s"], lw["b_s"], band, lw["w_pool"], lw["pool_scale"])


def _attn_kernel(*refs, seg_lens):
    nseg = len(seg_lens)
    qt_ref = refs[0]
    k_refs = refs[1:1 + nseg]
    v_refs = refs[1 + nseg:1 + 2 * nseg]
    o_ref = refs[1 + 2 * nseg]
    ot_ref = refs[2 + 2 * nseg]
    tq = qt_ref.shape[-1]

    def head(hd, carry):
        q = qt_ref[hd]
        m_run = jnp.full((1, tq), -jnp.inf, F32)
        l_run = jnp.zeros((1, tq), F32)
        acc = jnp.zeros((V_DIM, tq), F32)
        for k_ref, v_ref, m in zip(k_refs, v_refs, seg_lens):
            kc = min(KEY_CHUNK, m)
            for c0 in range(0, m, kc):
                s = _dot(k_ref[hd, c0:c0 + kc, :], q) * ATTN_SCALE
                m_new = jnp.maximum(m_run, jnp.max(s, axis=0, keepdims=True))
                alpha = jnp.exp(m_run - m_new)
                p = jnp.exp(s - m_new)
                l_run = alpha * l_run + jnp.sum(p, axis=0, keepdims=True)
                acc = alpha * acc + _dot(v_ref[hd, :, c0:c0 + kc], p.astype(BF16))
                m_run = m_new
        ot_ref[pl.ds(pl.multiple_of(hd * V_DIM, V_DIM), V_DIM), :] = acc / l_run
        return carry

    lax.fori_loop(0, C_HEADS, head, 0)
    o_ref[...] = ot_ref[...].T.astype(BF16)


def _attn_call(qt, ks, vts):
    nb, _, _, n = qt.shape
    tq = TOK_TILE
    seg_lens = tuple(k.shape[2] for k in ks)
    in_specs = [pl.BlockSpec((None, C_HEADS, HEAD_PAD, tq), lambda b, i: (b, 0, 0, i))]
    in_specs += [pl.BlockSpec((None, C_HEADS, m, HEAD_PAD), lambda b, i: (b, 0, 0, 0)) for m in seg_lens]
    in_specs += [pl.BlockSpec((None, C_HEADS, V_DIM, m), lambda b, i: (b, 0, 0, 0)) for m in seg_lens]
    return pl.pallas_call(
        functools.partial(_attn_kernel, seg_lens=seg_lens),
        grid=(nb, n // tq), in_specs=in_specs,
        out_specs=pl.BlockSpec((None, tq, C_WIDTH), lambda b, i: (b, i, 0)),
        out_shape=jax.ShapeDtypeStruct((nb, n, C_WIDTH), BF16),
        scratch_shapes=[pltpu.VMEM((C_WIDTH, tq), F32)],
        compiler_params=_cp(48), name="latent_attention",
    )(qt, *ks, *vts)


def _outproj_kernel(x_ref, oa_ref, ob_ref, oc_ref, wo_ref, g1_ref, sh_ref, sc_ref, ng_ref,
                    wrh_ref, wrl_ref, x1_ref, h_ref, aff_ref, asp_ref):
    mix = (_dot(oa_ref[...], wo_ref[0:A_WIDTH, :])
           + _dot(ob_ref[...], wo_ref[A_WIDTH:A_WIDTH + B_WIDTH, :])
           + _dot(oc_ref[...], wo_ref[A_WIDTH + B_WIDTH:, :]))
    x1 = x_ref[...] + g1_ref[...] * mix
    x1_ref[...] = x1
    h = _rms(x1) * ng_ref[...] * (1.0 + sc_ref[...]) + sh_ref[...]
    h_hi, h_lo = _split2(h)
    h_ref[...] = h_hi
    logits = _dot(h_hi, wrh_ref[...]) + _dot(h_lo, wrh_ref[...]) + _dot(h_hi, wrl_ref[...])
    lane = lax.broadcasted_iota(jnp.int32, logits.shape, 1)
    logits = jnp.where(lane < N_EXPERTS, logits, -jnp.inf)
    ex = jnp.exp(logits - jnp.max(logits, axis=-1, keepdims=True))
    aff = ex / jnp.sum(ex, axis=-1, keepdims=True)
    aff_ref[...] = aff
    a_hi = aff.astype(BF16).astype(F32)
    r1 = aff - a_hi
    a_lo = r1.astype(BF16).astype(F32)
    a_lo2 = r1 - a_lo
    packed = a_hi + pltpu.roll(a_lo, N_EXPERTS, axis=1) + pltpu.roll(a_lo2, 2 * N_EXPERTS, axis=1)
    asp_ref[...] = packed.astype(BF16)


def _outproj_call(x, oa, ob, oc, modl, row0, row_step, lw):
    nb, n, d = x.shape
    tm = TOK_TILE
    tok = lambda c: pl.BlockSpec((None, tm, c), lambda b, i: (b, i, 0))
    const = lambda a: pl.BlockSpec(a.shape, lambda b, i: (0,) * a.ndim)
    modspec = lambda j: pl.BlockSpec((None, None, 1, d), lambda b, i: (row0 + b * row_step, j, 0, 0))
    return pl.pallas_call(
        _outproj_kernel,
        grid=(nb, n // tm),
        in_specs=[tok(d), tok(A_WIDTH), tok(B_WIDTH), tok(C_WIDTH), const(lw["w_out"]),
                  modspec(2), modspec(3), modspec(4), const(lw["norm2_g"]),
                  const(lw["wr_hi"]), const(lw["wr_lo"])],
        out_specs=[tok(d), tok(d), tok(LANES), tok(LANES)],
        out_shape=[jax.ShapeDtypeStruct((nb, n, d), F32), jax.ShapeDtypeStruct((nb, n, d), BF16),
                   jax.ShapeDtypeStruct((nb, n, LANES), F32), jax.ShapeDtypeStruct((nb, n, LANES), BF16)],
        compiler_params=_cp(48), name="out_proj_router",
    )(x, oa, ob, oc, lw["w_out"], modl, modl, modl, lw["norm2_g"], lw["wr_hi"], lw["wr_lo"])


def _route_kernel(aff_ref, pos_ref, post_ref, offs_ref, *, cap):
    n = aff_ref.shape[0]
    c = TOK_TILE
    nck = n // c
    aff = aff_ref[...]
    capf = jnp.float32(cap)

    def search(i, bits):
        cand = bits | jnp.left_shift(jnp.int32(1), (SEARCH_BITS - 1 - i).astype(jnp.int32))
        cnt = jnp.sum(jnp.where(aff >= lax.bitcast_convert_type(cand, F32), 1.0, 0.0),
                      axis=0, keepdims=True)
        return jnp.where(cnt >= capf, cand, bits)

    bits = lax.fori_loop(0, SEARCH_BITS, search, jnp.zeros((1, LANES), jnp.int32))
    thr = lax.bitcast_convert_type(bits, F32)
    n_gt = jnp.sum(jnp.where(aff > thr, 1.0, 0.0), axis=0, keepdims=True)
    n_tie = capf - n_gt
    ri = lax.broadcasted_iota(jnp.int32, (c, c), 0)
    ci = lax.broadcasted_iota(jnp.int32, (c, c), 1)
    tri = jnp.where(ri > ci, 1.0, 0.0).astype(BF16)
    lane_ok = lax.broadcasted_iota(jnp.int32, (c, LANES), 1) < N_EXPERTS
    offs_ref[...] = jnp.zeros(offs_ref.shape, jnp.int32)
    tie_seen = jnp.zeros((1, LANES), F32)
    sel_seen = jnp.zeros((1, LANES), F32)
    for k in range(nck):
        kk = aff[k * c:(k + 1) * c, :]
        gt = kk > thr
        eq = kk == thr
        eqf = jnp.where(eq, 1.0, 0.0)
        tie_rank = _dot(tri, eqf.astype(BF16)) + tie_seen
        sel = (gt | (eq & (tie_rank < n_tie))) & lane_ok
        self_ = jnp.where(sel, 1.0, 0.0)
        slot = _dot(tri, self_.astype(BF16)) + sel_seen
        pos = jnp.where(sel, slot, -1.0)
        pos_ref[k * c:(k + 1) * c, :] = pos
        post_ref[:, k * c:(k + 1) * c] = pos.T[0:N_EXPERTS, :]
        offs_ref[k:k + 1, :] = sel_seen[:, 0:N_EXPERTS].astype(jnp.int32)
        tie_seen = tie_seen + jnp.sum(eqf, axis=0, keepdims=True)
        sel_seen = sel_seen + jnp.sum(self_, axis=0, keepdims=True)
    offs_ref[nck:nck + 1, :] = sel_seen[:, 0:N_EXPERTS].astype(jnp.int32)


def _offs_rows(n):
    return -(-(n // TOK_TILE + 1) // 8) * 8


def _route_call(aff, cap):
    nb, n, _ = aff.shape
    rows = _offs_rows(n)
    return pl.pallas_call(
        functools.partial(_route_kernel, cap=cap),
        grid=(nb,),
        in_specs=[pl.BlockSpec((None, n, LANES), lambda b: (b, 0, 0))],
        out_specs=[pl.BlockSpec((None, n, LANES), lambda b: (b, 0, 0)),
                   pl.BlockSpec((None, N_EXPERTS, n), lambda b: (b, 0, 0)),
                   pl.BlockSpec((None, rows, N_EXPERTS), lambda b: (b, 0, 0))],
        out_shape=[jax.ShapeDtypeStruct((nb, n, LANES), F32),
                   jax.ShapeDtypeStruct((nb, N_EXPERTS, n), F32),
                   jax.ShapeDtypeStruct((nb, rows, N_EXPERTS), jnp.int32)],
        compiler_params=_cp(32), name="ec_route",
    )(aff, )


def _window(offs_ref, b, k, e, rows, win):
    off = offs_ref[(b * rows + k) * N_EXPERTS + e]
    nxt = offs_ref[(b * rows + k + 1) * N_EXPERTS + e]
    base = lax.shift_right_logical(off, SLOT_ALIGN.bit_length() - 1) * SLOT_ALIGN
    nwin = lax.shift_right_logical(nxt - base + (win - 1), win.bit_length() - 1)
    return base, nwin


def _gather_kernel(offs_ref, post_ref, h_ref, asp_ref, xs_ref, gs_ref, *, cap, win, rows):
    b = pl.program_id(0)
    k = pl.program_id(1)
    c = h_ref.shape[0]

    @pl.when(k == 0)
    def _():
        xs_ref[...] = jnp.zeros(xs_ref.shape, xs_ref.dtype)
        gs_ref[...] = jnp.zeros(gs_ref.shape, gs_ref.dtype)

    sub = lax.broadcasted_iota(jnp.int32, (win, c), 0).astype(F32)

    def sel_rows(e, start):
        cb = pl.multiple_of(jnp.minimum(start, cap - win), SLOT_ALIGN)
        prow = post_ref[e:e + 1, :]
        startf = start.astype(F32)
        ok = (prow >= startf) & (prow < startf + win)
        p = jnp.where(ok & (prow == sub + cb.astype(F32)), 1.0, 0.0).astype(BF16)
        return p, cb

    bases = []
    parts = []
    for e in range(N_EXPERTS):
        base, nwin = _window(offs_ref, b, k, e, rows, win)
        p, cb = sel_rows(e, base)
        parts.append(p)
        bases.append((cb, base, nwin))
    p_all = jnp.concatenate(parts, axis=0)
    hk = h_ref[...]
    gk = asp_ref[...]
    r_all = _dot(p_all, hk).astype(BF16)
    g_all = _dot(p_all, gk)
    for e in range(N_EXPERTS):
        cb, base, nwin = bases[e]
        xs_ref[e, pl.ds(cb, win), :] += r_all[e * win:(e + 1) * win, :]
        gs_ref[e, pl.ds(cb, win), :] += g_all[e * win:(e + 1) * win, :]

        def extra(w, carry, e=e, base=base):
            p, cbw = sel_rows(e, base + w * win)
            xs_ref[e, pl.ds(cbw, win), :] += _dot(p, hk).astype(BF16)
            gs_ref[e, pl.ds(cbw, win), :] += _dot(p, gk)
            return carry

        lax.fori_loop(1, nwin, extra, 0)


def _gather_call(offs, post, h, asp, cap, win):
    nb, n, d = h.shape
    c = TOK_TILE
    rows = _offs_rows(n)
    grid_spec = pltpu.PrefetchScalarGridSpec(
        num_scalar_prefetch=1, grid=(nb, n // c),
        in_specs=[pl.BlockSpec((None, N_EXPERTS, c), lambda b, k, o: (b, 0, k)),
                  pl.BlockSpec((None, c, d), lambda b, k, o: (b, k, 0)),
                  pl.BlockSpec((None, c, LANES), lambda b, k, o: (b, k, 0))],
        out_specs=[pl.BlockSpec((N_EXPERTS, cap, d), lambda b, k, o: (0, b, 0)),
                   pl.BlockSpec((N_EXPERTS, cap, LANES), lambda b, k, o: (0, b, 0))])
    return pl.pallas_call(
        functools.partial(_gather_kernel, cap=cap, win=win, rows=rows),
        grid_spec=grid_spec,
        out_shape=[jax.ShapeDtypeStruct((N_EXPERTS, nb * cap, d), BF16),
                   jax.ShapeDtypeStruct((N_EXPERTS, nb * cap, LANES), F32)],
        compiler_params=_cp(48), name="ec_gather",
    )(offs.reshape(-1), post, h, asp)


def _ffn_kernel(xa_ref, xb_ref, ga_ref, gb_ref, w1_ref, w3_ref, w2_ref, ya_ref, yb_ref,
                acca_ref, accb_ref):
    e = pl.program_id(0)
    f = pl.program_id(1)
    w1 = w1_ref[...].astype(BF16)
    w3 = w3_ref[...].astype(BF16)
    w2 = w2_ref[...].astype(BF16)
    for x_ref, g_ref, y_ref, acc_ref in ((xa_ref, ga_ref, ya_ref, acca_ref),
                                         (xb_ref, gb_ref, yb_ref, accb_ref)):
        x = x_ref[...]
        hid = jax.nn.silu(_dot(x, w1)) * _dot(x, w3)
        part = _dot(hid.astype(BF16), w2)

        @pl.when(f == 0)
        def _(acc_ref=acc_ref, part=part):
            acc_ref[...] = part

        @pl.when(f > 0)
        def _(acc_ref=acc_ref, part=part):
            acc_ref[...] += part

        @pl.when(f == pl.num_programs(1) - 1)
        def _(acc_ref=acc_ref, g_ref=g_ref, y_ref=y_ref):
            g = g_ref[...]
            lane = lax.broadcasted_iota(jnp.int32, g.shape, 1)
            keep = ((lane % N_EXPERTS) == e) & (lane < 3 * N_EXPERTS)
            gate = jnp.sum(jnp.where(keep, g, 0.0), axis=-1, keepdims=True)
            y_ref[...] = (acc_ref[...] * gate).astype(BF16)


def _ffn_call(xa, xb, ga, gb, w1, w3, w2):
    ne, ra, d = xa.shape
    rb = xb.shape[1]
    ff = w1.shape[2]
    tf = FF_TILE
    rowsp = lambda r, c: pl.BlockSpec((None, r, c), lambda e, f: (e, 0, 0))
    return pl.pallas_call(
        _ffn_kernel,
        grid=(ne, ff // tf),
        in_specs=[rowsp(ra, d), rowsp(rb, d), rowsp(ra, LANES), rowsp(rb, LANES),
                  pl.BlockSpec((None, d, tf), lambda e, f: (e, 0, f)),
                  pl.BlockSpec((None, d, tf), lambda e, f: (e, 0, f)),
                  pl.BlockSpec((None, tf, d), lambda e, f: (e, f, 0))],
        out_specs=[rowsp(ra, d), rowsp(rb, d)],
        out_shape=[jax.ShapeDtypeStruct((ne, ra, d), BF16), jax.ShapeDtypeStruct((ne, rb, d), BF16)],
        scratch_shapes=[pltpu.VMEM((ra, d), F32), pltpu.VMEM((rb, d), F32)],
        compiler_params=_cp(56), name="expert_swiglu",
    )(xa, xb, ga, gb, w1, w3, w2)


def _combine_kernel(offs_ref, pos_ref, y_ref, x_ref, g2_ref, *rest, cap, win, rows, last):
    if last:
        fg_ref, o_ref, on_ref, acc_ref = rest
    else:
        o_ref, acc_ref = rest
    b = pl.program_id(0)
    k = pl.program_id(1)
    c = x_ref.shape[0]
    per = 256 // win
    depth = per * win
    lane = lax.broadcasted_iota(jnp.int32, (c, depth), 1)
    lane_grp = lane // win
    lane_in = (lane % win).astype(F32)
    lane1 = lax.broadcasted_iota(jnp.int32, (c, win), 1).astype(F32)
    acc_ref[...] = jnp.zeros(acc_ref.shape, F32)
    for g0 in range(0, N_EXPERTS, per):
        pmat = jnp.zeros((c, depth), F32)
        tgt = jnp.zeros((c, depth), F32)
        lo = jnp.zeros((c, depth), F32)
        ys = []
        extras = []
        for j in range(per):
            e = g0 + j
            base, nwin = _window(offs_ref, b, k, e, rows, win)
            cb = pl.multiple_of(jnp.minimum(base, cap - win), SLOT_ALIGN)
            here = lane_grp == j
            pmat = jnp.where(here, pos_ref[:, e:e + 1], pmat)
            tgt = jnp.where(here, lane_in + cb.astype(F32), tgt)
            lo = jnp.where(here, base.astype(F32), lo)
            ys.append(y_ref[e, pl.ds(cb, win), :])
            extras.append((e, base, nwin))
        pt = jnp.where((pmat == tgt) & (pmat >= lo) & (pmat < lo + win), 1.0, 0.0).astype(BF16)
        acc_ref[...] += _dot(pt, jnp.concatenate(ys, axis=0))
        for e, base, nwin in extras:
            def extra(w, carry, e=e, base=base):
                start = base + w * win
                cbw = pl.multiple_of(jnp.minimum(start, cap - win), SLOT_ALIGN)
                pcol = pos_ref[:, e:e + 1]
                startf = start.astype(F32)
                ok = (pcol >= startf) & (pcol < startf + win) & (pcol == lane1 + cbw.astype(F32))
                acc_ref[...] += _dot(jnp.where(ok, 1.0, 0.0).astype(BF16), y_ref[e, pl.ds(cbw, win), :])
                return carry

            lax.fori_loop(1, nwin, extra, 0)
    x2 = x_ref[...] + g2_ref[...] * acc_ref[...]
    o_ref[...] = x2
    if last:
        on_ref[...] = _rms(x2) * fg_ref[...]


def _combine_call(offs, pos, y, x1, modl, row0, row_step, final_g, cap, win):
    nb, n, d = x1.shape
    c = TOK_TILE
    rows = _offs_rows(n)
    last = final_g is not None
    tok = pl.BlockSpec((None, c, d), lambda b, k, o: (b, k, 0))
    in_specs = [pl.BlockSpec((None, c, LANES), lambda b, k, o: (b, k, 0)),
                pl.BlockSpec((N_EXPERTS, cap, d), lambda b, k, o: (0, b, 0)),
                tok,
                pl.BlockSpec((None, None, 1, d), lambda b, k, o: (row0 + b * row_step, 5, 0, 0))]
    ins = [offs.reshape(-1), pos, y, x1, modl]
    if last:
        in_specs.append(pl.BlockSpec((1, d), lambda b, k, o: (0, 0)))
        ins.append(final_g)
    nout = 2 if last else 1
    grid_spec = pltpu.PrefetchScalarGridSpec(
        num_scalar_prefetch=1, grid=(nb, n // c), in_specs=in_specs, out_specs=[tok] * nout,
        scratch_shapes=[pltpu.VMEM((c, d), F32)])
    return pl.pallas_call(
        functools.partial(_combine_kernel, cap=cap, win=win, rows=rows, last=last),
        grid_spec=grid_spec,
        out_shape=[jax.ShapeDtypeStruct((nb, n, d), F32)] * nout,
        compiler_params=_cp(56), name="ec_combine_final" if last else "ec_combine",
    )(*ins)


def _prep_weights(w_in, w_s, b_s, w_pool, pool_scale, q_norm_g, w_qb, kv_norm_g, w_kvb, w_out,
                  w_router, norm1_g, norm2_g):
    depth, d, cols = w_in.shape
    row = lambda a: a.reshape(depth, 1, -1)
    w_in_p = jnp.pad(w_in, ((0, 0), (0, 0), (0, PROJ_PAD - cols))).astype(BF16)
    wq = w_qb.reshape(depth, Q_LORA, C_HEADS, QK_NOPE + QK_ROPE)
    wq = jnp.pad(wq, ((0, 0), (0, 0), (0, 0), (0, HEAD_PAD - QK_NOPE - QK_ROPE)))
    wq_t = wq.reshape(depth, Q_LORA, C_HEADS * HEAD_PAD).transpose(0, 2, 1).astype(BF16)
    wkv = w_kvb.reshape(depth, KV_LORA, C_HEADS, QK_NOPE + V_DIM)
    wkn = jnp.pad(wkv[..., :QK_NOPE], ((0, 0), (0, 0), (0, 0), (0, HEAD_PAD - QK_NOPE)))
    wkn = wkn.reshape(depth, KV_LORA, C_HEADS * HEAD_PAD).astype(BF16)
    wv_t = wkv[..., QK_NOPE:].reshape(depth, KV_LORA, C_WIDTH).transpose(0, 2, 1).astype(BF16)
    b_rep = jnp.repeat(b_s.transpose(0, 2, 1), A_HEAD_DIM, axis=2)
    eye = jnp.eye(len(POOL_WINDOWS), dtype=F32)
    wp_bd = jnp.einsum("gh,lgcd->lgchd", eye, w_pool).reshape(depth, B_WIDTH, B_WIDTH).astype(BF16)
    wr = jnp.pad(w_router, ((0, 0), (0, 0), (0, LANES - N_EXPERTS)))
    wr_hi = wr.astype(BF16)
    wr_lo = (wr - wr_hi.astype(F32)).astype(BF16)
    per_layer = []
    for l in range(depth):
        per_layer.append(dict(
            norm1_g=row(norm1_g)[l], norm2_g=row(norm2_g)[l], w_in=w_in_p[l],
            q_norm_g=row(q_norm_g)[l], kv_norm_g=row(kv_norm_g)[l], wq_t=wq_t[l], wkn=wkn[l],
            wv_t=wv_t[l], w_s=w_s[l].astype(BF16), b_s=b_rep[l], w_pool=wp_bd[l],
            pool_scale=row(pool_scale)[l], w_out=w_out[l].astype(BF16), wr_hi=wr_hi[l], wr_lo=wr_lo[l]))
    return per_layer, wkn, wv_t


def _rope_tables(n):
    rows = n // GRID_W
    row = jnp.repeat(jnp.arange(rows), GRID_W).astype(F32)
    col = jnp.broadcast_to(jnp.arange(GRID_W), (rows, GRID_W)).reshape(-1).astype(F32)
    inv = 1.0 / (ROPE_THETA ** (jnp.arange(0, ROPE_AXIS, 2, dtype=F32) / ROPE_AXIS))
    ang_r = row[:, None] * inv[None, :]
    ang_c = col[:, None] * inv[None, :]
    cos_r, sin_r, cos_c, sin_c = jnp.cos(ang_r), jnp.sin(ang_r), jnp.cos(ang_c), jnp.sin(ang_c)
    pad = jnp.zeros((n, LANES - QK_ROPE), F32)
    cos_k = jnp.concatenate([cos_r, cos_r, cos_c, cos_c, pad], axis=1)
    sin_k = jnp.concatenate([-sin_r, sin_r, -sin_c, sin_c, pad], axis=1)
    return cos_k, sin_k, cos_r.T, sin_r.T, cos_c.T, sin_c.T


def kernel(x_prompt, x_sample, cache_ckv, cache_krope, c, c_ctx, w_ada, b_ada, norm1_g, norm2_g, w_in, w_s, b_s, w_pool, pool_scale, q_norm_g, w_qb, kv_norm_g, w_kvb, w_out, w_router, w_e1, w_e3, w_e2, final_norm_g):
    depth, d, _ = w_ada.shape
    nb_c, n_c, _ = x_prompt.shape
    nb_s, n_s, _ = x_sample.shape
    assert 1 + nb_s <= 8
    cond8 = jnp.concatenate([c_ctx[None, :], c, jnp.zeros((8 - 1 - nb_s, d), F32)], axis=0)
    mod = _mod_call(cond8, w_ada, b_ada).reshape(depth, 8, 6, 1, d)
    layers, wkn_all, wvt_all = _prep_weights(w_in, w_s, b_s, w_pool, pool_scale, q_norm_g, w_qb,
                                             kv_norm_g, w_kvb, w_out, w_router, norm1_g, norm2_g)
    k_cache, vt_cache = _kvexp_call(cache_ckv, cache_krope, wkn_all, wvt_all)
    tabs = _rope_tables(n_s)
    band = _pool_bands()
    final_g = final_norm_g.reshape(1, d)
    cap_c = EC_CAPACITY * n_c // N_EXPERTS
    cap_s = EC_CAPACITY * n_s // N_EXPERTS
    win_c = min(cap_c, 64)
    win_s = min(cap_s, 64)

    xc, xs = x_prompt, x_sample
    yc = ys = None
    ckvs, kropes = [], []
    for l in range(depth):
        lw = layers[l]
        modl = mod[l]
        uc, vac, pinc, qtc, khc, vtc, ckv_c, kr_c = _proj_call(xc, modl, 0, 0, lw, None)
        us, vas, pins, qts, khs, vts, _, _ = _proj_call(xs, modl, 1, 1, lw, tabs)
        ckvs.append(ckv_c)
        kropes.append(kr_c)
        oac, obc = _mix_call(uc, vac, pinc, lw, band)
        oas, obs = _mix_call(us, vas, pins, lw, band)
        occ = _attn_call(qtc, [khc], [vtc])
        ocs = _attn_call(qts, [k_cache[l], khs], [vt_cache[l], vts])
        x1c, hc, affc, aspc = _outproj_call(xc, oac, obc, occ, modl, 0, 0, lw)
        x1s, hs, affs, asps = _outproj_call(xs, oas, obs, ocs, modl, 1, 1, lw)
        posc, postc, offc = _route_call(affc, cap_c)
        poss, posts, offs = _route_call(affs, cap_s)
        xgc, ggc = _gather_call(offc, postc, hc, aspc, cap_c, win_c)
        xgs, ggs = _gather_call(offs, posts, hs, asps, cap_s, win_s)
        ygs, ygc = _ffn_call(xgs, xgc, ggs, ggc, w_e1[l], w_e3[l], w_e2[l])
        fg = final_g if l == depth - 1 else None
        outc = _combine_call(offc, posc, ygc, x1c, modl, 0, 0, fg, cap_c, win_c)
        outs = _combine_call(offs, poss, ygs, x1s, modl, 1, 1, fg, cap_s, win_s)
        xc, xs = outc[0], outs[0]
    yc, ys = outc[1], outs[1]
    new_ckv = jnp.stack(ckvs, axis=1)
    new_krope = jnp.stack(kropes, axis=1)
    return (yc, ys, new_ckv, new_krope)
```

```python
import functools
import math

import numpy as np
import jax
import jax.numpy as jnp
from jax import lax
from jax.experimental import pallas as pl
from jax.experimental.pallas import tpu as pltpu

F32 = jnp.float32
BF16 = jnp.bfloat16

GRID_W = 64
A_HEADS = 4
A_HEAD_DIM = 64
A_WIDTH = A_HEADS * A_HEAD_DIM
CHUNK = 128
POOL_WINDOWS = (2, 4, 8, 16)
POOL_GROUP = 64
B_WIDTH = len(POOL_WINDOWS) * POOL_GROUP
C_HEADS = 8
QK_NOPE = 64
QK_ROPE = 32
V_DIM = 64
Q_LORA = 384
KV_LORA = 256
C_WIDTH = C_HEADS * V_DIM
ROPE_AXIS = QK_ROPE // 2
ROPE_THETA = 10000.0
ATTN_SCALE = (QK_NOPE + QK_ROPE) ** -0.5
N_EXPERTS = 16
EC_CAPACITY = 2
NORM_EPS = 1e-6

LANES = 128
HEAD_PAD = 128
PROJ_PAD = 1536
OFF_P = 2 * A_WIDTH
OFF_CQ = OFF_P + B_WIDTH
OFF_CKV = OFF_CQ + Q_LORA
OFF_KR = OFF_CKV + KV_LORA
POOL_HALO = 8
TOK_TILE = 256
V_ROWS = V_DIM + 16
HEADS_PER_STEP = 2
LOG2_E = math.log2(math.e)
FF_TILE = 512
MOD_TILE = 1536
SEARCH_BITS = 31
SLOT_ALIGN = 16


def _cp(vmem_mb):
    return pltpu.CompilerParams(vmem_limit_bytes=vmem_mb * 1024 * 1024)


def _dot(a, b):
    return jnp.dot(a, b, preferred_element_type=F32)


def _dot_nt(a, b):
    return lax.dot_general(a, b, (((1,), (1,)), ((), ())), preferred_element_type=F32)


def _split2(x):
    hi = x.astype(BF16)
    lo = (x - hi.astype(F32)).astype(BF16)
    return hi, lo


def _rms(x):
    return x * lax.rsqrt(jnp.mean(x * x, axis=-1, keepdims=True) + NORM_EPS)


def _with_ones_row(vt):
    rows = lax.broadcasted_iota(jnp.int32, (V_ROWS - V_DIM, vt.shape[1]), 0)
    return jnp.concatenate([vt, jnp.where(rows == 0, 1.0, 0.0).astype(BF16)], axis=0)


def _mod_kernel(c_ref, w_ref, b_ref, o_ref):
    a = jax.nn.silu(c_ref[...])
    a_hi, a_lo = _split2(a)
    w_hi, w_lo = _split2(w_ref[...])
    o_ref[...] = _dot(a_hi, w_hi) + _dot(a_lo, w_hi) + _dot(a_hi, w_lo) + b_ref[...]


def _mod_call(cond8, w_ada, b_ada):
    depth, d, n = w_ada.shape
    return pl.pallas_call(
        _mod_kernel,
        grid=(depth, n // MOD_TILE),
        in_specs=[pl.BlockSpec((8, d), lambda l, j: (0, 0)),
                  pl.BlockSpec((None, d, MOD_TILE), lambda l, j: (l, 0, j)),
                  pl.BlockSpec((None, 1, MOD_TILE), lambda l, j: (l, 0, j))],
        out_specs=pl.BlockSpec((None, 8, MOD_TILE), lambda l, j: (l, 0, j)),
        out_shape=jax.ShapeDtypeStruct((depth, 8, n), F32),
        compiler_params=_cp(40),
        name="adaln_mod",
    )(cond8, w_ada, b_ada.reshape(depth, 1, n))


def _proj_kernel(*refs, rope):
    if rope:
        (x_ref, sh_ref, sc_ref, g_ref, win_ref, qg_ref, kvg_ref, wq_ref, wkn_ref, wvt_ref,
         ck_ref, sk_ref, cr_ref, sr_ref, cc_ref, scl_ref,
         u_ref, va_ref, pin_ref, qt_ref, kh_ref, vt_ref, ckv_ref, kr_ref) = refs
    else:
        (x_ref, sh_ref, sc_ref, g_ref, win_ref, qg_ref, kvg_ref, wq_ref, wkn_ref, wvt_ref,
         u_ref, va_ref, pin_ref, qt_ref, kh_ref, vt_ref, ckv_ref, kr_ref) = refs
    x = x_ref[...]
    h = _rms(x) * g_ref[...] * (1.0 + sc_ref[...]) + sh_ref[...]
    proj = _dot(h.astype(BF16), win_ref[...])
    u_ref[...] = jax.nn.gelu(proj[:, 0:A_WIDTH])
    va_ref[...] = jax.nn.gelu(proj[:, A_WIDTH:OFF_P]).astype(BF16)
    pin_ref[...] = proj[:, OFF_P:OFF_CQ]
    cq = _rms(proj[:, OFF_CQ:OFF_CKV]) * qg_ref[...]
    ckv = _rms(proj[:, OFF_CKV:OFF_KR]) * kvg_ref[...]
    ckv_ref[...] = ckv
    kr = proj[:, OFF_KR:OFF_KR + LANES]
    kr_ref[...] = kr[:, 0:QK_ROPE]

    qt = _dot_nt(wq_ref[...], cq.astype(BF16))
    ckv_b = ckv.astype(BF16)
    kn = _dot(ckv_b, wkn_ref[...])
    vt = _dot_nt(wvt_ref[...], ckv_b)

    if rope:
        lane = lax.broadcasted_iota(jnp.int32, kr.shape, 1)
        half = ROPE_AXIS // 2
        swapped = jnp.where((lane & half) == 0,
                            pltpu.roll(kr, LANES - half, axis=1), pltpu.roll(kr, half, axis=1))
        kr = kr * ck_ref[...] + swapped * sk_ref[...]
        cr, sr, cc, scl = cr_ref[...], sr_ref[...], cc_ref[...], scl_ref[...]
    kr_shift = pltpu.roll(kr, QK_NOPE, axis=1)

    for hd in range(C_HEADS):
        r0 = hd * HEAD_PAD
        q_h = qt[r0:r0 + HEAD_PAD, :]
        if rope:
            b0 = QK_NOPE
            a, b = q_h[b0:b0 + 8, :], q_h[b0 + 8:b0 + 16, :]
            c, d = q_h[b0 + 16:b0 + 24, :], q_h[b0 + 24:b0 + 32, :]
            q_h = jnp.concatenate(
                [q_h[0:b0, :], a * cr - b * sr, a * sr + b * cr, c * cc - d * scl, c * scl + d * cc,
                 q_h[b0 + 32:, :]], axis=0)
        qt_ref[hd] = q_h.astype(BF16)
        kh_ref[hd] = (kn[:, r0:r0 + HEAD_PAD] + kr_shift).astype(BF16)
        vt_ref[hd] = _with_ones_row(vt[hd * V_DIM:(hd + 1) * V_DIM, :].astype(BF16))


def _proj_call(x, modl, row0, row_step, lw, tabs):
    nb, n, d = x.shape
    tm = TOK_TILE
    rope = tabs is not None
    tok = lambda c: pl.BlockSpec((None, tm, c), lambda b, i: (b, i, 0))
    const = lambda a: pl.BlockSpec(a.shape, lambda b, i: (0,) * a.ndim)
    modspec = lambda j: pl.BlockSpec((None, None, 1, d), lambda b, i: (row0 + b * row_step, j, 0, 0))
    ins = [x, modl, modl, lw["norm1_g"], lw["w_in"], lw["q_norm_g"], lw["kv_norm_g"],
           lw["wq_t"], lw["wkn"], lw["wv_t"]]
    in_specs = [tok(d), modspec(0), modspec(1), const(lw["norm1_g"]), const(lw["w_in"]),
                const(lw["q_norm_g"]), const(lw["kv_norm_g"]), const(lw["wq_t"]), const(lw["wkn"]),
                const(lw["wv_t"])]
    if rope:
        cos_k, sin_k, cos_r, sin_r, cos_c, sin_c = tabs
        ins += [cos_k, sin_k, cos_r, sin_r, cos_c, sin_c]
        in_specs += [pl.BlockSpec((tm, LANES), lambda b, i: (i, 0))] * 2
        in_specs += [pl.BlockSpec((8, tm), lambda b, i: (0, i))] * 4
    out_shape = [jax.ShapeDtypeStruct((nb, n, A_WIDTH), F32),
                 jax.ShapeDtypeStruct((nb, n, A_WIDTH), BF16),
                 jax.ShapeDtypeStruct((nb, n, B_WIDTH), F32),
                 jax.ShapeDtypeStruct((nb, C_HEADS, HEAD_PAD, n), BF16),
                 jax.ShapeDtypeStruct((nb, C_HEADS, n, HEAD_PAD), BF16),
                 jax.ShapeDtypeStruct((nb, C_HEADS, V_ROWS,n), BF16),
                 jax.ShapeDtypeStruct((nb, n, KV_LORA), F32),
                 jax.ShapeDtypeStruct((nb, n, QK_ROPE), F32)]
    out_specs = [tok(A_WIDTH), tok(A_WIDTH), tok(B_WIDTH),
                 pl.BlockSpec((None, C_HEADS, HEAD_PAD, tm), lambda b, i: (b, 0, 0, i)),
                 pl.BlockSpec((None, C_HEADS, tm, HEAD_PAD), lambda b, i: (b, 0, i, 0)),
                 pl.BlockSpec((None, C_HEADS, V_ROWS,tm), lambda b, i: (b, 0, 0, i)),
                 tok(KV_LORA), tok(QK_ROPE)]
    return pl.pallas_call(
        functools.partial(_proj_kernel, rope=rope),
        grid=(nb, n // tm), in_specs=in_specs, out_specs=out_specs, out_shape=out_shape,
        compiler_params=_cp(48), name="in_proj_rope" if rope else "in_proj",
    )(*ins)


def _kvexp_kernel(ckv_ref, kr_ref, wkn_ref, wvt_ref, place_ref, kh_ref, vt_ref):
    ckv_b = ckv_ref[...].astype(BF16)
    kn = _dot(ckv_b, wkn_ref[...])
    vt = _dot_nt(wvt_ref[...], ckv_b)
    kr_placed = _dot(kr_ref[...].astype(BF16), place_ref[...])
    for hd in range(C_HEADS):
        r0 = hd * HEAD_PAD
        kh_ref[hd] = (kn[:, r0:r0 + HEAD_PAD] + kr_placed).astype(BF16)
        vt_ref[hd] = _with_ones_row(vt[hd * V_DIM:(hd + 1) * V_DIM, :].astype(BF16))


def _kvexp_call(cache_ckv, cache_krope, wkn, wv_t):
    nb, depth, m, r = cache_ckv.shape
    place = np.zeros((QK_ROPE, HEAD_PAD), np.float32)
    place[np.arange(QK_ROPE), QK_NOPE + np.arange(QK_ROPE)] = 1.0
    place = jnp.asarray(place, BF16)
    return pl.pallas_call(
        _kvexp_kernel,
        grid=(depth, nb),
        in_specs=[pl.BlockSpec((None, None, m, r), lambda l, b: (b, l, 0, 0)),
                  pl.BlockSpec((None, None, m, QK_ROPE), lambda l, b: (b, l, 0, 0)),
                  pl.BlockSpec((None,) + wkn.shape[1:], lambda l, b: (l, 0, 0)),
                  pl.BlockSpec((None,) + wv_t.shape[1:], lambda l, b: (l, 0, 0)),
                  pl.BlockSpec(place.shape, lambda l, b: (0, 0))],
        out_specs=[pl.BlockSpec((None, None, C_HEADS, m, HEAD_PAD), lambda l, b: (l, b, 0, 0, 0)),
                   pl.BlockSpec((None, None, C_HEADS, V_ROWS,m), lambda l, b: (l, b, 0, 0, 0))],
        out_shape=[jax.ShapeDtypeStruct((depth, nb, C_HEADS, m, HEAD_PAD), BF16),
                   jax.ShapeDtypeStruct((depth, nb, C_HEADS, V_ROWS,m), BF16)],
        compiler_params=_cp(32), name="cache_kv_expand",
    )(cache_ckv, cache_krope, wkn, wv_t, place)


def _mix_kernel(u_ref, va_ref, pin_ref, ws_ref, bs_ref, band_ref, wp_ref, ps_ref,
                oa_ref, ob_ref, pad_ref):
    n = u_ref.shape[0]
    width = u_ref.shape[1]
    zero_rows = jnp.zeros((POOL_HALO, width), F32)
    pad_ref[0:POOL_HALO, :] = zero_rows
    pad_ref[n + POOL_HALO:n + 2 * POOL_HALO, :] = zero_rows
    pad_ref[POOL_HALO:n + POOL_HALO, :] = pin_ref[...]
    lane_grp = lax.broadcasted_iota(jnp.int32, (CHUNK, width), 1) // POOL_GROUP
    row = lax.broadcasted_iota(jnp.int32, (CHUNK, width), 0)

    def body(c, carry):
        r0 = pl.multiple_of(c * CHUNK, CHUNK)
        v = va_ref[pl.ds(r0, CHUNK), :]
        s = bs_ref[...]
        for hd in range(A_HEADS):
            s = s + jnp.where(lane_grp == hd, _dot(ws_ref[hd], v), 0.0)
        oa_ref[pl.ds(r0, CHUNK), :] = (u_ref[pl.ds(r0, CHUNK), :] * s).astype(BF16)
        win = pad_ref[pl.ds(r0, CHUNK + 2 * POOL_HALO), :]
        w_hi, w_lo = _split2(win)
        t = r0 + row
        tot = jnp.zeros((CHUNK, width), F32)
        cnt = jnp.ones((CHUNK, width), jnp.int32)
        for g, w in enumerate(POOL_WINDOWS):
            left = w // 2
            right = w - 1 - left
            sg = _dot(band_ref[g], w_hi) + _dot(band_ref[g], w_lo)
            tot = jnp.where(lane_grp == g, sg, tot)
            cg = jnp.clip(t + right + 1, 0, n) - jnp.clip(t - left, 0, n)
            cnt = jnp.where(lane_grp == g, cg, cnt)
        p = win[POOL_HALO:POOL_HALO + CHUNK, :]
        diff = (tot / cnt.astype(F32) - p).astype(BF16)
        ob_ref[pl.ds(r0, CHUNK), :] = (_dot(diff, wp_ref[...]) * ps_ref[...]).astype(BF16)
        return carry

    lax.fori_loop(0, n // CHUNK, body, 0)


def _pool_bands():
    rows = CHUNK + 2 * POOL_HALO
    band = np.zeros((len(POOL_WINDOWS), CHUNK, rows), np.float32)
    i = np.arange(CHUNK)[:, None]
    j = np.arange(rows)[None, :]
    for g, w in enumerate(POOL_WINDOWS):
        left = w // 2
        right = w - 1 - left
        band[g] = ((j >= i + POOL_HALO - left) & (j <= i + POOL_HALO + right)).astype(np.float32)
    return jnp.asarray(band, BF16)


def _mix_call(u, va, pin, lw, band):
    nb, n, w = u.shape
    seq = lambda: pl.BlockSpec((None, n, w), lambda b: (b, 0, 0))
    const = lambda a: pl.BlockSpec(a.shape, lambda b: (0,) * a.ndim)
    return pl.pallas_call(
        _mix_kernel,
        grid=(nb,),
        in_specs=[seq(), seq(), seq(), const(lw["w_s"]), const(lw["b_s"]), const(band),
                  const(lw["w_pool"]), const(lw["pool_scale"])],
        out_specs=[seq(), seq()],
        out_shape=[jax.ShapeDtypeStruct((nb, n, w), BF16)] * 2,
        scratch_shapes=[pltpu.VMEM((n + 2 * POOL_HALO, w), F32)],
        compiler_params=_cp(48), name="mixers_ab",
    )(u, va, pin, lw["w_s"], lw["b_s"], band, lw["w_pool"], lw["pool_scale"])


def _attn_kernel(*refs, seg_lens):
    nseg = len(seg_lens)
    qt_ref = refs[0]
    k_refs = refs[1:1 + nseg]
    v_refs = refs[1 + nseg:1 + 2 * nseg]
    o_ref = refs[1 + 2 * nseg]
    ot_ref = refs[2 + 2 * nseg]
    tq = qt_ref.shape[-1]

    def one_head(hd):
        q = qt_ref[hd]
        scores = [_dot(k_ref[hd], q) for k_ref in k_refs]
        mx = functools.reduce(jnp.maximum, [jnp.max(s, axis=0, keepdims=True) for s in scores])
        acc = jnp.zeros((V_ROWS, tq), F32)
        for s, v_ref in zip(scores, v_refs):
            p = jnp.exp2((s - mx) * (ATTN_SCALE * LOG2_E)).astype(BF16)
            acc = acc + _dot(v_ref[hd], p)
        ot_ref[pl.ds(pl.multiple_of(hd * V_DIM, V_DIM), V_DIM), :] = acc[0:V_DIM, :] / acc[V_DIM:V_DIM + 1, :]

    def head_group(g, carry):
        for j in range(HEADS_PER_STEP):
            one_head(g * HEADS_PER_STEP + j)
        return carry

    lax.fori_loop(0, C_HEADS // HEADS_PER_STEP, head_group, 0)
    o_ref[...] = ot_ref[...].T.astype(BF16)


def _attn_call(qt, ks, vts):
    nb, _, _, n = qt.shape
    tq = TOK_TILE
    seg_lens = tuple(k.shape[2] for k in ks)
    in_specs = [pl.BlockSpec((None, C_HEADS, HEAD_PAD, tq), lambda b, i: (b, 0, 0, i))]
    in_specs += [pl.BlockSpec((None, C_HEADS, m, HEAD_PAD), lambda b, i: (b, 0, 0, 0)) for m in seg_lens]
    in_specs += [pl.BlockSpec((None, C_HEADS, V_ROWS,m), lambda b, i: (b, 0, 0, 0)) for m in seg_lens]
    return pl.pallas_call(
        functools.partial(_attn_kernel, seg_lens=seg_lens),
        grid=(nb, n // tq), in_specs=in_specs,
        out_specs=pl.BlockSpec((None, tq, C_WIDTH), lambda b, i: (b, i, 0)),
        out_shape=jax.ShapeDtypeStruct((nb, n, C_WIDTH), BF16),
        scratch_shapes=[pltpu.VMEM((C_WIDTH, tq), F32)],
        compiler_params=_cp(48), name="latent_attention",
    )(qt, *ks, *vts)


def _outproj_kernel(x_ref, oa_ref, ob_ref, oc_ref, wo_ref, g1_ref, sh_ref, sc_ref, ng_ref,
                    wrh_ref, wrl_ref, x1_ref, h_ref, aff_ref, asp_ref):
    mix = (_dot(oa_ref[...], wo_ref[0:A_WIDTH, :])
           + _dot(ob_ref[...], wo_ref[A_WIDTH:A_WIDTH + B_WIDTH, :])
           + _dot(oc_ref[...], wo_ref[A_WIDTH + B_WIDTH:, :]))
    x1 = x_ref[...] + g1_ref[...] * mix
    x1_ref[...] = x1
    h = _rms(x1) * ng_ref[...] * (1.0 + sc_ref[...]) + sh_ref[...]
    h_hi, h_lo = _split2(h)
    h_ref[...] = h_hi
    logits = _dot(h_hi, wrh_ref[...]) + _dot(h_lo, wrh_ref[...]) + _dot(h_hi, wrl_ref[...])
    lane = lax.broadcasted_iota(jnp.int32, logits.shape, 1)
    logits = jnp.where(lane < N_EXPERTS, logits, -jnp.inf)
    ex = jnp.exp(logits - jnp.max(logits, axis=-1, keepdims=True))
    aff = ex / jnp.sum(ex, axis=-1, keepdims=True)
    aff_ref[...] = aff
    a_hi = aff.astype(BF16).astype(F32)
    r1 = aff - a_hi
    a_lo = r1.astype(BF16).astype(F32)
    a_lo2 = r1 - a_lo
    packed = a_hi + pltpu.roll(a_lo, N_EXPERTS, axis=1) + pltpu.roll(a_lo2, 2 * N_EXPERTS, axis=1)
    asp_ref[...] = packed.astype(BF16)


def _outproj_call(x, oa, ob, oc, modl, row0, row_step, lw):
    nb, n, d = x.shape
    tm = TOK_TILE
    tok = lambda c: pl.BlockSpec((None, tm, c), lambda b, i: (b, i, 0))
    const = lambda a: pl.BlockSpec(a.shape, lambda b, i: (0,) * a.ndim)
    modspec = lambda j: pl.BlockSpec((None, None, 1, d), lambda b, i: (row0 + b * row_step, j, 0, 0))
    return pl.pallas_call(
        _outproj_kernel,
        grid=(nb, n // tm),
        in_specs=[tok(d), tok(A_WIDTH), tok(B_WIDTH), tok(C_WIDTH), const(lw["w_out"]),
                  modspec(2), modspec(3), modspec(4), const(lw["norm2_g"]),
                  const(lw["wr_hi"]), const(lw["wr_lo"])],
        out_specs=[tok(d), tok(d), tok(LANES), tok(LANES)],
        out_shape=[jax.ShapeDtypeStruct((nb, n, d), F32), jax.ShapeDtypeStruct((nb, n, d), BF16),
                   jax.ShapeDtypeStruct((nb, n, LANES), F32), jax.ShapeDtypeStruct((nb, n, LANES), BF16)],
        compiler_params=_cp(48), name="out_proj_router",
    )(x, oa, ob, oc, lw["w_out"], modl, modl, modl, lw["norm2_g"], lw["wr_hi"], lw["wr_lo"])


def _route_kernel(aff_ref, pos_ref, post_ref, offs_ref, *, cap):
    n = aff_ref.shape[0]
    c = TOK_TILE
    nck = n // c
    aff = aff_ref[...]
    capf = jnp.float32(cap)

    def search(i, bits):
        cand = bits | jnp.left_shift(jnp.int32(1), jnp.int32(SEARCH_BITS - 1) - i)
        cnt = jnp.sum(jnp.where(aff >= lax.bitcast_convert_type(cand, F32), 1.0, 0.0),
                      axis=0, keepdims=True)
        return jnp.where(cnt >= capf, cand, bits)

    bits = lax.fori_loop(0, SEARCH_BITS, search, jnp.zeros((1, LANES), jnp.int32))
    thr = lax.bitcast_convert_type(bits, F32)
    n_gt = jnp.sum(jnp.where(aff > thr, 1.0, 0.0), axis=0, keepdims=True)
    n_tie = capf - n_gt
    ri = lax.broadcasted_iota(jnp.int32, (c, c), 0)
    ci = lax.broadcasted_iota(jnp.int32, (c, c), 1)
    tri = jnp.where(ri > ci, 1.0, 0.0).astype(BF16)
    lane_ok = lax.broadcasted_iota(jnp.int32, (c, LANES), 1) < N_EXPERTS
    offs_ref[...] = jnp.zeros(offs_ref.shape, jnp.int32)
    tie_seen = jnp.zeros((1, LANES), F32)
    sel_seen = jnp.zeros((1, LANES), F32)
    for k in range(nck):
        kk = aff[k * c:(k + 1) * c, :]
        gt = kk > thr
        eq = kk == thr
        eqf = jnp.where(eq, 1.0, 0.0)
        tie_rank = _dot(tri, eqf.astype(BF16)) + tie_seen
        sel = (gt | (eq & (tie_rank < n_tie))) & lane_ok
        self_ = jnp.where(sel, 1.0, 0.0)
        slot = _dot(tri, self_.astype(BF16)) + sel_seen
        pos = jnp.where(sel, slot, -1.0)
        pos_ref[k * c:(k + 1) * c, :] = pos
        post_ref[:, k * c:(k + 1) * c] = pos.T[0:N_EXPERTS, :]
        offs_ref[k:k + 1, :] = sel_seen[:, 0:N_EXPERTS].astype(jnp.int32)
        tie_seen = tie_seen + jnp.sum(eqf, axis=0, keepdims=True)
        sel_seen = sel_seen + jnp.sum(self_, axis=0, keepdims=True)
    offs_ref[nck:nck + 1, :] = sel_seen[:, 0:N_EXPERTS].astype(jnp.int32)


def _offs_rows(n):
    return -(-(n // TOK_TILE + 1) // 8) * 8


def _route_call(aff, cap):
    nb, n, _ = aff.shape
    rows = _offs_rows(n)
    return pl.pallas_call(
        functools.partial(_route_kernel, cap=cap),
        grid=(nb,),
        in_specs=[pl.BlockSpec((None, n, LANES), lambda b: (b, 0, 0))],
        out_specs=[pl.BlockSpec((None, n, LANES), lambda b: (b, 0, 0)),
                   pl.BlockSpec((None, N_EXPERTS, n), lambda b: (b, 0, 0)),
                   pl.BlockSpec((None, rows, N_EXPERTS), lambda b: (b, 0, 0))],
        out_shape=[jax.ShapeDtypeStruct((nb, n, LANES), F32),
                   jax.ShapeDtypeStruct((nb, N_EXPERTS, n), F32),
                   jax.ShapeDtypeStruct((nb, rows, N_EXPERTS), jnp.int32)],
        compiler_params=_cp(32), name="ec_route",
    )(aff, )


def _window(offs_ref, b, k, e, rows, win):
    off = offs_ref[(b * rows + k) * N_EXPERTS + e]
    nxt = offs_ref[(b * rows + k + 1) * N_EXPERTS + e]
    base = lax.shift_right_logical(off, SLOT_ALIGN.bit_length() - 1) * SLOT_ALIGN
    nwin = lax.shift_right_logical(nxt - base + (win - 1), win.bit_length() - 1)
    return base, nwin


def _gather_kernel(offs_ref, post_ref, h_ref, asp_ref, xs_ref, gs_ref, *, cap, win, rows):
    b = pl.program_id(0)
    k = pl.program_id(1)
    c = h_ref.shape[0]

    @pl.when(k == 0)
    def _():
        xs_ref[...] = jnp.zeros(xs_ref.shape, xs_ref.dtype)
        gs_ref[...] = jnp.zeros(gs_ref.shape, gs_ref.dtype)

    sub = lax.broadcasted_iota(jnp.int32, (win, c), 0).astype(F32)

    def sel_rows(e, start):
        cb = pl.multiple_of(jnp.minimum(start, cap - win), SLOT_ALIGN)
        prow = post_ref[e:e + 1, :]
        startf = start.astype(F32)
        ok = (prow >= startf) & (prow < startf + win)
        p = jnp.where(ok & (prow == sub + cb.astype(F32)), 1.0, 0.0).astype(BF16)
        return p, cb

    bases = []
    parts = []
    for e in range(N_EXPERTS):
        base, nwin = _window(offs_ref, b, k, e, rows, win)
        p, cb = sel_rows(e, base)
        parts.append(p)
        bases.append((cb, base, nwin))
    p_all = jnp.concatenate(parts, axis=0)
    hk = h_ref[...]
    gk = asp_ref[...]
    r_all = _dot(p_all, hk).astype(BF16)
    g_all = _dot(p_all, gk)
    for e in range(N_EXPERTS):
        cb, base, nwin = bases[e]
        xs_ref[e, pl.ds(cb, win), :] += r_all[e * win:(e + 1) * win, :]
        gs_ref[e, pl.ds(cb, win), :] += g_all[e * win:(e + 1) * win, :]

        def extra(w, carry, e=e, base=base):
            p, cbw = sel_rows(e, base + w * win)
            xs_ref[e, pl.ds(cbw, win), :] += _dot(p, hk).astype(BF16)
            gs_ref[e, pl.ds(cbw, win), :] += _dot(p, gk)
            return carry

        lax.fori_loop(1, nwin, extra, 0)


def _gather_call(offs, post, h, asp, cap, win):
    nb, n, d = h.shape
    c = TOK_TILE
    rows = _offs_rows(n)
    grid_spec = pltpu.PrefetchScalarGridSpec(
        num_scalar_prefetch=1, grid=(nb, n // c),
        in_specs=[pl.BlockSpec((None, N_EXPERTS, c), lambda b, k, o: (b, 0, k)),
                  pl.BlockSpec((None, c, d), lambda b, k, o: (b, k, 0)),
                  pl.BlockSpec((None, c, LANES), lambda b, k, o: (b, k, 0))],
        out_specs=[pl.BlockSpec((N_EXPERTS, cap, d), lambda b, k, o: (0, b, 0)),
                   pl.BlockSpec((N_EXPERTS, cap, LANES), lambda b, k, o: (0, b, 0))])
    return pl.pallas_call(
        functools.partial(_gather_kernel, cap=cap, win=win, rows=rows),
        grid_spec=grid_spec,
        out_shape=[jax.ShapeDtypeStruct((N_EXPERTS, nb * cap, d), BF16),
                   jax.ShapeDtypeStruct((N_EXPERTS, nb * cap, LANES), F32)],
        compiler_params=_cp(48), name="ec_gather",
    )(offs.reshape(-1), post, h, asp)


def _ffn_kernel(xa_ref, xb_ref, ga_ref, gb_ref, w1_ref, w3_ref, w2_ref, ya_ref, yb_ref,
                acca_ref, accb_ref):
    e = pl.program_id(0)
    f = pl.program_id(1)
    w1 = w1_ref[...].astype(BF16)
    w3 = w3_ref[...].astype(BF16)
    w2 = w2_ref[...].astype(BF16)
    for x_ref, g_ref, y_ref, acc_ref in ((xa_ref, ga_ref, ya_ref, acca_ref),
                                         (xb_ref, gb_ref, yb_ref, accb_ref)):
        x = x_ref[...]
        hid = jax.nn.silu(_dot(x, w1)) * _dot(x, w3)
        part = _dot(hid.astype(BF16), w2)

        @pl.when(f == 0)
        def _(acc_ref=acc_ref, part=part):
            acc_ref[...] = part

        @pl.when(f > 0)
        def _(acc_ref=acc_ref, part=part):
            acc_ref[...] += part

        @pl.when(f == pl.num_programs(1) - 1)
        def _(acc_ref=acc_ref, g_ref=g_ref, y_ref=y_ref):
            g = g_ref[...]
            lane = lax.broadcasted_iota(jnp.int32, g.shape, 1)
            keep = ((lane % N_EXPERTS) == e) & (lane < 3 * N_EXPERTS)
            gate = jnp.sum(jnp.where(keep, g, 0.0), axis=-1, keepdims=True)
            y_ref[...] = (acc_ref[...] * gate).astype(BF16)


def _ffn_call(xa, xb, ga, gb, w1, w3, w2, layer):
    ne, ra, d = xa.shape
    rb = xb.shape[1]
    ff = w1.shape[3]
    tf = FF_TILE
    rowsp = lambda r, c: pl.BlockSpec((None, r, c), lambda e, f: (e, 0, 0))
    return pl.pallas_call(
        _ffn_kernel,
        grid=(ne, ff // tf),
        in_specs=[rowsp(ra, d), rowsp(rb, d), rowsp(ra, LANES), rowsp(rb, LANES),
                  pl.BlockSpec((None, None, d, tf), lambda e, f: (layer, e, 0, f)),
                  pl.BlockSpec((None, None, d, tf), lambda e, f: (layer, e, 0, f)),
                  pl.BlockSpec((None, None, tf, d), lambda e, f: (layer, e, f, 0))],
        out_specs=[rowsp(ra, d), rowsp(rb, d)],
        out_shape=[jax.ShapeDtypeStruct((ne, ra, d), BF16), jax.ShapeDtypeStruct((ne, rb, d), BF16)],
        scratch_shapes=[pltpu.VMEM((ra, d), F32), pltpu.VMEM((rb, d), F32)],
        compiler_params=_cp(56), name="expert_swiglu",
    )(xa, xb, ga, gb, w1, w3, w2)


def _combine_kernel(offs_ref, pos_ref, y_ref, x_ref, g2_ref, *rest, cap, win, rows, last):
    if last:
        fg_ref, o_ref, on_ref, acc_ref = rest
    else:
        o_ref, acc_ref = rest
    b = pl.program_id(0)
    k = pl.program_id(1)
    c = x_ref.shape[0]
    per = 256 // win
    depth = per * win
    lane = lax.broadcasted_iota(jnp.int32, (c, depth), 1)
    lane_grp = lane // win
    lane_in = (lane % win).astype(F32)
    lane1 = lax.broadcasted_iota(jnp.int32, (c, win), 1).astype(F32)
    acc_ref[...] = jnp.zeros(acc_ref.shape, F32)
    for g0 in range(0, N_EXPERTS, per):
        pmat = jnp.zeros((c, depth), F32)
        tgt = jnp.zeros((c, depth), F32)
        lo = jnp.zeros((c, depth), F32)
        ys = []
        extras = []
        for j in range(per):
            e = g0 + j
            base, nwin = _window(offs_ref, b, k, e, rows, win)
            cb = pl.multiple_of(jnp.minimum(base, cap - win), SLOT_ALIGN)
            here = lane_grp == j
            pmat = jnp.where(here, pos_ref[:, e:e + 1], pmat)
            tgt = jnp.where(here, lane_in + cb.astype(F32), tgt)
            lo = jnp.where(here, base.astype(F32), lo)
            ys.append(y_ref[e, pl.ds(cb, win), :])
            extras.append((e, base, nwin))
        pt = jnp.where((pmat == tgt) & (pmat >= lo) & (pmat < lo + win), 1.0, 0.0).astype(BF16)
        acc_ref[...] += _dot(pt, jnp.concatenate(ys, axis=0))
        for e, base, nwin in extras:
            def extra(w, carry, e=e, base=base):
                start = base + w * win
                cbw = pl.multiple_of(jnp.minimum(start, cap - win), SLOT_ALIGN)
                pcol = pos_ref[:, e:e + 1]
                startf = start.astype(F32)
                ok = (pcol >= startf) & (pcol < startf + win) & (pcol == lane1 + cbw.astype(F32))
                acc_ref[...] += _dot(jnp.where(ok, 1.0, 0.0).astype(BF16), y_ref[e, pl.ds(cbw, win), :])
                return carry

            lax.fori_loop(1, nwin, extra, 0)
    x2 = x_ref[...] + g2_ref[...] * acc_ref[...]
    o_ref[...] = x2
    if last:
        on_ref[...] = _rms(x2) * fg_ref[...]


def _combine_call(offs, pos, y, x1, modl, row0, row_step, final_g, cap, win):
    nb, n, d = x1.shape
    c = TOK_TILE
    rows = _offs_rows(n)
    last = final_g is not None
    tok = pl.BlockSpec((None, c, d), lambda b, k, o: (b, k, 0))
    in_specs = [pl.BlockSpec((None, c, LANES), lambda b, k, o: (b, k, 0)),
                pl.BlockSpec((N_EXPERTS, cap, d), lambda b, k, o: (0, b, 0)),
                tok,
                pl.BlockSpec((None, None, 1, d), lambda b, k, o: (row0 + b * row_step, 5, 0, 0))]
    ins = [offs.reshape(-1), pos, y, x1, modl]
    if last:
        in_specs.append(pl.BlockSpec((1, d), lambda b, k, o: (0, 0)))
        ins.append(final_g)
    nout = 2 if last else 1
    grid_spec = pltpu.PrefetchScalarGridSpec(
        num_scalar_prefetch=1, grid=(nb, n // c), in_specs=in_specs, out_specs=[tok] * nout,
        scratch_shapes=[pltpu.VMEM((c, d), F32)])
    return pl.pallas_call(
        functools.partial(_combine_kernel, cap=cap, win=win, rows=rows, last=last),
        grid_spec=grid_spec,
        out_shape=[jax.ShapeDtypeStruct((nb, n, d), F32)] * nout,
        compiler_params=_cp(56), name="ec_combine_final" if last else "ec_combine",
    )(*ins)


def _prep_weights(w_in, w_s, b_s, w_pool, pool_scale, q_norm_g, w_qb, kv_norm_g, w_kvb, w_out,
                  w_router, norm1_g, norm2_g):
    depth, d, cols = w_in.shape
    row = lambda a: a.reshape(depth, 1, -1)
    w_in_p = jnp.pad(w_in, ((0, 0), (0, 0), (0, PROJ_PAD - cols))).astype(BF16)
    wq = w_qb.reshape(depth, Q_LORA, C_HEADS, QK_NOPE + QK_ROPE)
    wq = jnp.pad(wq, ((0, 0), (0, 0), (0, 0), (0, HEAD_PAD - QK_NOPE - QK_ROPE)))
    wq_t = wq.reshape(depth, Q_LORA, C_HEADS * HEAD_PAD).transpose(0, 2, 1).astype(BF16)
    wkv = w_kvb.reshape(depth, KV_LORA, C_HEADS, QK_NOPE + V_DIM)
    wkn = jnp.pad(wkv[..., :QK_NOPE], ((0, 0), (0, 0), (0, 0), (0, HEAD_PAD - QK_NOPE)))
    wkn = wkn.reshape(depth, KV_LORA, C_HEADS * HEAD_PAD).astype(BF16)
    wv_t = wkv[..., QK_NOPE:].reshape(depth, KV_LORA, C_WIDTH).transpose(0, 2, 1).astype(BF16)
    b_rep = jnp.repeat(b_s.transpose(0, 2, 1), A_HEAD_DIM, axis=2)
    eye = jnp.eye(len(POOL_WINDOWS), dtype=F32)
    wp_bd = jnp.einsum("gh,lgcd->lgchd", eye, w_pool).reshape(depth, B_WIDTH, B_WIDTH).astype(BF16)
    wr = jnp.pad(w_router, ((0, 0), (0, 0), (0, LANES - N_EXPERTS)))
    wr_hi = wr.astype(BF16)
    wr_lo = (wr - wr_hi.astype(F32)).astype(BF16)
    per_layer = []
    for l in range(depth):
        per_layer.append(dict(
            norm1_g=row(norm1_g)[l], norm2_g=row(norm2_g)[l], w_in=w_in_p[l],
            q_norm_g=row(q_norm_g)[l], kv_norm_g=row(kv_norm_g)[l], wq_t=wq_t[l], wkn=wkn[l],
            wv_t=wv_t[l], w_s=w_s[l].astype(BF16), b_s=b_rep[l], w_pool=wp_bd[l],
            pool_scale=row(pool_scale)[l], w_out=w_out[l].astype(BF16), wr_hi=wr_hi[l], wr_lo=wr_lo[l]))
    return per_layer, wkn, wv_t


def _rope_tables(n):
    rows = n // GRID_W
    row = jnp.repeat(jnp.arange(rows), GRID_W).astype(F32)
    col = jnp.broadcast_to(jnp.arange(GRID_W), (rows, GRID_W)).reshape(-1).astype(F32)
    inv = 1.0 / (ROPE_THETA ** (jnp.arange(0, ROPE_AXIS, 2, dtype=F32) / ROPE_AXIS))
    ang_r = row[:, None] * inv[None, :]
    ang_c = col[:, None] * inv[None, :]
    cos_r, sin_r, cos_c, sin_c = jnp.cos(ang_r), jnp.sin(ang_r), jnp.cos(ang_c), jnp.sin(ang_c)
    pad = jnp.zeros((n, LANES - QK_ROPE), F32)
    cos_k = jnp.concatenate([cos_r, cos_r, cos_c, cos_c, pad], axis=1)
    sin_k = jnp.concatenate([-sin_r, sin_r, -sin_c, sin_c, pad], axis=1)
    return cos_k, sin_k, cos_r.T, sin_r.T, cos_c.T, sin_c.T


def kernel(x_prompt, x_sample, cache_ckv, cache_krope, c, c_ctx, w_ada, b_ada, norm1_g, norm2_g, w_in, w_s, b_s, w_pool, pool_scale, q_norm_g, w_qb, kv_norm_g, w_kvb, w_out, w_router, w_e1, w_e3, w_e2, final_norm_g):
    depth, d, _ = w_ada.shape
    nb_c, n_c, _ = x_prompt.shape
    nb_s, n_s, _ = x_sample.shape
    assert 1 + nb_s <= 8
    cond8 = jnp.concatenate([c_ctx[None, :], c, jnp.zeros((8 - 1 - nb_s, d), F32)], axis=0)
    mod = _mod_call(cond8, w_ada, b_ada).reshape(depth, 8, 6, 1, d)
    layers, wkn_all, wvt_all = _prep_weights(w_in, w_s, b_s, w_pool, pool_scale, q_norm_g, w_qb,
                                             kv_norm_g, w_kvb, w_out, w_router, norm1_g, norm2_g)
    k_cache, vt_cache = _kvexp_call(cache_ckv, cache_krope, wkn_all, wvt_all)
    tabs = _rope_tables(n_s)
    band = _pool_bands()
    final_g = final_norm_g.reshape(1, d)
    cap_c = EC_CAPACITY * n_c // N_EXPERTS
    cap_s = EC_CAPACITY * n_s // N_EXPERTS
    win_c = min(cap_c, 64)
    win_s = min(cap_s, 64)

    xc, xs = x_prompt, x_sample
    yc = ys = None
    ckvs, kropes = [], []
    for l in range(depth):
        lw = layers[l]
        modl = mod[l]
        uc, vac, pinc, qtc, khc, vtc, ckv_c, kr_c = _proj_call(xc, modl, 0, 0, lw, None)
        us, vas, pins, qts, khs, vts, _, _ = _proj_call(xs, modl, 1, 1, lw, tabs)
        ckvs.append(ckv_c)
        kropes.append(kr_c)
        oac, obc = _mix_call(uc, vac, pinc, lw, band)
        oas, obs = _mix_call(us, vas, pins, lw, band)
        occ = _attn_call(qtc, [khc], [vtc])
        ocs = _attn_call(qts, [k_cache[l], khs], [vt_cache[l], vts])
        x1c, hc, affc, aspc = _outproj_call(xc, oac, obc, occ, modl, 0, 0, lw)
        x1s, hs, affs, asps = _outproj_call(xs, oas, obs, ocs, modl, 1, 1, lw)
        posc, postc, offc = _route_call(affc, cap_c)
        poss, posts, offs = _route_call(affs, cap_s)
        xgc, ggc = _gather_call(offc, postc, hc, aspc, cap_c, win_c)
        xgs, ggs = _gather_call(offs, posts, hs, asps, cap_s, win_s)
        ygs, ygc = _ffn_call(xgs, xgc, ggs, ggc, w_e1, w_e3, w_e2, l)
        fg = final_g if l == depth - 1 else None
        outc = _combine_call(offc, posc, ygc, x1c, modl, 0, 0, fg, cap_c, win_c)
        outs = _combine_call(offs, poss, ygs, x1s, modl, 1, 1, fg, cap_s, win_s)
        xc, xs = outc[0], outs[0]
    yc, ys = outc[1], outs[1]
    new_ckv = jnp.stack(ckvs, axis=1)
    new_krope = jnp.stack(kropes, axis=1)
    return (yc, ys, new_ckv, new_krope)
```

```python
import functools
import math

import numpy as np
import jax
import jax.numpy as jnp
from jax import lax
from jax.experimental import pallas as pl
from jax.experimental.pallas import tpu as pltpu

F32 = jnp.float32
BF16 = jnp.bfloat16

GRID_W = 64
A_HEADS = 4
A_HEAD_DIM = 64
A_WIDTH = A_HEADS * A_HEAD_DIM
CHUNK = 128
POOL_WINDOWS = (2, 4, 8, 16)
POOL_GROUP = 64
B_WIDTH = len(POOL_WINDOWS) * POOL_GROUP
C_HEADS = 8
QK_NOPE = 64
QK_ROPE = 32
V_DIM = 64
Q_LORA = 384
KV_LORA = 256
C_WIDTH = C_HEADS * V_DIM
ROPE_AXIS = QK_ROPE // 2
ROPE_THETA = 10000.0
ATTN_SCALE = (QK_NOPE + QK_ROPE) ** -0.5
N_EXPERTS = 16
EC_CAPACITY = 2
NORM_EPS = 1e-6

LANES = 128
HEAD_PAD = 128
PROJ_PAD = 1536
OFF_P = 2 * A_WIDTH
OFF_CQ = OFF_P + B_WIDTH
OFF_CKV = OFF_CQ + Q_LORA
OFF_KR = OFF_CKV + KV_LORA
POOL_HALO = 8
TOK_TILE = 256
V_ROWS = V_DIM + 16
KEY_CHUNK = 512
Q_PRESCALE = ATTN_SCALE * math.log2(math.e)
FF_TILE = 512
MOD_TILE = 1536
SEARCH_BITS = 31
SLOT_ALIGN = 16


def _cp(vmem_mb):
    return pltpu.CompilerParams(vmem_limit_bytes=vmem_mb * 1024 * 1024)


def _dot(a, b):
    return jnp.dot(a, b, preferred_element_type=F32)


def _dot_nt(a, b):
    return lax.dot_general(a, b, (((1,), (1,)), ((), ())), preferred_element_type=F32)


def _split2(x):
    hi = x.astype(BF16)
    lo = (x - hi.astype(F32)).astype(BF16)
    return hi, lo


def _rms(x):
    return x * lax.rsqrt(jnp.mean(x * x, axis=-1, keepdims=True) + NORM_EPS)


def _with_ones_row(vt):
    rows = lax.broadcasted_iota(jnp.int32, (V_ROWS - V_DIM, vt.shape[1]), 0)
    return jnp.concatenate([vt, jnp.where(rows == 0, 1.0, 0.0).astype(BF16)], axis=0)


def _mod_kernel(c_ref, w_ref, b_ref, o_ref):
    a = jax.nn.silu(c_ref[...])
    a_hi, a_lo = _split2(a)
    w_hi, w_lo = _split2(w_ref[...])
    o_ref[...] = _dot(a_hi, w_hi) + _dot(a_lo, w_hi) + _dot(a_hi, w_lo) + b_ref[...]


def _mod_call(cond8, w_ada, b_ada):
    depth, d, n = w_ada.shape
    return pl.pallas_call(
        _mod_kernel,
        grid=(depth, n // MOD_TILE),
        in_specs=[pl.BlockSpec((8, d), lambda l, j: (0, 0)),
                  pl.BlockSpec((None, d, MOD_TILE), lambda l, j: (l, 0, j)),
                  pl.BlockSpec((None, 1, MOD_TILE), lambda l, j: (l, 0, j))],
        out_specs=pl.BlockSpec((None, 8, MOD_TILE), lambda l, j: (l, 0, j)),
        out_shape=jax.ShapeDtypeStruct((depth, 8, n), F32),
        compiler_params=_cp(40),
        name="adaln_mod",
    )(cond8, w_ada, b_ada.reshape(depth, 1, n))


def _proj_kernel(*refs, rope):
    if rope:
        (x_ref, sh_ref, sc_ref, g_ref, win_ref, qg_ref, kvg_ref, wq_ref, wkn_ref, wvt_ref,
         ck_ref, sk_ref, cr_ref, sr_ref, cc_ref, scl_ref,
         u_ref, va_ref, pin_ref, qt_ref, kh_ref, vt_ref, ckv_ref, kr_ref) = refs
    else:
        (x_ref, sh_ref, sc_ref, g_ref, win_ref, qg_ref, kvg_ref, wq_ref, wkn_ref, wvt_ref,
         u_ref, va_ref, pin_ref, qt_ref, kh_ref, vt_ref, ckv_ref, kr_ref) = refs
    x = x_ref[...]
    h = _rms(x) * g_ref[...] * (1.0 + sc_ref[...]) + sh_ref[...]
    proj = _dot(h.astype(BF16), win_ref[...])
    u_ref[...] = jax.nn.gelu(proj[:, 0:A_WIDTH])
    va_ref[...] = jax.nn.gelu(proj[:, A_WIDTH:OFF_P]).astype(BF16)
    pin_ref[...] = proj[:, OFF_P:OFF_CQ]
    cq = _rms(proj[:, OFF_CQ:OFF_CKV]) * qg_ref[...]
    ckv = _rms(proj[:, OFF_CKV:OFF_KR]) * kvg_ref[...]
    ckv_ref[...] = ckv
    kr = proj[:, OFF_KR:OFF_KR + LANES]
    kr_ref[...] = kr[:, 0:QK_ROPE]

    qt = _dot_nt(wq_ref[...], cq.astype(BF16))
    ckv_b = ckv.astype(BF16)
    kn = _dot(ckv_b, wkn_ref[...])
    vt = _dot_nt(wvt_ref[...], ckv_b)

    if rope:
        lane = lax.broadcasted_iota(jnp.int32, kr.shape, 1)
        half = ROPE_AXIS // 2
        swapped = jnp.where((lane & half) == 0,
                            pltpu.roll(kr, LANES - half, axis=1), pltpu.roll(kr, half, axis=1))
        kr = kr * ck_ref[...] + swapped * sk_ref[...]
        cr, sr, cc, scl = cr_ref[...], sr_ref[...], cc_ref[...], scl_ref[...]
    kr_shift = pltpu.roll(kr, QK_NOPE, axis=1)

    for hd in range(C_HEADS):
        r0 = hd * HEAD_PAD
        q_h = qt[r0:r0 + HEAD_PAD, :]
        if rope:
            b0 = QK_NOPE
            a, b = q_h[b0:b0 + 8, :], q_h[b0 + 8:b0 + 16, :]
            c, d = q_h[b0 + 16:b0 + 24, :], q_h[b0 + 24:b0 + 32, :]
            q_h = jnp.concatenate(
                [q_h[0:b0, :], a * cr - b * sr, a * sr + b * cr, c * cc - d * scl, c * scl + d * cc,
                 q_h[b0 + 32:, :]], axis=0)
        qt_ref[hd] = (q_h * Q_PRESCALE).astype(BF16)
        kh_ref[hd] = (kn[:, r0:r0 + HEAD_PAD] + kr_shift).astype(BF16)
        vt_ref[hd] = _with_ones_row(vt[hd * V_DIM:(hd + 1) * V_DIM, :].astype(BF16))


def _proj_call(x, modl, row0, row_step, lw, tabs):
    nb, n, d = x.shape
    tm = TOK_TILE
    rope = tabs is not None
    tok = lambda c: pl.BlockSpec((None, tm, c), lambda b, i: (b, i, 0))
    const = lambda a: pl.BlockSpec(a.shape, lambda b, i: (0,) * a.ndim)
    modspec = lambda j: pl.BlockSpec((None, None, 1, d), lambda b, i: (row0 + b * row_step, j, 0, 0))
    ins = [x, modl, modl, lw["norm1_g"], lw["w_in"], lw["q_norm_g"], lw["kv_norm_g"],
           lw["wq_t"], lw["wkn"], lw["wv_t"]]
    in_specs = [tok(d), modspec(0), modspec(1), const(lw["norm1_g"]), const(lw["w_in"]),
                const(lw["q_norm_g"]), const(lw["kv_norm_g"]), const(lw["wq_t"]), const(lw["wkn"]),
                const(lw["wv_t"])]
    if rope:
        cos_k, sin_k, cos_r, sin_r, cos_c, sin_c = tabs
        ins += [cos_k, sin_k, cos_r, sin_r, cos_c, sin_c]
        in_specs += [pl.BlockSpec((tm, LANES), lambda b, i: (i, 0))] * 2
        in_specs += [pl.BlockSpec((8, tm), lambda b, i: (0, i))] * 4
    out_shape = [jax.ShapeDtypeStruct((nb, n, A_WIDTH), F32),
                 jax.ShapeDtypeStruct((nb, n, A_WIDTH), BF16),
                 jax.ShapeDtypeStruct((nb, n, B_WIDTH), F32),
                 jax.ShapeDtypeStruct((nb, C_HEADS, HEAD_PAD, n), BF16),
                 jax.ShapeDtypeStruct((nb, C_HEADS, n, HEAD_PAD), BF16),
                 jax.ShapeDtypeStruct((nb, C_HEADS, V_ROWS,n), BF16),
                 jax.ShapeDtypeStruct((nb, n, KV_LORA), F32),
                 jax.ShapeDtypeStruct((nb, n, QK_ROPE), F32)]
    out_specs = [tok(A_WIDTH), tok(A_WIDTH), tok(B_WIDTH),
                 pl.BlockSpec((None, C_HEADS, HEAD_PAD, tm), lambda b, i: (b, 0, 0, i)),
                 pl.BlockSpec((None, C_HEADS, tm, HEAD_PAD), lambda b, i: (b, 0, i, 0)),
                 pl.BlockSpec((None, C_HEADS, V_ROWS,tm), lambda b, i: (b, 0, 0, i)),
                 tok(KV_LORA), tok(QK_ROPE)]
    return pl.pallas_call(
        functools.partial(_proj_kernel, rope=rope),
        grid=(nb, n // tm), in_specs=in_specs, out_specs=out_specs, out_shape=out_shape,
        compiler_params=_cp(48), name="in_proj_rope" if rope else "in_proj",
    )(*ins)


def _kvexp_kernel(ckv_ref, kr_ref, wkn_ref, wvt_ref, place_ref, kh_ref, vt_ref):
    ckv_b = ckv_ref[...].astype(BF16)
    kn = _dot(ckv_b, wkn_ref[...])
    vt = _dot_nt(wvt_ref[...], ckv_b)
    kr_placed = _dot(kr_ref[...].astype(BF16), place_ref[...])
    for hd in range(C_HEADS):
        r0 = hd * HEAD_PAD
        kh_ref[hd] = (kn[:, r0:r0 + HEAD_PAD] + kr_placed).astype(BF16)
        vt_ref[hd] = _with_ones_row(vt[hd * V_DIM:(hd + 1) * V_DIM, :].astype(BF16))


def _kvexp_call(cache_ckv, cache_krope, wkn, wv_t):
    nb, depth, m, r = cache_ckv.shape
    place = np.zeros((QK_ROPE, HEAD_PAD), np.float32)
    place[np.arange(QK_ROPE), QK_NOPE + np.arange(QK_ROPE)] = 1.0
    place = jnp.asarray(place, BF16)
    return pl.pallas_call(
        _kvexp_kernel,
        grid=(depth, nb),
        in_specs=[pl.BlockSpec((None, None, m, r), lambda l, b: (b, l, 0, 0)),
                  pl.BlockSpec((None, None, m, QK_ROPE), lambda l, b: (b, l, 0, 0)),
                  pl.BlockSpec((None,) + wkn.shape[1:], lambda l, b: (l, 0, 0)),
                  pl.BlockSpec((None,) + wv_t.shape[1:], lambda l, b: (l, 0, 0)),
                  pl.BlockSpec(place.shape, lambda l, b: (0, 0))],
        out_specs=[pl.BlockSpec((None, None, C_HEADS, m, HEAD_PAD), lambda l, b: (l, b, 0, 0, 0)),
                   pl.BlockSpec((None, None, C_HEADS, V_ROWS,m), lambda l, b: (l, b, 0, 0, 0))],
        out_shape=[jax.ShapeDtypeStruct((depth, nb, C_HEADS, m, HEAD_PAD), BF16),
                   jax.ShapeDtypeStruct((depth, nb, C_HEADS, V_ROWS,m), BF16)],
        compiler_params=_cp(32), name="cache_kv_expand",
    )(cache_ckv, cache_krope, wkn, wv_t, place)


def _mix_kernel(u_ref, va_ref, pin_ref, ws_ref, bs_ref, band_ref, wp_ref, ps_ref,
                oa_ref, ob_ref, pad_ref):
    n = u_ref.shape[0]
    width = u_ref.shape[1]
    zero_rows = jnp.zeros((POOL_HALO, width), F32)
    pad_ref[0:POOL_HALO, :] = zero_rows
    pad_ref[n + POOL_HALO:n + 2 * POOL_HALO, :] = zero_rows
    pad_ref[POOL_HALO:n + POOL_HALO, :] = pin_ref[...]
    lane_grp = lax.broadcasted_iota(jnp.int32, (CHUNK, width), 1) // POOL_GROUP
    row = lax.broadcasted_iota(jnp.int32, (CHUNK, width), 0)

    def body(c, carry):
        r0 = pl.multiple_of(c * CHUNK, CHUNK)
        v = va_ref[pl.ds(r0, CHUNK), :]
        s = bs_ref[...]
        for hd in range(A_HEADS):
            s = s + jnp.where(lane_grp == hd, _dot(ws_ref[hd], v), 0.0)
        oa_ref[pl.ds(r0, CHUNK), :] = (u_ref[pl.ds(r0, CHUNK), :] * s).astype(BF16)
        win = pad_ref[pl.ds(r0, CHUNK + 2 * POOL_HALO), :]
        w_hi, w_lo = _split2(win)
        t = r0 + row
        tot = jnp.zeros((CHUNK, width), F32)
        cnt = jnp.ones((CHUNK, width), jnp.int32)
        for g, w in enumerate(POOL_WINDOWS):
            left = w // 2
            right = w - 1 - left
            sg = _dot(band_ref[g], w_hi) + _dot(band_ref[g], w_lo)
            tot = jnp.where(lane_grp == g, sg, tot)
            cg = jnp.clip(t + right + 1, 0, n) - jnp.clip(t - left, 0, n)
            cnt = jnp.where(lane_grp == g, cg, cnt)
        p = win[POOL_HALO:POOL_HALO + CHUNK, :]
        diff = (tot / cnt.astype(F32) - p).astype(BF16)
        ob_ref[pl.ds(r0, CHUNK), :] = (_dot(diff, wp_ref[...]) * ps_ref[...]).astype(BF16)
        return carry

    lax.fori_loop(0, n // CHUNK, body, 0)


def _pool_bands():
    rows = CHUNK + 2 * POOL_HALO
    band = np.zeros((len(POOL_WINDOWS), CHUNK, rows), np.float32)
    i = np.arange(CHUNK)[:, None]
    j = np.arange(rows)[None, :]
    for g, w in enumerate(POOL_WINDOWS):
        left = w // 2
        right = w - 1 - left
        band[g] = ((j >= i + POOL_HALO - left) & (j <= i + POOL_HALO + right)).astype(np.float32)
    return jnp.asarray(band, BF16)


def _mix_call(u, va, pin, lw, band):
    nb, n, w = u.shape
    seq = lambda: pl.BlockSpec((None, n, w), lambda b: (b, 0, 0))
    const = lambda a: pl.BlockSpec(a.shape, lambda b: (0,) * a.ndim)
    return pl.pallas_call(
        _mix_kernel,
        grid=(nb,),
        in_specs=[seq(), seq(), seq(), const(lw["w_s"]), const(lw["b_s"]), const(band),
                  const(lw["w_pool"]), const(lw["pool_scale"])],
        out_specs=[seq(), seq()],
        out_shape=[jax.ShapeDtypeStruct((nb, n, w), BF16)] * 2,
        scratch_shapes=[pltpu.VMEM((n + 2 * POOL_HALO, w), F32)],
        compiler_params=_cp(48), name="mixers_ab",
    )(u, va, pin, lw["w_s"], lw["b_s"], band, lw["w_pool"], lw["pool_scale"])


def _attn_kernel(*refs, seg_lens):
    nseg = len(seg_lens)
    qt_ref = refs[0]
    k_refs = refs[1:1 + nseg]
    v_refs = refs[1 + nseg:1 + 2 * nseg]
    o_ref, ot_ref, sa_ref, sb_ref = refs[1 + 2 * nseg:]
    tq = qt_ref.shape[-1]

    def scores_into(hd, s_ref):
        q = qt_ref[hd]
        mx = None
        r0 = 0
        for k_ref, m in zip(k_refs, seg_lens):
            for c0 in range(0, m, KEY_CHUNK):
                kc = min(KEY_CHUNK, m - c0)
                s = _dot(k_ref[hd, c0:c0 + kc, :], q)
                s_ref[r0:r0 + kc, :] = s
                cm = jnp.max(s, axis=0, keepdims=True)
                mx = cm if mx is None else jnp.maximum(mx, cm)
                r0 += kc
        return mx

    def finish(hd, s_ref, mx):
        acc = jnp.zeros((V_ROWS, tq), F32)
        r0 = 0
        for v_ref, m in zip(v_refs, seg_lens):
            p = jnp.exp2(s_ref[r0:r0 + m, :] - mx).astype(BF16)
            acc = acc + _dot(v_ref[hd], p)
            r0 += m
        ot_ref[pl.ds(pl.multiple_of(hd * V_DIM, V_DIM), V_DIM), :] = acc[0:V_DIM, :] / acc[V_DIM:V_DIM + 1, :]

    def head_pair(j, mx_a):
        h1 = 2 * j + 1
        mx_b = scores_into(h1, sb_ref)
        finish(h1 - 1, sa_ref, mx_a)
        mx_next = scores_into(h1 + 1, sa_ref)
        finish(h1, sb_ref, mx_b)
        return mx_next

    mx_a = lax.fori_loop(0, (C_HEADS - 2) // 2, head_pair, scores_into(0, sa_ref))
    mx_b = scores_into(C_HEADS - 1, sb_ref)
    finish(C_HEADS - 2, sa_ref, mx_a)
    finish(C_HEADS - 1, sb_ref, mx_b)
    o_ref[...] = ot_ref[...].T.astype(BF16)


def _attn_call(qt, ks, vts):
    nb, _, _, n = qt.shape
    tq = TOK_TILE
    seg_lens = tuple(k.shape[2] for k in ks)
    in_specs = [pl.BlockSpec((None, C_HEADS, HEAD_PAD, tq), lambda b, i: (b, 0, 0, i))]
    in_specs += [pl.BlockSpec((None, C_HEADS, m, HEAD_PAD), lambda b, i: (b, 0, 0, 0)) for m in seg_lens]
    in_specs += [pl.BlockSpec((None, C_HEADS, V_ROWS,m), lambda b, i: (b, 0, 0, 0)) for m in seg_lens]
    return pl.pallas_call(
        functools.partial(_attn_kernel, seg_lens=seg_lens),
        grid=(nb, n // tq), in_specs=in_specs,
        out_specs=pl.BlockSpec((None, tq, C_WIDTH), lambda b, i: (b, i, 0)),
        out_shape=jax.ShapeDtypeStruct((nb, n, C_WIDTH), BF16),
        scratch_shapes=[pltpu.VMEM((C_WIDTH, tq), F32)] + [pltpu.VMEM((sum(seg_lens), tq), F32)] * 2,
        compiler_params=_cp(48), name="latent_attention",
    )(qt, *ks, *vts)


def _outproj_kernel(x_ref, oa_ref, ob_ref, oc_ref, wo_ref, g1_ref, sh_ref, sc_ref, ng_ref,
                    wrh_ref, wrl_ref, x1_ref, h_ref, aff_ref, asp_ref):
    mix = (_dot(oa_ref[...], wo_ref[0:A_WIDTH, :])
           + _dot(ob_ref[...], wo_ref[A_WIDTH:A_WIDTH + B_WIDTH, :])
           + _dot(oc_ref[...], wo_ref[A_WIDTH + B_WIDTH:, :]))
    x1 = x_ref[...] + g1_ref[...] * mix
    x1_ref[...] = x1
    h = _rms(x1) * ng_ref[...] * (1.0 + sc_ref[...]) + sh_ref[...]
    h_hi, h_lo = _split2(h)
    h_ref[...] = h_hi
    logits = _dot(h_hi, wrh_ref[...]) + _dot(h_lo, wrh_ref[...]) + _dot(h_hi, wrl_ref[...])
    lane = lax.broadcasted_iota(jnp.int32, logits.shape, 1)
    logits = jnp.where(lane < N_EXPERTS, logits, -jnp.inf)
    ex = jnp.exp(logits - jnp.max(logits, axis=-1, keepdims=True))
    aff = ex / jnp.sum(ex, axis=-1, keepdims=True)
    aff_ref[...] = aff
    a_hi = aff.astype(BF16).astype(F32)
    r1 = aff - a_hi
    a_lo = r1.astype(BF16).astype(F32)
    a_lo2 = r1 - a_lo
    packed = a_hi + pltpu.roll(a_lo, N_EXPERTS, axis=1) + pltpu.roll(a_lo2, 2 * N_EXPERTS, axis=1)
    asp_ref[...] = packed.astype(BF16)


def _outproj_call(x, oa, ob, oc, modl, row0, row_step, lw):
    nb, n, d = x.shape
    tm = TOK_TILE
    tok = lambda c: pl.BlockSpec((None, tm, c), lambda b, i: (b, i, 0))
    const = lambda a: pl.BlockSpec(a.shape, lambda b, i: (0,) * a.ndim)
    modspec = lambda j: pl.BlockSpec((None, None, 1, d), lambda b, i: (row0 + b * row_step, j, 0, 0))
    return pl.pallas_call(
        _outproj_kernel,
        grid=(nb, n // tm),
        in_specs=[tok(d), tok(A_WIDTH), tok(B_WIDTH), tok(C_WIDTH), const(lw["w_out"]),
                  modspec(2), modspec(3), modspec(4), const(lw["norm2_g"]),
                  const(lw["wr_hi"]), const(lw["wr_lo"])],
        out_specs=[tok(d), tok(d), tok(LANES), tok(LANES)],
        out_shape=[jax.ShapeDtypeStruct((nb, n, d), F32), jax.ShapeDtypeStruct((nb, n, d), BF16),
                   jax.ShapeDtypeStruct((nb, n, LANES), F32), jax.ShapeDtypeStruct((nb, n, LANES), BF16)],
        compiler_params=_cp(48), name="out_proj_router",
    )(x, oa, ob, oc, lw["w_out"], modl, modl, modl, lw["norm2_g"], lw["wr_hi"], lw["wr_lo"])


def _route_kernel(aff_ref, pos_ref, post_ref, offs_ref, *, cap):
    nb, n, _ = aff_ref.shape
    c = TOK_TILE
    nck = n // c
    aff = aff_ref[...]
    capf = jnp.float32(cap)

    def search(i, bits):
        cand = bits | jnp.left_shift(jnp.int32(1), jnp.int32(SEARCH_BITS - 1) - i)
        cnt = jnp.sum(jnp.where(aff >= lax.bitcast_convert_type(cand, F32), 1.0, 0.0),
                      axis=1, keepdims=True)
        return jnp.where(cnt >= capf, cand, bits)

    bits = lax.fori_loop(0, SEARCH_BITS, search, jnp.zeros((nb, 1, LANES), jnp.int32))
    thr_all = lax.bitcast_convert_type(bits, F32)
    n_gt = jnp.sum(jnp.where(aff > thr_all, 1.0, 0.0), axis=1, keepdims=True)
    n_tie_all = capf - n_gt
    ri = lax.broadcasted_iota(jnp.int32, (c, c), 0)
    ci = lax.broadcasted_iota(jnp.int32, (c, c), 1)
    tri = jnp.where(ri > ci, 1.0, 0.0).astype(BF16)
    lane_ok = lax.broadcasted_iota(jnp.int32, (c, LANES), 1) < N_EXPERTS
    offs_ref[...] = jnp.zeros(offs_ref.shape, jnp.int32)
    for s in range(nb):
        thr = thr_all[s]
        n_tie = n_tie_all[s]
        tie_seen = jnp.zeros((1, LANES), F32)
        sel_seen = jnp.zeros((1, LANES), F32)
        for k in range(nck):
            kk = aff_ref[s, k * c:(k + 1) * c, :]
            gt = kk > thr
            eq = kk == thr
            eqf = jnp.where(eq, 1.0, 0.0)
            tie_rank = _dot(tri, eqf.astype(BF16)) + tie_seen
            sel = (gt | (eq & (tie_rank < n_tie))) & lane_ok
            self_ = jnp.where(sel, 1.0, 0.0)
            slot = _dot(tri, self_.astype(BF16)) + sel_seen
            pos = jnp.where(sel, slot, -1.0)
            pos_ref[s, k * c:(k + 1) * c, :] = pos
            post_ref[s, :, k * c:(k + 1) * c] = pos.T[0:N_EXPERTS, :]
            offs_ref[s, k:k + 1, :] = sel_seen[:, 0:N_EXPERTS].astype(jnp.int32)
            tie_seen = tie_seen + jnp.sum(eqf, axis=0, keepdims=True)
            sel_seen = sel_seen + jnp.sum(self_, axis=0, keepdims=True)
        offs_ref[s, nck:nck + 1, :] = sel_seen[:, 0:N_EXPERTS].astype(jnp.int32)


def _offs_rows(n):
    return -(-(n // TOK_TILE + 1) // 8) * 8


def _route_call(aff, cap):
    nb, n, _ = aff.shape
    rows = _offs_rows(n)
    return pl.pallas_call(
        functools.partial(_route_kernel, cap=cap),
        grid=(1,),
        in_specs=[pl.BlockSpec((nb, n, LANES), lambda i: (0, 0, 0))],
        out_specs=[pl.BlockSpec((nb, n, LANES), lambda i: (0, 0, 0)),
                   pl.BlockSpec((nb, N_EXPERTS, n), lambda i: (0, 0, 0)),
                   pl.BlockSpec((nb, rows, N_EXPERTS), lambda i: (0, 0, 0))],
        out_shape=[jax.ShapeDtypeStruct((nb, n, LANES), F32),
                   jax.ShapeDtypeStruct((nb, N_EXPERTS, n), F32),
                   jax.ShapeDtypeStruct((nb, rows, N_EXPERTS), jnp.int32)],
        compiler_params=_cp(32), name="ec_route",
    )(aff, )


def _window(offs_ref, b, k, e, rows, win):
    off = offs_ref[(b * rows + k) * N_EXPERTS + e]
    nxt = offs_ref[(b * rows + k + 1) * N_EXPERTS + e]
    base = lax.shift_right_logical(off, SLOT_ALIGN.bit_length() - 1) * SLOT_ALIGN
    nwin = lax.shift_right_logical(nxt - base + (win - 1), win.bit_length() - 1)
    return base, nwin


def _gather_kernel(offs_ref, post_ref, h_ref, asp_ref, xs_ref, gs_ref, *, cap, win, rows):
    b = pl.program_id(0)
    k = pl.program_id(1)
    c = h_ref.shape[0]

    @pl.when(k == 0)
    def _():
        xs_ref[...] = jnp.zeros(xs_ref.shape, xs_ref.dtype)
        gs_ref[...] = jnp.zeros(gs_ref.shape, gs_ref.dtype)

    sub = lax.broadcasted_iota(jnp.int32, (win, c), 0).astype(F32)

    def sel_rows(e, start):
        cb = pl.multiple_of(jnp.minimum(start, cap - win), SLOT_ALIGN)
        prow = post_ref[e:e + 1, :]
        startf = start.astype(F32)
        ok = (prow >= startf) & (prow < startf + win)
        p = jnp.where(ok & (prow == sub + cb.astype(F32)), 1.0, 0.0).astype(BF16)
        return p, cb

    bases = []
    parts = []
    for e in range(N_EXPERTS):
        base, nwin = _window(offs_ref, b, k, e, rows, win)
        p, cb = sel_rows(e, base)
        parts.append(p)
        bases.append((cb, base, nwin))
    p_all = jnp.concatenate(parts, axis=0)
    hk = h_ref[...]
    gk = asp_ref[...]
    r_all = _dot(p_all, hk).astype(BF16)
    g_all = _dot(p_all, gk)
    for e in range(N_EXPERTS):
        cb, base, nwin = bases[e]
        xs_ref[e, pl.ds(cb, win), :] += r_all[e * win:(e + 1) * win, :]
        gs_ref[e, pl.ds(cb, win), :] += g_all[e * win:(e + 1) * win, :]

        def extra(w, carry, e=e, base=base):
            p, cbw = sel_rows(e, base + w * win)
            xs_ref[e, pl.ds(cbw, win), :] += _dot(p, hk).astype(BF16)
            gs_ref[e, pl.ds(cbw, win), :] += _dot(p, gk)
            return carry

        lax.fori_loop(1, nwin, extra, 0)


def _gather_call(offs, post, h, asp, cap, win):
    nb, n, d = h.shape
    c = TOK_TILE
    rows = _offs_rows(n)
    grid_spec = pltpu.PrefetchScalarGridSpec(
        num_scalar_prefetch=1, grid=(nb, n // c),
        in_specs=[pl.BlockSpec((None, N_EXPERTS, c), lambda b, k, o: (b, 0, k)),
                  pl.BlockSpec((None, c, d), lambda b, k, o: (b, k, 0)),
                  pl.BlockSpec((None, c, LANES), lambda b, k, o: (b, k, 0))],
        out_specs=[pl.BlockSpec((N_EXPERTS, cap, d), lambda b, k, o: (0, b, 0)),
                   pl.BlockSpec((N_EXPERTS, cap, LANES), lambda b, k, o: (0, b, 0))])
    return pl.pallas_call(
        functools.partial(_gather_kernel, cap=cap, win=win, rows=rows),
        grid_spec=grid_spec,
        out_shape=[jax.ShapeDtypeStruct((N_EXPERTS, nb * cap, d), BF16),
                   jax.ShapeDtypeStruct((N_EXPERTS, nb * cap, LANES), F32)],
        compiler_params=_cp(48), name="ec_gather",
    )(offs.reshape(-1), post, h, asp)


def _ffn_kernel(xa_ref, xb_ref, ga_ref, gb_ref, w1_ref, w3_ref, w2_ref, ya_ref, yb_ref,
                acca_ref, accb_ref):
    e = pl.program_id(0)
    f = pl.program_id(1)
    w1 = w1_ref[...].astype(BF16)
    w3 = w3_ref[...].astype(BF16)
    w2 = w2_ref[...].astype(BF16)
    for x_ref, g_ref, y_ref, acc_ref in ((xa_ref, ga_ref, ya_ref, acca_ref),
                                         (xb_ref, gb_ref, yb_ref, accb_ref)):
        x = x_ref[...]
        hid = jax.nn.silu(_dot(x, w1)) * _dot(x, w3)
        part = _dot(hid.astype(BF16), w2)

        @pl.when(f == 0)
        def _(acc_ref=acc_ref, part=part):
            acc_ref[...] = part

        @pl.when(f > 0)
        def _(acc_ref=acc_ref, part=part):
            acc_ref[...] += part

        @pl.when(f == pl.num_programs(1) - 1)
        def _(acc_ref=acc_ref, g_ref=g_ref, y_ref=y_ref):
            g = g_ref[...]
            lane = lax.broadcasted_iota(jnp.int32, g.shape, 1)
            keep = ((lane % N_EXPERTS) == e) & (lane < 3 * N_EXPERTS)
            gate = jnp.sum(jnp.where(keep, g, 0.0), axis=-1, keepdims=True)
            y_ref[...] = (acc_ref[...] * gate).astype(BF16)


def _ffn_call(xa, xb, ga, gb, w1, w3, w2, layer):
    ne, ra, d = xa.shape
    rb = xb.shape[1]
    ff = w1.shape[3]
    tf = FF_TILE
    rowsp = lambda r, c: pl.BlockSpec((None, r, c), lambda e, f: (e, 0, 0))
    return pl.pallas_call(
        _ffn_kernel,
        grid=(ne, ff // tf),
        in_specs=[rowsp(ra, d), rowsp(rb, d), rowsp(ra, LANES), rowsp(rb, LANES),
                  pl.BlockSpec((None, None, d, tf), lambda e, f: (layer, e, 0, f)),
                  pl.BlockSpec((None, None, d, tf), lambda e, f: (layer, e, 0, f)),
                  pl.BlockSpec((None, None, tf, d), lambda e, f: (layer, e, f, 0))],
        out_specs=[rowsp(ra, d), rowsp(rb, d)],
        out_shape=[jax.ShapeDtypeStruct((ne, ra, d), BF16), jax.ShapeDtypeStruct((ne, rb, d), BF16)],
        scratch_shapes=[pltpu.VMEM((ra, d), F32), pltpu.VMEM((rb, d), F32)],
        compiler_params=_cp(56), name="expert_swiglu",
    )(xa, xb, ga, gb, w1, w3, w2)


def _combine_kernel(offs_ref, pos_ref, y_ref, x_ref, g2_ref, *rest, cap, win, rows, last):
    if last:
        fg_ref, o_ref, on_ref, acc_ref = rest
    else:
        o_ref, acc_ref = rest
    b = pl.program_id(0)
    k = pl.program_id(1)
    c = x_ref.shape[0]
    per = 256 // win
    depth = per * win
    lane = lax.broadcasted_iota(jnp.int32, (c, depth), 1)
    lane_grp = lane // win
    lane_in = (lane % win).astype(F32)
    lane1 = lax.broadcasted_iota(jnp.int32, (c, win), 1).astype(F32)
    acc_ref[...] = jnp.zeros(acc_ref.shape, F32)
    for g0 in range(0, N_EXPERTS, per):
        pmat = jnp.zeros((c, depth), F32)
        tgt = jnp.zeros((c, depth), F32)
        lo = jnp.zeros((c, depth), F32)
        ys = []
        extras = []
        for j in range(per):
            e = g0 + j
            base, nwin = _window(offs_ref, b, k, e, rows, win)
            cb = pl.multiple_of(jnp.minimum(base, cap - win), SLOT_ALIGN)
            here = lane_grp == j
            pmat = jnp.where(here, pos_ref[:, e:e + 1], pmat)
            tgt = jnp.where(here, lane_in + cb.astype(F32), tgt)
            lo = jnp.where(here, base.astype(F32), lo)
            ys.append(y_ref[e, pl.ds(cb, win), :])
            extras.append((e, base, nwin))
        pt = jnp.where((pmat == tgt) & (pmat >= lo) & (pmat < lo + win), 1.0, 0.0).astype(BF16)
        acc_ref[...] += _dot(pt, jnp.concatenate(ys, axis=0))
        for e, base, nwin in extras:
            def extra(w, carry, e=e, base=base):
                start = base + w * win
                cbw = pl.multiple_of(jnp.minimum(start, cap - win), SLOT_ALIGN)
                pcol = pos_ref[:, e:e + 1]
                startf = start.astype(F32)
                ok = (pcol >= startf) & (pcol < startf + win) & (pcol == lane1 + cbw.astype(F32))
                acc_ref[...] += _dot(jnp.where(ok, 1.0, 0.0).astype(BF16), y_ref[e, pl.ds(cbw, win), :])
                return carry

            lax.fori_loop(1, nwin, extra, 0)
    x2 = x_ref[...] + g2_ref[...] * acc_ref[...]
    o_ref[...] = x2
    if last:
        on_ref[...] = _rms(x2) * fg_ref[...]


def _combine_call(offs, pos, y, x1, modl, row0, row_step, final_g, cap, win):
    nb, n, d = x1.shape
    c = TOK_TILE
    rows = _offs_rows(n)
    last = final_g is not None
    tok = pl.BlockSpec((None, c, d), lambda b, k, o: (b, k, 0))
    in_specs = [pl.BlockSpec((None, c, LANES), lambda b, k, o: (b, k, 0)),
                pl.BlockSpec((N_EXPERTS, cap, d), lambda b, k, o: (0, b, 0)),
                tok,
                pl.BlockSpec((None, None, 1, d), lambda b, k, o: (row0 + b * row_step, 5, 0, 0))]
    ins = [offs.reshape(-1), pos, y, x1, modl]
    if last:
        in_specs.append(pl.BlockSpec((1, d), lambda b, k, o: (0, 0)))
        ins.append(final_g)
    nout = 2 if last else 1
    grid_spec = pltpu.PrefetchScalarGridSpec(
        num_scalar_prefetch=1, grid=(nb, n // c), in_specs=in_specs, out_specs=[tok] * nout,
        scratch_shapes=[pltpu.VMEM((c, d), F32)])
    return pl.pallas_call(
        functools.partial(_combine_kernel, cap=cap, win=win, rows=rows, last=last),
        grid_spec=grid_spec,
        out_shape=[jax.ShapeDtypeStruct((nb, n, d), F32)] * nout,
        compiler_params=_cp(56), name="ec_combine_final" if last else "ec_combine",
    )(*ins)


def _prep_weights(w_in, w_s, b_s, w_pool, pool_scale, q_norm_g, w_qb, kv_norm_g, w_kvb, w_out,
                  w_router, norm1_g, norm2_g):
    depth, d, cols = w_in.shape
    row = lambda a: a.reshape(depth, 1, -1)
    w_in_p = jnp.pad(w_in, ((0, 0), (0, 0), (0, PROJ_PAD - cols))).astype(BF16)
    wq = w_qb.reshape(depth, Q_LORA, C_HEADS, QK_NOPE + QK_ROPE)
    wq = jnp.pad(wq, ((0, 0), (0, 0), (0, 0), (0, HEAD_PAD - QK_NOPE - QK_ROPE)))
    wq_t = wq.reshape(depth, Q_LORA, C_HEADS * HEAD_PAD).transpose(0, 2, 1).astype(BF16)
    wkv = w_kvb.reshape(depth, KV_LORA, C_HEADS, QK_NOPE + V_DIM)
    wkn = jnp.pad(wkv[..., :QK_NOPE], ((0, 0), (0, 0), (0, 0), (0, HEAD_PAD - QK_NOPE)))
    wkn = wkn.reshape(depth, KV_LORA, C_HEADS * HEAD_PAD).astype(BF16)
    wv_t = wkv[..., QK_NOPE:].reshape(depth, KV_LORA, C_WIDTH).transpose(0, 2, 1).astype(BF16)
    b_rep = jnp.repeat(b_s.transpose(0, 2, 1), A_HEAD_DIM, axis=2)
    eye = jnp.eye(len(POOL_WINDOWS), dtype=F32)
    wp_bd = jnp.einsum("gh,lgcd->lgchd", eye, w_pool).reshape(depth, B_WIDTH, B_WIDTH).astype(BF16)
    wr = jnp.pad(w_router, ((0, 0), (0, 0), (0, LANES - N_EXPERTS)))
    wr_hi = wr.astype(BF16)
    wr_lo = (wr - wr_hi.astype(F32)).astype(BF16)
    per_layer = []
    for l in range(depth):
        per_layer.append(dict(
            norm1_g=row(norm1_g)[l], norm2_g=row(norm2_g)[l], w_in=w_in_p[l],
            q_norm_g=row(q_norm_g)[l], kv_norm_g=row(kv_norm_g)[l], wq_t=wq_t[l], wkn=wkn[l],
            wv_t=wv_t[l], w_s=w_s[l].astype(BF16), b_s=b_rep[l], w_pool=wp_bd[l],
            pool_scale=row(pool_scale)[l], w_out=w_out[l].astype(BF16), wr_hi=wr_hi[l], wr_lo=wr_lo[l]))
    return per_layer, wkn, wv_t


def _rope_tables(n):
    rows = n // GRID_W
    row = jnp.repeat(jnp.arange(rows), GRID_W).astype(F32)
    col = jnp.broadcast_to(jnp.arange(GRID_W), (rows, GRID_W)).reshape(-1).astype(F32)
    inv = 1.0 / (ROPE_THETA ** (jnp.arange(0, ROPE_AXIS, 2, dtype=F32) / ROPE_AXIS))
    ang_r = row[:, None] * inv[None, :]
    ang_c = col[:, None] * inv[None, :]
    cos_r, sin_r, cos_c, sin_c = jnp.cos(ang_r), jnp.sin(ang_r), jnp.cos(ang_c), jnp.sin(ang_c)
    pad = jnp.zeros((n, LANES - QK_ROPE), F32)
    cos_k = jnp.concatenate([cos_r, cos_r, cos_c, cos_c, pad], axis=1)
    sin_k = jnp.concatenate([-sin_r, sin_r, -sin_c, sin_c, pad], axis=1)
    return cos_k, sin_k, cos_r.T, sin_r.T, cos_c.T, sin_c.T


def kernel(x_prompt, x_sample, cache_ckv, cache_krope, c, c_ctx, w_ada, b_ada, norm1_g, norm2_g, w_in, w_s, b_s, w_pool, pool_scale, q_norm_g, w_qb, kv_norm_g, w_kvb, w_out, w_router, w_e1, w_e3, w_e2, final_norm_g):
    depth, d, _ = w_ada.shape
    nb_c, n_c, _ = x_prompt.shape
    nb_s, n_s, _ = x_sample.shape
    assert 1 + nb_s <= 8
    cond8 = jnp.concatenate([c_ctx[None, :], c, jnp.zeros((8 - 1 - nb_s, d), F32)], axis=0)
    mod = _mod_call(cond8, w_ada, b_ada).reshape(depth, 8, 6, 1, d)
    layers, wkn_all, wvt_all = _prep_weights(w_in, w_s, b_s, w_pool, pool_scale, q_norm_g, w_qb,
                                             kv_norm_g, w_kvb, w_out, w_router, norm1_g, norm2_g)
    k_cache, vt_cache = _kvexp_call(cache_ckv, cache_krope, wkn_all, wvt_all)
    tabs = _rope_tables(n_s)
    band = _pool_bands()
    final_g = final_norm_g.reshape(1, d)
    cap_c = EC_CAPACITY * n_c // N_EXPERTS
    cap_s = EC_CAPACITY * n_s // N_EXPERTS
    win_c = min(cap_c, 64)
    win_s = min(cap_s, 64)

    xc, xs = x_prompt, x_sample
    yc = ys = None
    ckvs, kropes = [], []
    for l in range(depth):
        lw = layers[l]
        modl = mod[l]
        uc, vac, pinc, qtc, khc, vtc, ckv_c, kr_c = _proj_call(xc, modl, 0, 0, lw, None)
        us, vas, pins, qts, khs, vts, _, _ = _proj_call(xs, modl, 1, 1, lw, tabs)
        ckvs.append(ckv_c)
        kropes.append(kr_c)
        oac, obc = _mix_call(uc, vac, pinc, lw, band)
        oas, obs = _mix_call(us, vas, pins, lw, band)
        occ = _attn_call(qtc, [khc], [vtc])
        ocs = _attn_call(qts, [k_cache[l], khs], [vt_cache[l], vts])
        x1c, hc, affc, aspc = _outproj_call(xc, oac, obc, occ, modl, 0, 0, lw)
        x1s, hs, affs, asps = _outproj_call(xs, oas, obs, ocs, modl, 1, 1, lw)
        posc, postc, offc = _route_call(affc, cap_c)
        poss, posts, offs = _route_call(affs, cap_s)
        xgc, ggc = _gather_call(offc, postc, hc, aspc, cap_c, win_c)
        xgs, ggs = _gather_call(offs, posts, hs, asps, cap_s, win_s)
        ygs, ygc = _ffn_call(xgs, xgc, ggs, ggc, w_e1, w_e3, w_e2, l)
        fg = final_g if l == depth - 1 else None
        outc = _combine_call(offc, posc, ygc, x1c, modl, 0, 0, fg, cap_c, win_c)
        outs = _combine_call(offs, poss, ygs, x1s, modl, 1, 1, fg, cap_s, win_s)
        xc, xs = outc[0], outs[0]
    yc, ys = outc[1], outs[1]
    new_ckv = jnp.stack(ckvs, axis=1)
    new_krope = jnp.stack(kropes, axis=1)
    return (yc, ys, new_ckv, new_krope)
```

```python
import functools
import math

import numpy as np
import jax
import jax.numpy as jnp
from jax import lax
from jax.experimental import pallas as pl
from jax.experimental.pallas import tpu as pltpu

F32 = jnp.float32
BF16 = jnp.bfloat16

GRID_W = 64
A_HEADS = 4
A_HEAD_DIM = 64
A_WIDTH = A_HEADS * A_HEAD_DIM
CHUNK = 128
POOL_WINDOWS = (2, 4, 8, 16)
POOL_GROUP = 64
B_WIDTH = len(POOL_WINDOWS) * POOL_GROUP
C_HEADS = 8
QK_NOPE = 64
QK_ROPE = 32
V_DIM = 64
Q_LORA = 384
KV_LORA = 256
C_WIDTH = C_HEADS * V_DIM
ROPE_AXIS = QK_ROPE // 2
ROPE_THETA = 10000.0
ATTN_SCALE = (QK_NOPE + QK_ROPE) ** -0.5
N_EXPERTS = 16
EC_CAPACITY = 2
NORM_EPS = 1e-6

LANES = 128
HEAD_PAD = 128
PROJ_PAD = 1536
OFF_P = 2 * A_WIDTH
OFF_CQ = OFF_P + B_WIDTH
OFF_CKV = OFF_CQ + Q_LORA
OFF_KR = OFF_CKV + KV_LORA
POOL_HALO = 8
TOK_TILE = 256
ROW_TILE = 512
SUB_ROWS = 256
V_ROWS = V_DIM + 16
KEY_CHUNK = 512
Q_PRESCALE = ATTN_SCALE * math.log2(math.e)
FF_TILE = 512
MOD_TILE = 1536
SEARCH_BITS = 31
SLOT_ALIGN = 16
GATHER_GROUP = 4
MXU_DEPTH = 256


def _cp(vmem_mb):
    return pltpu.CompilerParams(vmem_limit_bytes=vmem_mb * 1024 * 1024)


def _dot(a, b):
    return jnp.dot(a, b, preferred_element_type=F32)


def _dot_nt(a, b):
    return lax.dot_general(a, b, (((1,), (1,)), ((), ())), preferred_element_type=F32)


def _split2(x):
    hi = x.astype(BF16)
    lo = (x - hi.astype(F32)).astype(BF16)
    return hi, lo


def _rms(x):
    return x * lax.rsqrt(jnp.mean(x * x, axis=-1, keepdims=True) + NORM_EPS)


def _load_rows(ref):
    return jnp.concatenate([ref[s] for s in range(ref.shape[0])], axis=0)


def _store_rows(ref, val):
    tm = ref.shape[1]
    for s in range(ref.shape[0]):
        ref[s] = val[s * tm:(s + 1) * tm, :]


def _with_ones_row(vt):
    rows = lax.broadcasted_iota(jnp.int32, (V_ROWS - V_DIM, vt.shape[1]), 0)
    return jnp.concatenate([vt, jnp.where(rows == 0, 1.0, 0.0).astype(BF16)], axis=0)


def _mod_kernel(c_ref, w_ref, b_ref, o_ref):
    a = jax.nn.silu(c_ref[...])
    a_hi, a_lo = _split2(a)
    w_hi, w_lo = _split2(w_ref[...])
    o_ref[...] = _dot(a_hi, w_hi) + _dot(a_lo, w_hi) + _dot(a_hi, w_lo) + b_ref[...]


def _mod_call(cond8, w_ada, b_ada):
    depth, d, n = w_ada.shape
    return pl.pallas_call(
        _mod_kernel,
        grid=(depth, n // MOD_TILE),
        in_specs=[pl.BlockSpec((8, d), lambda l, j: (0, 0)),
                  pl.BlockSpec((None, d, MOD_TILE), lambda l, j: (l, 0, j)),
                  pl.BlockSpec((None, 1, MOD_TILE), lambda l, j: (l, 0, j))],
        out_specs=pl.BlockSpec((None, 8, MOD_TILE), lambda l, j: (l, 0, j)),
        out_shape=jax.ShapeDtypeStruct((depth, 8, n), F32),
        compiler_params=_cp(40),
        name="adaln_mod",
    )(cond8, w_ada, b_ada.reshape(depth, 1, n))


def _proj_kernel(*refs, rope):
    if rope:
        (x_ref, sh_ref, sc_ref, g_ref, win_ref, qg_ref, kvg_ref, wq_ref, wkn_ref, wvt_ref,
         ck_ref, sk_ref, cr_ref, sr_ref, cc_ref, scl_ref,
         u_ref, va_ref, pin_ref, qt_ref, kh_ref, vt_ref, ckv_ref, kr_ref) = refs
    else:
        (x_ref, sh_ref, sc_ref, g_ref, win_ref, qg_ref, kvg_ref, wq_ref, wkn_ref, wvt_ref,
         u_ref, va_ref, pin_ref, qt_ref, kh_ref, vt_ref, ckv_ref, kr_ref) = refs
    nseq, tm = x_ref.shape[0], x_ref.shape[1]
    x = _load_rows(x_ref)
    h = _rms(x) * g_ref[...] * (1.0 + sc_ref[...]) + sh_ref[...]
    proj = _dot(h.astype(BF16), win_ref[...])
    _store_rows(u_ref, jax.nn.gelu(proj[:, 0:A_WIDTH]))
    _store_rows(va_ref, jax.nn.gelu(proj[:, A_WIDTH:OFF_P]).astype(BF16))
    _store_rows(pin_ref, proj[:, OFF_P:OFF_CQ])
    cq = _rms(proj[:, OFF_CQ:OFF_CKV]) * qg_ref[...]
    ckv = _rms(proj[:, OFF_CKV:OFF_KR]) * kvg_ref[...]
    _store_rows(ckv_ref, ckv)
    kr = proj[:, OFF_KR:OFF_KR + LANES]
    _store_rows(kr_ref, kr[:, 0:QK_ROPE])

    qt = _dot_nt(wq_ref[...], cq.astype(BF16))
    ckv_b = ckv.astype(BF16)
    kn = _dot(ckv_b, wkn_ref[...])
    vt = _dot_nt(wvt_ref[...], ckv_b)

    if rope:
        lane = lax.broadcasted_iota(jnp.int32, kr.shape, 1)
        half = ROPE_AXIS // 2
        swapped = jnp.where((lane & half) == 0,
                            pltpu.roll(kr, LANES - half, axis=1), pltpu.roll(kr, half, axis=1))
        kr = kr * ck_ref[...] + swapped * sk_ref[...]
        cr, sr, cc, scl = cr_ref[...], sr_ref[...], cc_ref[...], scl_ref[...]
    kr_shift = pltpu.roll(kr, QK_NOPE, axis=1)

    for hd in range(C_HEADS):
        r0 = hd * HEAD_PAD
        q_h = qt[r0:r0 + HEAD_PAD, :]
        if rope:
            b0 = QK_NOPE
            a, b = q_h[b0:b0 + 8, :], q_h[b0 + 8:b0 + 16, :]
            c, d = q_h[b0 + 16:b0 + 24, :], q_h[b0 + 24:b0 + 32, :]
            q_h = jnp.concatenate(
                [q_h[0:b0, :], a * cr - b * sr, a * sr + b * cr, c * cc - d * scl, c * scl + d * cc,
                 q_h[b0 + 32:, :]], axis=0)
        q_h = (q_h * Q_PRESCALE).astype(BF16)
        k_h = (kn[:, r0:r0 + HEAD_PAD] + kr_shift).astype(BF16)
        v_h = _with_ones_row(vt[hd * V_DIM:(hd + 1) * V_DIM, :].astype(BF16))
        for s in range(nseq):
            qt_ref[s, hd] = q_h[:, s * tm:(s + 1) * tm]
            kh_ref[s, hd] = k_h[s * tm:(s + 1) * tm, :]
            vt_ref[s, hd] = v_h[:, s * tm:(s + 1) * tm]


def _seq_tiling(n, row_step):
    tm = min(n, ROW_TILE)
    return (ROW_TILE // tm if row_step == 0 else 1), tm


def _proj_call(x, modl, row0, row_step, lw, tabs):
    nb, n, d = x.shape
    sb, tm = _seq_tiling(n, row_step)
    rope = tabs is not None
    assert not rope or sb == 1
    tok = lambda c: pl.BlockSpec((sb, tm, c), lambda b, i: (b, i, 0))
    const = lambda a: pl.BlockSpec(a.shape, lambda b, i: (0,) * a.ndim)
    modspec = lambda j: pl.BlockSpec((None, None, 1, d), lambda b, i: (row0 + b * sb * row_step, j, 0, 0))
    ins = [x, modl, modl, lw["norm1_g"], lw["w_in"], lw["q_norm_g"], lw["kv_norm_g"],
           lw["wq_t"], lw["wkn"], lw["wv_t"]]
    in_specs = [tok(d), modspec(0), modspec(1), const(lw["norm1_g"]), const(lw["w_in"]),
                const(lw["q_norm_g"]), const(lw["kv_norm_g"]), const(lw["wq_t"]), const(lw["wkn"]),
                const(lw["wv_t"])]
    if rope:
        cos_k, sin_k, cos_r, sin_r, cos_c, sin_c = tabs
        ins += [cos_k, sin_k, cos_r, sin_r, cos_c, sin_c]
        in_specs += [pl.BlockSpec((tm, LANES), lambda b, i: (i, 0))] * 2
        in_specs += [pl.BlockSpec((8, tm), lambda b, i: (0, i))] * 4
    out_shape = [jax.ShapeDtypeStruct((nb, n, A_WIDTH), F32),
                 jax.ShapeDtypeStruct((nb, n, A_WIDTH), BF16),
                 jax.ShapeDtypeStruct((nb, n, B_WIDTH), F32),
                 jax.ShapeDtypeStruct((nb, C_HEADS, HEAD_PAD, n), BF16),
                 jax.ShapeDtypeStruct((nb, C_HEADS, n, HEAD_PAD), BF16),
                 jax.ShapeDtypeStruct((nb, C_HEADS, V_ROWS,n), BF16),
                 jax.ShapeDtypeStruct((nb, n, KV_LORA), F32),
                 jax.ShapeDtypeStruct((nb, n, QK_ROPE), F32)]
    out_specs = [tok(A_WIDTH), tok(A_WIDTH), tok(B_WIDTH),
                 pl.BlockSpec((sb, C_HEADS, HEAD_PAD, tm), lambda b, i: (b, 0, 0, i)),
                 pl.BlockSpec((sb, C_HEADS, tm, HEAD_PAD), lambda b, i: (b, 0, i, 0)),
                 pl.BlockSpec((sb, C_HEADS, V_ROWS, tm), lambda b, i: (b, 0, 0, i)),
                 tok(KV_LORA), tok(QK_ROPE)]
    return pl.pallas_call(
        functools.partial(_proj_kernel, rope=rope),
        grid=(nb // sb, n // tm), in_specs=in_specs, out_specs=out_specs, out_shape=out_shape,
        compiler_params=_cp(48), name="in_proj_rope" if rope else "in_proj",
    )(*ins)


def _kvexp_kernel(ckv_ref, kr_ref, wkn_ref, wvt_ref, place_ref, kh_ref, vt_ref):
    ckv_b = ckv_ref[...].astype(BF16)
    kn = _dot(ckv_b, wkn_ref[...])
    vt = _dot_nt(wvt_ref[...], ckv_b)
    kr_placed = _dot(kr_ref[...].astype(BF16), place_ref[...])
    for hd in range(C_HEADS):
        r0 = hd * HEAD_PAD
        kh_ref[hd] = (kn[:, r0:r0 + HEAD_PAD] + kr_placed).astype(BF16)
        vt_ref[hd] = _with_ones_row(vt[hd * V_DIM:(hd + 1) * V_DIM, :].astype(BF16))


def _kvexp_call(cache_ckv, cache_krope, wkn, wv_t):
    nb, depth, m, r = cache_ckv.shape
    place = np.zeros((QK_ROPE, HEAD_PAD), np.float32)
    place[np.arange(QK_ROPE), QK_NOPE + np.arange(QK_ROPE)] = 1.0
    place = jnp.asarray(place, BF16)
    return pl.pallas_call(
        _kvexp_kernel,
        grid=(depth, nb),
        in_specs=[pl.BlockSpec((None, None, m, r), lambda l, b: (b, l, 0, 0)),
                  pl.BlockSpec((None, None, m, QK_ROPE), lambda l, b: (b, l, 0, 0)),
                  pl.BlockSpec((None,) + wkn.shape[1:], lambda l, b: (l, 0, 0)),
                  pl.BlockSpec((None,) + wv_t.shape[1:], lambda l, b: (l, 0, 0)),
                  pl.BlockSpec(place.shape, lambda l, b: (0, 0))],
        out_specs=[pl.BlockSpec((None, None, C_HEADS, m, HEAD_PAD), lambda l, b: (l, b, 0, 0, 0)),
                   pl.BlockSpec((None, None, C_HEADS, V_ROWS,m), lambda l, b: (l, b, 0, 0, 0))],
        out_shape=[jax.ShapeDtypeStruct((depth, nb, C_HEADS, m, HEAD_PAD), BF16),
                   jax.ShapeDtypeStruct((depth, nb, C_HEADS, V_ROWS,m), BF16)],
        compiler_params=_cp(32), name="cache_kv_expand",
    )(cache_ckv, cache_krope, wkn, wv_t, place)


def _mix_kernel(u_ref, va_ref, pin_ref, ws_ref, bs_ref, band_ref, wp_ref, ps_ref,
                oa_ref, ob_ref, pad_ref):
    n = u_ref.shape[0]
    width = u_ref.shape[1]
    zero_rows = jnp.zeros((POOL_HALO, width), F32)
    pad_ref[0:POOL_HALO, :] = zero_rows
    pad_ref[n + POOL_HALO:n + 2 * POOL_HALO, :] = zero_rows
    pad_ref[POOL_HALO:n + POOL_HALO, :] = pin_ref[...]
    lane_grp = lax.broadcasted_iota(jnp.int32, (CHUNK, width), 1) // POOL_GROUP
    lane_grp_w = lax.broadcasted_iota(jnp.int32, (CHUNK + 2 * POOL_HALO, width), 1) // POOL_GROUP
    row = lax.broadcasted_iota(jnp.int32, (CHUNK, width), 0)

    def group_stack(x):
        grp = lane_grp if x.shape[0] == CHUNK else lane_grp_w
        return [jnp.where(grp == g, x, 0.0) for g in range(len(POOL_WINDOWS))]

    def body(c, carry):
        r0 = pl.multiple_of(c * CHUNK, CHUNK)
        v = va_ref[pl.ds(r0, CHUNK), :].astype(F32)
        v_stack = jnp.concatenate([x.astype(BF16) for x in group_stack(v)], axis=0)
        s = _dot(ws_ref[...], v_stack) + bs_ref[...]
        oa_ref[pl.ds(r0, CHUNK), :] = (u_ref[pl.ds(r0, CHUNK), :] * s).astype(BF16)
        win = pad_ref[pl.ds(r0, CHUNK + 2 * POOL_HALO), :]
        splits = [_split2(x) for x in group_stack(win)]
        tot = (_dot(band_ref[...], jnp.concatenate([hi for hi, _ in splits], axis=0))
               + _dot(band_ref[...], jnp.concatenate([lo for _, lo in splits], axis=0)))
        t = r0 + row
        cnt = jnp.ones((CHUNK, width), jnp.int32)
        for g, w in enumerate(POOL_WINDOWS):
            left = w // 2
            right = w - 1 - left
            cg = jnp.clip(t + right + 1, 0, n) - jnp.clip(t - left, 0, n)
            cnt = jnp.where(lane_grp == g, cg, cnt)
        p = win[POOL_HALO:POOL_HALO + CHUNK, :]
        diff = (tot / cnt.astype(F32) - p).astype(BF16)
        ob_ref[pl.ds(r0, CHUNK), :] = (_dot(diff, wp_ref[...]) * ps_ref[...]).astype(BF16)
        return carry

    lax.fori_loop(0, n // CHUNK, body, 0)


def _pool_bands():
    rows = CHUNK + 2 * POOL_HALO
    band = np.zeros((len(POOL_WINDOWS), CHUNK, rows), np.float32)
    i = np.arange(CHUNK)[:, None]
    j = np.arange(rows)[None, :]
    for g, w in enumerate(POOL_WINDOWS):
        left = w // 2
        right = w - 1 - left
        band[g] = ((j >= i + POOL_HALO - left) & (j <= i + POOL_HALO + right)).astype(np.float32)
    return jnp.asarray(band.transpose(1, 0, 2).reshape(CHUNK, -1), BF16)


def _mix_call(u, va, pin, lw, band):
    nb, n, w = u.shape
    seq = lambda: pl.BlockSpec((None, n, w), lambda b: (b, 0, 0))
    const = lambda a: pl.BlockSpec(a.shape, lambda b: (0,) * a.ndim)
    return pl.pallas_call(
        _mix_kernel,
        grid=(nb,),
        in_specs=[seq(), seq(), seq(), const(lw["w_s"]), const(lw["b_s"]), const(band),
                  const(lw["w_pool"]), const(lw["pool_scale"])],
        out_specs=[seq(), seq()],
        out_shape=[jax.ShapeDtypeStruct((nb, n, w), BF16)] * 2,
        scratch_shapes=[pltpu.VMEM((n + 2 * POOL_HALO, w), F32)],
        compiler_params=_cp(48), name="mixers_ab",
    )(u, va, pin, lw["w_s"], lw["b_s"], band, lw["w_pool"], lw["pool_scale"])


def _attn_kernel(*refs, seg_lens):
    nseg = len(seg_lens)
    qt_ref = refs[0]
    k_refs = refs[1:1 + nseg]
    v_refs = refs[1 + nseg:1 + 2 * nseg]
    o_ref, ot_ref, sa_ref, sb_ref = refs[1 + 2 * nseg:]
    tq = qt_ref.shape[-1]

    def scores_into(hd, s_ref):
        q = qt_ref[hd]
        mx = None
        r0 = 0
        for k_ref, m in zip(k_refs, seg_lens):
            for c0 in range(0, m, KEY_CHUNK):
                kc = min(KEY_CHUNK, m - c0)
                s = _dot(k_ref[hd, c0:c0 + kc, :], q)
                s_ref[r0:r0 + kc, :] = s
                cm = jnp.max(s, axis=0, keepdims=True)
                mx = cm if mx is None else jnp.maximum(mx, cm)
                r0 += kc
        return mx

    def finish(hd, s_ref, mx):
        acc = jnp.zeros((V_ROWS, tq), F32)
        r0 = 0
        for v_ref, m in zip(v_refs, seg_lens):
            p = jnp.exp2(s_ref[r0:r0 + m, :] - mx).astype(BF16)
            acc = acc + _dot(v_ref[hd], p)
            r0 += m
        ot_ref[pl.ds(pl.multiple_of(hd * V_DIM, V_DIM), V_DIM), :] = acc[0:V_DIM, :] / acc[V_DIM:V_DIM + 1, :]

    def head_pair(j, mx_a):
        h1 = 2 * j + 1
        mx_b = scores_into(h1, sb_ref)
        finish(h1 - 1, sa_ref, mx_a)
        mx_next = scores_into(h1 + 1, sa_ref)
        finish(h1, sb_ref, mx_b)
        return mx_next

    mx_a = lax.fori_loop(0, (C_HEADS - 2) // 2, head_pair, scores_into(0, sa_ref))
    mx_b = scores_into(C_HEADS - 1, sb_ref)
    finish(C_HEADS - 2, sa_ref, mx_a)
    finish(C_HEADS - 1, sb_ref, mx_b)
    o_ref[...] = ot_ref[...].T.astype(BF16)


def _attn_call(qt, ks, vts):
    nb, _, _, n = qt.shape
    tq = TOK_TILE
    seg_lens = tuple(k.shape[2] for k in ks)
    in_specs = [pl.BlockSpec((None, C_HEADS, HEAD_PAD, tq), lambda b, i: (b, 0, 0, i))]
    in_specs += [pl.BlockSpec((None, C_HEADS, m, HEAD_PAD), lambda b, i: (b, 0, 0, 0)) for m in seg_lens]
    in_specs += [pl.BlockSpec((None, C_HEADS, V_ROWS,m), lambda b, i: (b, 0, 0, 0)) for m in seg_lens]
    return pl.pallas_call(
        functools.partial(_attn_kernel, seg_lens=seg_lens),
        grid=(nb, n // tq), in_specs=in_specs,
        out_specs=pl.BlockSpec((None, tq, C_WIDTH), lambda b, i: (b, i, 0)),
        out_shape=jax.ShapeDtypeStruct((nb, n, C_WIDTH), BF16),
        scratch_shapes=[pltpu.VMEM((C_WIDTH, tq), F32)] + [pltpu.VMEM((sum(seg_lens), tq), F32)] * 2,
        compiler_params=_cp(48), name="latent_attention",
    )(qt, *ks, *vts)


def _outproj_rows(rows, x_ref, oa_ref, ob_ref, oc_ref, wo_ref, g1_ref, sh_ref, sc_ref, ng_ref,
                  wr_ref, x1_ref, h_ref, aff_ref, asp_ref):
    mix = (_dot(oa_ref[rows], wo_ref[0:A_WIDTH, :])
           + _dot(ob_ref[rows], wo_ref[A_WIDTH:A_WIDTH + B_WIDTH, :])
           + _dot(oc_ref[rows], wo_ref[A_WIDTH + B_WIDTH:, :]))
    x1 = x_ref[rows] + g1_ref[...] * mix
    x1_ref[rows] = x1
    h = _rms(x1) * ng_ref[...] * (1.0 + sc_ref[...]) + sh_ref[...]
    h_hi, h_lo = _split2(h)
    h_ref[rows] = h_hi
    part = _dot(h_hi, wr_ref[...])
    logits = part + pltpu.roll(part, LANES - N_EXPERTS, axis=1) + _dot(h_lo, wr_ref[...])
    lane = lax.broadcasted_iota(jnp.int32, logits.shape, 1)
    logits = jnp.where(lane < N_EXPERTS, logits, -jnp.inf)
    ex = jnp.exp(logits - jnp.max(logits, axis=-1, keepdims=True))
    aff = ex / jnp.sum(ex, axis=-1, keepdims=True)
    aff_ref[rows] = aff
    a_hi = aff.astype(BF16).astype(F32)
    r1 = aff - a_hi
    a_lo = r1.astype(BF16).astype(F32)
    a_lo2 = r1 - a_lo
    packed = a_hi + pltpu.roll(a_lo, N_EXPERTS, axis=1) + pltpu.roll(a_lo2, 2 * N_EXPERTS, axis=1)
    asp_ref[rows] = packed.astype(BF16)


def _outproj_kernel(x_ref, *refs):
    nseq, tm = x_ref.shape[0], x_ref.shape[1]
    for s in range(nseq):
        for r0 in range(0, tm, SUB_ROWS):
            _outproj_rows((s, slice(r0, r0 + SUB_ROWS)), x_ref, *refs)


def _outproj_call(x, oa, ob, oc, modl, row0, row_step, lw):
    nb, n, d = x.shape
    sb, tm = _seq_tiling(n, row_step)
    tok = lambda c: pl.BlockSpec((sb, tm, c), lambda b, i: (b, i, 0))
    const = lambda a: pl.BlockSpec(a.shape, lambda b, i: (0,) * a.ndim)
    modspec = lambda j: pl.BlockSpec((None, None, 1, d), lambda b, i: (row0 + b * sb * row_step, j, 0, 0))
    return pl.pallas_call(
        _outproj_kernel,
        grid=(nb // sb, n // tm),
        in_specs=[tok(d), tok(A_WIDTH), tok(B_WIDTH), tok(C_WIDTH), const(lw["w_out"]),
                  modspec(2), modspec(3), modspec(4), const(lw["norm2_g"]),
                  const(lw["w_router"])],
        out_specs=[tok(d), tok(d), tok(LANES), tok(LANES)],
        out_shape=[jax.ShapeDtypeStruct((nb, n, d), F32), jax.ShapeDtypeStruct((nb, n, d), BF16),
                   jax.ShapeDtypeStruct((nb, n, LANES), F32), jax.ShapeDtypeStruct((nb, n, LANES), BF16)],
        compiler_params=_cp(48), name="out_proj_router",
    )(x, oa, ob, oc, lw["w_out"], modl, modl, modl, lw["norm2_g"], lw["w_router"])


def _route_kernel(aff_ref, pos_ref, post_ref, offs_ref, *, cap):
    nb, n, _ = aff_ref.shape
    c = TOK_TILE
    nck = n // c
    aff = aff_ref[...]
    capf = jnp.float32(cap)

    def search(i, bits):
        cand = bits | jnp.left_shift(jnp.int32(1), jnp.int32(SEARCH_BITS - 1) - i)
        cnt = jnp.sum(jnp.where(aff >= lax.bitcast_convert_type(cand, F32), 1.0, 0.0),
                      axis=1, keepdims=True)
        return jnp.where(cnt >= capf, cand, bits)

    bits = lax.fori_loop(0, SEARCH_BITS, search, jnp.zeros((nb, 1, LANES), jnp.int32))
    thr_all = lax.bitcast_convert_type(bits, F32)
    n_gt = jnp.sum(jnp.where(aff > thr_all, 1.0, 0.0), axis=1, keepdims=True)
    n_tie_all = capf - n_gt
    ri = lax.broadcasted_iota(jnp.int32, (c, c), 0)
    ci = lax.broadcasted_iota(jnp.int32, (c, c), 1)
    tri = jnp.where(ri > ci, 1.0, 0.0).astype(BF16)
    lane_ok = lax.broadcasted_iota(jnp.int32, (c, LANES), 1) < N_EXPERTS
    offs_ref[...] = jnp.zeros(offs_ref.shape, jnp.int32)
    for s in range(nb):
        thr = thr_all[s]
        n_tie = n_tie_all[s]
        tie_seen = jnp.zeros((1, LANES), F32)
        sel_seen = jnp.zeros((1, LANES), F32)
        for k in range(nck):
            kk = aff_ref[s, k * c:(k + 1) * c, :]
            gt = kk > thr
            eq = kk == thr
            eqf = jnp.where(eq, 1.0, 0.0)
            tie_rank = _dot(tri, eqf.astype(BF16)) + tie_seen
            sel = (gt | (eq & (tie_rank < n_tie))) & lane_ok
            self_ = jnp.where(sel, 1.0, 0.0)
            slot = _dot(tri, self_.astype(BF16)) + sel_seen
            pos = jnp.where(sel, slot, -1.0)
            pos_ref[s, k * c:(k + 1) * c, :] = pos
            post_ref[s, :, k * c:(k + 1) * c] = pos.T[0:N_EXPERTS, :]
            offs_ref[s, k:k + 1, :] = sel_seen[:, 0:N_EXPERTS].astype(jnp.int32)
            tie_seen = tie_seen + jnp.sum(eqf, axis=0, keepdims=True)
            sel_seen = sel_seen + jnp.sum(self_, axis=0, keepdims=True)
        offs_ref[s, nck:nck + 1, :] = sel_seen[:, 0:N_EXPERTS].astype(jnp.int32)


def _offs_rows(n):
    return -(-(n // TOK_TILE + 1) // 8) * 8


def _route_call(aff, cap):
    nb, n, _ = aff.shape
    rows = _offs_rows(n)
    return pl.pallas_call(
        functools.partial(_route_kernel, cap=cap),
        grid=(1,),
        in_specs=[pl.BlockSpec((nb, n, LANES), lambda i: (0, 0, 0))],
        out_specs=[pl.BlockSpec((nb, n, LANES), lambda i: (0, 0, 0)),
                   pl.BlockSpec((nb, N_EXPERTS, n), lambda i: (0, 0, 0)),
                   pl.BlockSpec((nb, rows, N_EXPERTS), lambda i: (0, 0, 0))],
        out_shape=[jax.ShapeDtypeStruct((nb, n, LANES), F32),
                   jax.ShapeDtypeStruct((nb, N_EXPERTS, n), F32),
                   jax.ShapeDtypeStruct((nb, rows, N_EXPERTS), jnp.int32)],
        compiler_params=_cp(32), name="ec_route",
    )(aff, )


def _window(offs_ref, b, k, e, rows, win):
    off = offs_ref[(b * rows + k) * N_EXPERTS + e]
    nxt = offs_ref[(b * rows + k + 1) * N_EXPERTS + e]
    base = lax.shift_right_logical(off, SLOT_ALIGN.bit_length() - 1) * SLOT_ALIGN
    nwin = lax.shift_right_logical(nxt - base + (win - 1), win.bit_length() - 1)
    return base, nwin


def _slot_targets(ids, start, cap, win):
    cb = pl.multiple_of(jnp.minimum(start, cap - win), SLOT_ALIGN)
    slot = ids + cb.astype(F32)
    return jnp.where(slot >= start.astype(F32), slot, -2.0), cb


def _gather_kernel(offs_ref, post_ref, h_ref, asp_ref, xs_ref, gs_ref, *, cap, win, rows):
    b = pl.program_id(0)
    k = pl.program_id(1)

    @pl.when(k == 0)
    def _():
        xs_ref[...] = jnp.zeros(xs_ref.shape, xs_ref.dtype)
        gs_ref[...] = jnp.zeros(gs_ref.shape, gs_ref.dtype)

    sub = lax.broadcasted_iota(jnp.int32, (win, 1), 0).astype(F32)
    hk = h_ref[...]
    gk = asp_ref[...]

    def sel_rows(e, start):
        tgt, cb = _slot_targets(sub, start, cap, win)
        return jnp.where(post_ref[e:e + 1, :] == tgt, 1.0, 0.0).astype(BF16), cb

    for g0 in range(0, N_EXPERTS, GATHER_GROUP):
        parts, cbs = [], []
        for e in range(g0, g0 + GATHER_GROUP):
            base, _ = _window(offs_ref, b, k, e, rows, win)
            p, cb = sel_rows(e, base)
            parts.append(p)
            cbs.append(cb)
        p_grp = jnp.concatenate(parts, axis=0)
        r_grp = _dot(p_grp, hk).astype(BF16)
        g_grp = _dot(p_grp, gk)
        for j, cb in enumerate(cbs):
            xs_ref[g0 + j, pl.ds(cb, win), :] += r_grp[j * win:(j + 1) * win, :]
            gs_ref[g0 + j, pl.ds(cb, win), :] += g_grp[j * win:(j + 1) * win, :]

    for e in range(N_EXPERTS):
        base, nwin = _window(offs_ref, b, k, e, rows, win)

        def extra(w, carry, e=e, base=base):
            p, cbw = sel_rows(e, base + w * win)
            xs_ref[e, pl.ds(cbw, win), :] += _dot(p, hk).astype(BF16)
            gs_ref[e, pl.ds(cbw, win), :] += _dot(p, gk)
            return carry

        lax.fori_loop(1, nwin, extra, 0)


def _gather_call(offs, post, h, asp, cap, win):
    nb, n, d = h.shape
    c = TOK_TILE
    rows = _offs_rows(n)
    grid_spec = pltpu.PrefetchScalarGridSpec(
        num_scalar_prefetch=1, grid=(nb, n // c),
        in_specs=[pl.BlockSpec((None, N_EXPERTS, c), lambda b, k, o: (b, 0, k)),
                  pl.BlockSpec((None, c, d), lambda b, k, o: (b, k, 0)),
                  pl.BlockSpec((None, c, LANES), lambda b, k, o: (b, k, 0))],
        out_specs=[pl.BlockSpec((N_EXPERTS, cap, d), lambda b, k, o: (0, b, 0)),
                   pl.BlockSpec((N_EXPERTS, cap, LANES), lambda b, k, o: (0, b, 0))])
    return pl.pallas_call(
        functools.partial(_gather_kernel, cap=cap, win=win, rows=rows),
        grid_spec=grid_spec,
        out_shape=[jax.ShapeDtypeStruct((N_EXPERTS, nb * cap, d), BF16),
                   jax.ShapeDtypeStruct((N_EXPERTS, nb * cap, LANES), F32)],
        compiler_params=_cp(48), name="ec_gather",
    )(offs.reshape(-1), post, h, asp)


def _ffn_kernel(xa_ref, xb_ref, ga_ref, gb_ref, w1_ref, w3_ref, w2_ref, ya_ref, yb_ref,
                acca_ref, accb_ref):
    e = pl.program_id(0)
    f = pl.program_id(1)
    w1 = w1_ref[...].astype(BF16)
    w3 = w3_ref[...].astype(BF16)
    w2 = w2_ref[...].astype(BF16)

    @pl.when((e == 0) & (f == 0))
    def _():
        acca_ref[...] = jnp.zeros(acca_ref.shape, F32)
        accb_ref[...] = jnp.zeros(accb_ref.shape, F32)

    for x_ref, g_ref, y_ref, acc_ref in ((xa_ref, ga_ref, ya_ref, acca_ref),
                                         (xb_ref, gb_ref, yb_ref, accb_ref)):
        x = x_ref[...]
        hid = jax.nn.silu(_dot(x, w1)) * _dot(x, w3)
        acc = jnp.where(f > 0, acc_ref[...], 0.0) + _dot(hid.astype(BF16), w2)
        acc_ref[...] = acc
        g = g_ref[...]
        lane = lax.broadcasted_iota(jnp.int32, g.shape, 1)
        keep = ((lane % N_EXPERTS) == e) & (lane < 3 * N_EXPERTS)
        gate = jnp.sum(jnp.where(keep, g, 0.0), axis=-1, keepdims=True)
        y_ref[...] = (acc * gate).astype(BF16)


def _ffn_call(xa, xb, ga, gb, w1, w3, w2, layer):
    ne, ra, d = xa.shape
    rb = xb.shape[1]
    ff = w1.shape[3]
    tf = FF_TILE
    rowsp = lambda r, c: pl.BlockSpec((None, r, c), lambda e, f: (e, 0, 0))
    return pl.pallas_call(
        _ffn_kernel,
        grid=(ne, ff // tf),
        in_specs=[rowsp(ra, d), rowsp(rb, d), rowsp(ra, LANES), rowsp(rb, LANES),
                  pl.BlockSpec((None, None, d, tf), lambda e, f: (layer, e, 0, f)),
                  pl.BlockSpec((None, None, d, tf), lambda e, f: (layer, e, 0, f)),
                  pl.BlockSpec((None, None, tf, d), lambda e, f: (layer, e, f, 0))],
        out_specs=[rowsp(ra, d), rowsp(rb, d)],
        out_shape=[jax.ShapeDtypeStruct((ne, ra, d), BF16), jax.ShapeDtypeStruct((ne, rb, d), BF16)],
        scratch_shapes=[pltpu.VMEM((ra, d), F32), pltpu.VMEM((rb, d), F32)],
        compiler_params=_cp(56), name="expert_swiglu",
    )(xa, xb, ga, gb, w1, w3, w2)


def _combine_kernel(offs_ref, pos_ref, y_ref, x_ref, g2_ref, *rest, cap, win, rows, last):
    if last:
        fg_ref, o_ref, on_ref = rest
    else:
        (o_ref,) = rest
    b = pl.program_id(0)
    k = pl.program_id(1)
    c = x_ref.shape[0]
    per = MXU_DEPTH // win
    depth = per * win
    lane = lax.broadcasted_iota(jnp.int32, (1, depth), 1)
    lane_grp = lane // win
    lane_in = (lane % win).astype(F32)
    lane1 = lax.broadcasted_iota(jnp.int32, (1, win), 1).astype(F32)
    tot = None
    for g0 in range(0, N_EXPERTS, per):
        pmat = None
        tgt = jnp.full((1, depth), -2.0, F32)
        ys = []
        for j in range(per):
            e = g0 + j
            base, _ = _window(offs_ref, b, k, e, rows, win)
            tgt_e, cb = _slot_targets(lane_in, base, cap, win)
            here = lane_grp == j
            col = pos_ref[:, e:e + 1]
            pmat = jnp.broadcast_to(col, (c, depth)) if pmat is None else jnp.where(here, col, pmat)
            tgt = jnp.where(here, tgt_e, tgt)
            ys.append(y_ref[e, pl.ds(cb, win), :])
        d = _dot(jnp.where(pmat == tgt, 1.0, 0.0).astype(BF16), jnp.concatenate(ys, axis=0))
        tot = d if tot is None else tot + d
    o_ref[...] = x_ref[...] + g2_ref[...] * tot

    for e in range(N_EXPERTS):
        base, nwin = _window(offs_ref, b, k, e, rows, win)

        def extra(w, carry, e=e, base=base):
            tgt_w, cbw = _slot_targets(lane1, base + w * win, cap, win)
            pt = jnp.where(pos_ref[:, e:e + 1] == tgt_w, 1.0, 0.0).astype(BF16)
            o_ref[...] += g2_ref[...] * _dot(pt, y_ref[e, pl.ds(cbw, win), :])
            return carry

        lax.fori_loop(1, nwin, extra, 0)
    if last:
        on_ref[...] = _rms(o_ref[...]) * fg_ref[...]


def _combine_call(offs, pos, y, x1, modl, row0, row_step, final_g, cap, win):
    nb, n, d = x1.shape
    c = TOK_TILE
    rows = _offs_rows(n)
    last = final_g is not None
    tok = pl.BlockSpec((None, c, d), lambda b, k, o: (b, k, 0))
    in_specs = [pl.BlockSpec((None, c, LANES), lambda b, k, o: (b, k, 0)),
                pl.BlockSpec((N_EXPERTS, cap, d), lambda b, k, o: (0, b, 0)),
                tok,
                pl.BlockSpec((None, None, 1, d), lambda b, k, o: (row0 + b * row_step, 5, 0, 0))]
    ins = [offs.reshape(-1), pos, y, x1, modl]
    if last:
        in_specs.append(pl.BlockSpec((1, d), lambda b, k, o: (0, 0)))
        ins.append(final_g)
    nout = 2 if last else 1
    grid_spec = pltpu.PrefetchScalarGridSpec(
        num_scalar_prefetch=1, grid=(nb, n // c), in_specs=in_specs, out_specs=[tok] * nout)
    return pl.pallas_call(
        functools.partial(_combine_kernel, cap=cap, win=win, rows=rows, last=last),
        grid_spec=grid_spec,
        out_shape=[jax.ShapeDtypeStruct((nb, n, d), F32)] * nout,
        compiler_params=_cp(56), name="ec_combine_final" if last else "ec_combine",
    )(*ins)


def _prep_weights(w_in, w_s, b_s, w_pool, pool_scale, q_norm_g, w_qb, kv_norm_g, w_kvb, w_out,
                  w_router, norm1_g, norm2_g):
    depth, d, cols = w_in.shape
    row = lambda a: a.reshape(depth, 1, -1)
    w_in_p = jnp.pad(w_in, ((0, 0), (0, 0), (0, PROJ_PAD - cols))).astype(BF16)
    wq = w_qb.reshape(depth, Q_LORA, C_HEADS, QK_NOPE + QK_ROPE)
    wq = jnp.pad(wq, ((0, 0), (0, 0), (0, 0), (0, HEAD_PAD - QK_NOPE - QK_ROPE)))
    wq_t = wq.reshape(depth, Q_LORA, C_HEADS * HEAD_PAD).transpose(0, 2, 1).astype(BF16)
    wkv = w_kvb.reshape(depth, KV_LORA, C_HEADS, QK_NOPE + V_DIM)
    wkn = jnp.pad(wkv[..., :QK_NOPE], ((0, 0), (0, 0), (0, 0), (0, HEAD_PAD - QK_NOPE)))
    wkn = wkn.reshape(depth, KV_LORA, C_HEADS * HEAD_PAD).astype(BF16)
    wv_t = wkv[..., QK_NOPE:].reshape(depth, KV_LORA, C_WIDTH).transpose(0, 2, 1).astype(BF16)
    assert A_HEADS == len(POOL_WINDOWS) and A_HEAD_DIM == POOL_GROUP
    ws_cat = w_s.transpose(0, 2, 1, 3).reshape(depth, CHUNK, A_HEADS * CHUNK).astype(BF16)
    b_rep = jnp.repeat(b_s.transpose(0, 2, 1), A_HEAD_DIM, axis=2)
    eye = jnp.eye(len(POOL_WINDOWS), dtype=F32)
    wp_bd = jnp.einsum("gh,lgcd->lgchd", eye, w_pool).reshape(depth, B_WIDTH, B_WIDTH).astype(BF16)
    wr_hi = w_router.astype(BF16)
    wr_lo = (w_router - wr_hi.astype(F32)).astype(BF16)
    wr = jnp.pad(jnp.concatenate([wr_hi, wr_lo], axis=2), ((0, 0), (0, 0), (0, LANES - 2 * N_EXPERTS)))
    per_layer = []
    for l in range(depth):
        per_layer.append(dict(
            norm1_g=row(norm1_g)[l], norm2_g=row(norm2_g)[l], w_in=w_in_p[l],
            q_norm_g=row(q_norm_g)[l], kv_norm_g=row(kv_norm_g)[l], wq_t=wq_t[l], wkn=wkn[l],
            wv_t=wv_t[l], w_s=ws_cat[l], b_s=b_rep[l], w_pool=wp_bd[l],
            pool_scale=row(pool_scale)[l], w_out=w_out[l].astype(BF16), w_router=wr[l]))
    return per_layer, wkn, wv_t


def _rope_tables(n):
    rows = n // GRID_W
    row = jnp.repeat(jnp.arange(rows), GRID_W).astype(F32)
    col = jnp.broadcast_to(jnp.arange(GRID_W), (rows, GRID_W)).reshape(-1).astype(F32)
    inv = 1.0 / (ROPE_THETA ** (jnp.arange(0, ROPE_AXIS, 2, dtype=F32) / ROPE_AXIS))
    ang_r = row[:, None] * inv[None, :]
    ang_c = col[:, None] * inv[None, :]
    cos_r, sin_r, cos_c, sin_c = jnp.cos(ang_r), jnp.sin(ang_r), jnp.cos(ang_c), jnp.sin(ang_c)
    pad = jnp.zeros((n, LANES - QK_ROPE), F32)
    cos_k = jnp.concatenate([cos_r, cos_r, cos_c, cos_c, pad], axis=1)
    sin_k = jnp.concatenate([-sin_r, sin_r, -sin_c, sin_c, pad], axis=1)
    return cos_k, sin_k, cos_r.T, sin_r.T, cos_c.T, sin_c.T


def kernel(x_prompt, x_sample, cache_ckv, cache_krope, c, c_ctx, w_ada, b_ada, norm1_g, norm2_g, w_in, w_s, b_s, w_pool, pool_scale, q_norm_g, w_qb, kv_norm_g, w_kvb, w_out, w_router, w_e1, w_e3, w_e2, final_norm_g):
    depth, d, _ = w_ada.shape
    nb_c, n_c, _ = x_prompt.shape
    nb_s, n_s, _ = x_sample.shape
    assert 1 + nb_s <= 8
    cond8 = jnp.concatenate([c_ctx[None, :], c, jnp.zeros((8 - 1 - nb_s, d), F32)], axis=0)
    mod = _mod_call(cond8, w_ada, b_ada).reshape(depth, 8, 6, 1, d)
    layers, wkn_all, wvt_all = _prep_weights(w_in, w_s, b_s, w_pool, pool_scale, q_norm_g, w_qb,
                                             kv_norm_g, w_kvb, w_out, w_router, norm1_g, norm2_g)
    k_cache, vt_cache = _kvexp_call(cache_ckv, cache_krope, wkn_all, wvt_all)
    tabs = _rope_tables(n_s)
    band = _pool_bands()
    final_g = final_norm_g.reshape(1, d)
    cap_c = EC_CAPACITY * n_c // N_EXPERTS
    cap_s = EC_CAPACITY * n_s // N_EXPERTS
    win_c = min(cap_c, 64)
    win_s = min(cap_s, 64)

    xc, xs = x_prompt, x_sample
    yc = ys = None
    ckvs, kropes = [], []
    for l in range(depth):
        lw = layers[l]
        modl = mod[l]
        uc, vac, pinc, qtc, khc, vtc, ckv_c, kr_c = _proj_call(xc, modl, 0, 0, lw, None)
        us, vas, pins, qts, khs, vts, _, _ = _proj_call(xs, modl, 1, 1, lw, tabs)
        ckvs.append(ckv_c)
        kropes.append(kr_c)
        oac, obc = _mix_call(uc, vac, pinc, lw, band)
        oas, obs = _mix_call(us, vas, pins, lw, band)
        occ = _attn_call(qtc, [khc], [vtc])
        ocs = _attn_call(qts, [k_cache[l], khs], [vt_cache[l], vts])
        x1c, hc, affc, aspc = _outproj_call(xc, oac, obc, occ, modl, 0, 0, lw)
        x1s, hs, affs, asps = _outproj_call(xs, oas, obs, ocs, modl, 1, 1, lw)
        posc, postc, offc = _route_call(affc, cap_c)
        poss, posts, offs = _route_call(affs, cap_s)
        xgc, ggc = _gather_call(offc, postc, hc, aspc, cap_c, win_c)
        xgs, ggs = _gather_call(offs, posts, hs, asps, cap_s, win_s)
        ygs, ygc = _ffn_call(xgs, xgc, ggs, ggc, w_e1, w_e3, w_e2, l)
        fg = final_g if l == depth - 1 else None
        outc = _combine_call(offc, posc, ygc, x1c, modl, 0, 0, fg, cap_c, win_c)
        outs = _combine_call(offs, poss, ygs, x1s, modl, 1, 1, fg, cap_s, win_s)
        xc, xs = outc[0], outs[0]
    yc, ys = outc[1], outs[1]
    new_ckv = jnp.stack(ckvs, axis=1)
    new_krope = jnp.stack(kropes, axis=1)
    return (yc, ys, new_ckv, new_krope)
```

```python
import functools
import math

import numpy as np
import jax
import jax.numpy as jnp
from jax import lax
from jax.experimental import pallas as pl
from jax.experimental.pallas import tpu as pltpu

F32 = jnp.float32
BF16 = jnp.bfloat16

GRID_W = 64
A_HEADS = 4
A_HEAD_DIM = 64
A_WIDTH = A_HEADS * A_HEAD_DIM
CHUNK = 128
POOL_WINDOWS = (2, 4, 8, 16)
POOL_GROUP = 64
B_WIDTH = len(POOL_WINDOWS) * POOL_GROUP
C_HEADS = 8
QK_NOPE = 64
QK_ROPE = 32
V_DIM = 64
Q_LORA = 384
KV_LORA = 256
C_WIDTH = C_HEADS * V_DIM
ROPE_AXIS = QK_ROPE // 2
ROPE_THETA = 10000.0
ATTN_SCALE = (QK_NOPE + QK_ROPE) ** -0.5
N_EXPERTS = 16
EC_CAPACITY = 2
NORM_EPS = 1e-6

LANES = 128
HEAD_PAD = 128
PROJ_PAD = 1536
OFF_P = 2 * A_WIDTH
OFF_CQ = OFF_P + B_WIDTH
OFF_CKV = OFF_CQ + Q_LORA
OFF_KR = OFF_CKV + KV_LORA
POOL_HALO = 8
TOK_TILE = 256
ROW_TILE = 512
SUB_ROWS = 256
V_ROWS = V_DIM + 16
KEY_CHUNK = 512
ATTN_Q_BLOCKS = 2
Q_PRESCALE = ATTN_SCALE * math.log2(math.e)
FF_TILE = 512
MOD_TILE = 1536
SEARCH_BITS = 31
SLOT_ALIGN = 16
GATHER_GROUP = 4
MXU_DEPTH = 256


def _cp(vmem_mb):
    return pltpu.CompilerParams(vmem_limit_bytes=vmem_mb * 1024 * 1024)


def _layer_spec(a, layer):
    return pl.BlockSpec((None,) + a.shape[1:], lambda *_: (layer,) + (0,) * (a.ndim - 1))


def _dot(a, b):
    return jnp.dot(a, b, preferred_element_type=F32)


def _dot_nt(a, b):
    return lax.dot_general(a, b, (((1,), (1,)), ((), ())), preferred_element_type=F32)


def _split2(x):
    hi = x.astype(BF16)
    lo = (x - hi.astype(F32)).astype(BF16)
    return hi, lo


def _rms(x):
    return x * lax.rsqrt(jnp.mean(x * x, axis=-1, keepdims=True) + NORM_EPS)


def _load_rows(ref):
    return jnp.concatenate([ref[s] for s in range(ref.shape[0])], axis=0)


def _store_rows(ref, val):
    tm = ref.shape[1]
    for s in range(ref.shape[0]):
        ref[s] = val[s * tm:(s + 1) * tm, :]


def _with_ones_row(vt):
    rows = lax.broadcasted_iota(jnp.int32, (V_ROWS - V_DIM, vt.shape[1]), 0)
    return jnp.concatenate([vt, jnp.where(rows == 0, 1.0, 0.0).astype(BF16)], axis=0)


def _mod_kernel(c_ref, w_ref, b_ref, o_ref):
    a = jax.nn.silu(c_ref[...])
    a_hi, a_lo = _split2(a)
    w_hi, w_lo = _split2(w_ref[...])
    o_ref[...] = _dot(a_hi, w_hi) + _dot(a_lo, w_hi) + _dot(a_hi, w_lo) + b_ref[...]


def _mod_call(cond8, w_ada, b_ada):
    depth, d, n = w_ada.shape
    return pl.pallas_call(
        _mod_kernel,
        grid=(depth, n // MOD_TILE),
        in_specs=[pl.BlockSpec((8, d), lambda l, j: (0, 0)),
                  pl.BlockSpec((None, d, MOD_TILE), lambda l, j: (l, 0, j)),
                  pl.BlockSpec((None, 1, MOD_TILE), lambda l, j: (l, 0, j))],
        out_specs=pl.BlockSpec((None, 8, MOD_TILE), lambda l, j: (l, 0, j)),
        out_shape=jax.ShapeDtypeStruct((depth, 8, n), F32),
        compiler_params=_cp(40),
        name="adaln_mod",
    )(cond8, w_ada, b_ada.reshape(depth, 1, n))


def _proj_kernel(*refs, rope):
    if rope:
        (x_ref, sh_ref, sc_ref, g_ref, win_ref, qg_ref, kvg_ref, wq_ref, wkn_ref, wvt_ref,
         ck_ref, sk_ref, cr_ref, sr_ref, cc_ref, scl_ref,
         u_ref, va_ref, pin_ref, qt_ref, kh_ref, vt_ref, ckv_ref, kr_ref) = refs
    else:
        (x_ref, sh_ref, sc_ref, g_ref, win_ref, qg_ref, kvg_ref, wq_ref, wkn_ref, wvt_ref,
         u_ref, va_ref, pin_ref, qt_ref, kh_ref, vt_ref, ckv_ref, kr_ref) = refs
    nseq, tm = x_ref.shape[0], x_ref.shape[1]
    x = _load_rows(x_ref)
    h = _rms(x) * g_ref[...] * (1.0 + sc_ref[...]) + sh_ref[...]
    proj = _dot(h.astype(BF16), win_ref[...])
    _store_rows(u_ref, jax.nn.gelu(proj[:, 0:A_WIDTH]))
    _store_rows(va_ref, jax.nn.gelu(proj[:, A_WIDTH:OFF_P]).astype(BF16))
    _store_rows(pin_ref, proj[:, OFF_P:OFF_CQ])
    cq = _rms(proj[:, OFF_CQ:OFF_CKV]) * qg_ref[...]
    ckv = _rms(proj[:, OFF_CKV:OFF_KR]) * kvg_ref[...]
    _store_rows(ckv_ref, ckv)
    kr = proj[:, OFF_KR:OFF_KR + LANES]
    _store_rows(kr_ref, kr[:, 0:QK_ROPE])

    qt = _dot_nt(wq_ref[...], cq.astype(BF16))
    ckv_b = ckv.astype(BF16)
    kn = _dot(ckv_b, wkn_ref[...])
    vt = _dot_nt(wvt_ref[...], ckv_b)

    if rope:
        lane = lax.broadcasted_iota(jnp.int32, kr.shape, 1)
        half = ROPE_AXIS // 2
        swapped = jnp.where((lane & half) == 0,
                            pltpu.roll(kr, LANES - half, axis=1), pltpu.roll(kr, half, axis=1))
        kr = kr * ck_ref[...] + swapped * sk_ref[...]
        cr, sr, cc, scl = cr_ref[...], sr_ref[...], cc_ref[...], scl_ref[...]
    kr_shift = pltpu.roll(kr, QK_NOPE, axis=1)

    for hd in range(C_HEADS):
        r0 = hd * HEAD_PAD
        q_h = qt[r0:r0 + HEAD_PAD, :]
        if rope:
            b0 = QK_NOPE
            a, b = q_h[b0:b0 + 8, :], q_h[b0 + 8:b0 + 16, :]
            c, d = q_h[b0 + 16:b0 + 24, :], q_h[b0 + 24:b0 + 32, :]
            q_h = jnp.concatenate(
                [q_h[0:b0, :], a * cr - b * sr, a * sr + b * cr, c * cc - d * scl, c * scl + d * cc,
                 q_h[b0 + 32:, :]], axis=0)
        q_h = (q_h * Q_PRESCALE).astype(BF16)
        k_h = (kn[:, r0:r0 + HEAD_PAD] + kr_shift).astype(BF16)
        v_h = _with_ones_row(vt[hd * V_DIM:(hd + 1) * V_DIM, :].astype(BF16))
        for s in range(nseq):
            for j in range(tm // TOK_TILE):
                c0 = s * tm + j * TOK_TILE
                qt_ref[s, j, hd] = q_h[:, c0:c0 + TOK_TILE]
            kh_ref[s, hd] = k_h[s * tm:(s + 1) * tm, :]
            vt_ref[s, hd] = v_h[:, s * tm:(s + 1) * tm]


def _seq_tiling(n, row_step):
    tm = min(n, ROW_TILE)
    return (ROW_TILE // tm if row_step == 0 else 1), tm


def _proj_call(x, modl, row0, row_step, lw, tabs, layer):
    nb, n, d = x.shape
    sb, tm = _seq_tiling(n, row_step)
    rope = tabs is not None
    assert not rope or sb == 1
    tok = lambda c: pl.BlockSpec((sb, tm, c), lambda b, i: (b, i, 0))
    const = lambda a: _layer_spec(a, layer)
    modspec = lambda j: pl.BlockSpec((None, None, None, 1, d),
                                     lambda b, i: (layer, row0 + b * sb * row_step, j, 0, 0))
    ins = [x, modl, modl, lw["norm1_g"], lw["w_in"], lw["q_norm_g"], lw["kv_norm_g"],
           lw["wq_t"], lw["wkn"], lw["wv_t"]]
    in_specs = [tok(d), modspec(0), modspec(1), const(lw["norm1_g"]), const(lw["w_in"]),
                const(lw["q_norm_g"]), const(lw["kv_norm_g"]), const(lw["wq_t"]), const(lw["wkn"]),
                const(lw["wv_t"])]
    if rope:
        cos_k, sin_k, cos_r, sin_r, cos_c, sin_c = tabs
        ins += [cos_k, sin_k, cos_r, sin_r, cos_c, sin_c]
        in_specs += [pl.BlockSpec((tm, LANES), lambda b, i: (i, 0))] * 2
        in_specs += [pl.BlockSpec((8, tm), lambda b, i: (0, i))] * 4
    out_shape = [jax.ShapeDtypeStruct((nb, n, A_WIDTH), F32),
                 jax.ShapeDtypeStruct((nb, n, A_WIDTH), BF16),
                 jax.ShapeDtypeStruct((nb, n, B_WIDTH), F32),
                 jax.ShapeDtypeStruct((nb, n // TOK_TILE, C_HEADS, HEAD_PAD, TOK_TILE), BF16),
                 jax.ShapeDtypeStruct((nb, C_HEADS, n, HEAD_PAD), BF16),
                 jax.ShapeDtypeStruct((nb, C_HEADS, V_ROWS,n), BF16),
                 jax.ShapeDtypeStruct((nb, n, KV_LORA), F32),
                 jax.ShapeDtypeStruct((nb, n, QK_ROPE), F32)]
    out_specs = [tok(A_WIDTH), tok(A_WIDTH), tok(B_WIDTH),
                 pl.BlockSpec((sb, tm // TOK_TILE, C_HEADS, HEAD_PAD, TOK_TILE), lambda b, i: (b, i, 0, 0, 0)),
                 pl.BlockSpec((sb, C_HEADS, tm, HEAD_PAD), lambda b, i: (b, 0, i, 0)),
                 pl.BlockSpec((sb, C_HEADS, V_ROWS, tm), lambda b, i: (b, 0, 0, i)),
                 tok(KV_LORA), tok(QK_ROPE)]
    return pl.pallas_call(
        functools.partial(_proj_kernel, rope=rope),
        grid=(nb // sb, n // tm), in_specs=in_specs, out_specs=out_specs, out_shape=out_shape,
        compiler_params=_cp(48), name="in_proj_rope" if rope else "in_proj",
    )(*ins)


def _kvexp_kernel(ckv_ref, kr_ref, wkn_ref, wvt_ref, place_ref, kh_ref, vt_ref):
    ckv_b = ckv_ref[...].astype(BF16)
    kn = _dot(ckv_b, wkn_ref[...])
    vt = _dot_nt(wvt_ref[...], ckv_b)
    kr_placed = _dot(kr_ref[...].astype(BF16), place_ref[...])
    for hd in range(C_HEADS):
        r0 = hd * HEAD_PAD
        kh_ref[hd] = (kn[:, r0:r0 + HEAD_PAD] + kr_placed).astype(BF16)
        vt_ref[hd] = _with_ones_row(vt[hd * V_DIM:(hd + 1) * V_DIM, :].astype(BF16))


def _kvexp_call(cache_ckv, cache_krope, wkn, wv_t):
    nb, depth, m, r = cache_ckv.shape
    place = np.zeros((QK_ROPE, HEAD_PAD), np.float32)
    place[np.arange(QK_ROPE), QK_NOPE + np.arange(QK_ROPE)] = 1.0
    place = jnp.asarray(place, BF16)
    return pl.pallas_call(
        _kvexp_kernel,
        grid=(depth, nb),
        in_specs=[pl.BlockSpec((None, None, m, r), lambda l, b: (b, l, 0, 0)),
                  pl.BlockSpec((None, None, m, QK_ROPE), lambda l, b: (b, l, 0, 0)),
                  pl.BlockSpec((None,) + wkn.shape[1:], lambda l, b: (l, 0, 0)),
                  pl.BlockSpec((None,) + wv_t.shape[1:], lambda l, b: (l, 0, 0)),
                  pl.BlockSpec(place.shape, lambda l, b: (0, 0))],
        out_specs=[pl.BlockSpec((None, None, C_HEADS, m, HEAD_PAD), lambda l, b: (l, b, 0, 0, 0)),
                   pl.BlockSpec((None, None, C_HEADS, V_ROWS,m), lambda l, b: (l, b, 0, 0, 0))],
        out_shape=[jax.ShapeDtypeStruct((depth, nb, C_HEADS, m, HEAD_PAD), BF16),
                   jax.ShapeDtypeStruct((depth, nb, C_HEADS, V_ROWS,m), BF16)],
        compiler_params=_cp(32), name="cache_kv_expand",
    )(cache_ckv, cache_krope, wkn, wv_t, place)


def _mix_kernel(u_ref, va_ref, pin_ref, ws_ref, bs_ref, band_ref, wp_ref, ps_ref,
                oa_ref, ob_ref, pad_ref):
    n = u_ref.shape[0]
    width = u_ref.shape[1]
    zero_rows = jnp.zeros((POOL_HALO, width), F32)
    pad_ref[0:POOL_HALO, :] = zero_rows
    pad_ref[n + POOL_HALO:n + 2 * POOL_HALO, :] = zero_rows
    pad_ref[POOL_HALO:n + POOL_HALO, :] = pin_ref[...]
    lane_grp = lax.broadcasted_iota(jnp.int32, (CHUNK, width), 1) // POOL_GROUP
    lane_grp_w = lax.broadcasted_iota(jnp.int32, (CHUNK + 2 * POOL_HALO, width), 1) // POOL_GROUP
    row = lax.broadcasted_iota(jnp.int32, (CHUNK, width), 0)

    def group_stack(x):
        grp = lane_grp if x.shape[0] == CHUNK else lane_grp_w
        return [jnp.where(grp == g, x, 0.0) for g in range(len(POOL_WINDOWS))]

    def body(c, carry):
        r0 = pl.multiple_of(c * CHUNK, CHUNK)
        v = va_ref[pl.ds(r0, CHUNK), :].astype(F32)
        v_stack = jnp.concatenate([x.astype(BF16) for x in group_stack(v)], axis=0)
        s = _dot(ws_ref[...], v_stack) + bs_ref[...]
        oa_ref[pl.ds(r0, CHUNK), :] = (u_ref[pl.ds(r0, CHUNK), :] * s).astype(BF16)
        win = pad_ref[pl.ds(r0, CHUNK + 2 * POOL_HALO), :]
        splits = [_split2(x) for x in group_stack(win)]
        tot = (_dot(band_ref[...], jnp.concatenate([hi for hi, _ in splits], axis=0))
               + _dot(band_ref[...], jnp.concatenate([lo for _, lo in splits], axis=0)))
        t = r0 + row
        cnt = jnp.ones((CHUNK, width), jnp.int32)
        for g, w in enumerate(POOL_WINDOWS):
            left = w // 2
            right = w - 1 - left
            cg = jnp.clip(t + right + 1, 0, n) - jnp.clip(t - left, 0, n)
            cnt = jnp.where(lane_grp == g, cg, cnt)
        p = win[POOL_HALO:POOL_HALO + CHUNK, :]
        diff = (tot / cnt.astype(F32) - p).astype(BF16)
        ob_ref[pl.ds(r0, CHUNK), :] = (_dot(diff, wp_ref[...]) * ps_ref[...]).astype(BF16)
        return carry

    lax.fori_loop(0, n // CHUNK, body, 0)


def _pool_bands():
    rows = CHUNK + 2 * POOL_HALO
    band = np.zeros((len(POOL_WINDOWS), CHUNK, rows), np.float32)
    i = np.arange(CHUNK)[:, None]
    j = np.arange(rows)[None, :]
    for g, w in enumerate(POOL_WINDOWS):
        left = w // 2
        right = w - 1 - left
        band[g] = ((j >= i + POOL_HALO - left) & (j <= i + POOL_HALO + right)).astype(np.float32)
    return jnp.asarray(band.transpose(1, 0, 2).reshape(CHUNK, -1), BF16)


def _mix_call(u, va, pin, lw, band, layer):
    nb, n, w = u.shape
    seq = lambda: pl.BlockSpec((None, n, w), lambda b: (b, 0, 0))
    const = lambda a: _layer_spec(a, layer)
    return pl.pallas_call(
        _mix_kernel,
        grid=(nb,),
        in_specs=[seq(), seq(), seq(), const(lw["w_s"]), const(lw["b_s"]),
                  pl.BlockSpec(band.shape, lambda b: (0, 0)),
                  const(lw["w_pool"]), const(lw["pool_scale"])],
        out_specs=[seq(), seq()],
        out_shape=[jax.ShapeDtypeStruct((nb, n, w), BF16)] * 2,
        scratch_shapes=[pltpu.VMEM((n + 2 * POOL_HALO, w), F32)],
        compiler_params=_cp(48), name="mixers_ab",
    )(u, va, pin, lw["w_s"], lw["b_s"], band, lw["w_pool"], lw["pool_scale"])


def _attn_kernel(*refs, seg_lens):
    nseg = len(seg_lens)
    qt_ref = refs[0]
    k_refs = refs[1:1 + nseg]
    v_refs = refs[1 + nseg:1 + 2 * nseg]
    o_ref, ot_ref, sa_ref, sb_ref = refs[1 + 2 * nseg:]
    nqb, _, _, tq = qt_ref.shape
    nitem = nqb * C_HEADS
    head_bits = C_HEADS.bit_length() - 1
    assert C_HEADS == 1 << head_bits

    def scores_into(it, s_ref):
        qb, hd = it >> head_bits, it & (C_HEADS - 1)
        q = qt_ref[qb, hd]
        mx = None
        r0 = 0
        for k_ref, m in zip(k_refs, seg_lens):
            for c0 in range(0, m, KEY_CHUNK):
                kc = min(KEY_CHUNK, m - c0)
                s = _dot(k_ref[hd, c0:c0 + kc, :], q)
                s_ref[r0:r0 + kc, :] = s
                cm = jnp.max(s, axis=0, keepdims=True)
                mx = cm if mx is None else jnp.maximum(mx, cm)
                r0 += kc
        return mx

    def finish(it, s_ref, mx):
        qb, hd = it >> head_bits, it & (C_HEADS - 1)
        acc = jnp.zeros((V_ROWS, tq), F32)
        r0 = 0
        for v_ref, m in zip(v_refs, seg_lens):
            p = jnp.exp2(s_ref[r0:r0 + m, :] - mx).astype(BF16)
            acc = acc + _dot(v_ref[hd], p)
            r0 += m
        rows = pl.ds(pl.multiple_of(hd * V_DIM, V_DIM), V_DIM)
        ot_ref[qb, rows, :] = acc[0:V_DIM, :] / acc[V_DIM:V_DIM + 1, :]

    def item_pair(j, mx_a):
        i1 = 2 * j + 1
        mx_b = scores_into(i1, sb_ref)
        finish(i1 - 1, sa_ref, mx_a)
        mx_next = scores_into(i1 + 1, sa_ref)
        finish(i1, sb_ref, mx_b)
        return mx_next

    mx_a = lax.fori_loop(0, (nitem - 2) // 2, item_pair, scores_into(0, sa_ref))
    mx_b = scores_into(nitem - 1, sb_ref)
    finish(nitem - 2, sa_ref, mx_a)
    finish(nitem - 1, sb_ref, mx_b)
    for qb in range(nqb):
        o_ref[qb * tq:(qb + 1) * tq, :] = ot_ref[qb].T.astype(BF16)


def _attn_call(qt, segs):
    nb, nblk, _, _, tq = qt.shape
    nqb = min(nblk, ATTN_Q_BLOCKS)
    seg_lens = tuple(k.shape[-2] for k, _, _ in segs)

    def seg_spec(a, layer):
        if layer is None:
            return pl.BlockSpec((None,) + a.shape[1:], lambda b, i: (b, 0, 0, 0))
        return pl.BlockSpec((None, None) + a.shape[2:], lambda b, i: (layer, b, 0, 0, 0))

    in_specs = [pl.BlockSpec((None, nqb, C_HEADS, HEAD_PAD, tq), lambda b, i: (b, i, 0, 0, 0))]
    in_specs += [seg_spec(k, layer) for k, _, layer in segs]
    in_specs += [seg_spec(v, layer) for _, v, layer in segs]
    return pl.pallas_call(
        functools.partial(_attn_kernel, seg_lens=seg_lens),
        grid=(nb, nblk // nqb), in_specs=in_specs,
        out_specs=pl.BlockSpec((None, nqb * tq, C_WIDTH), lambda b, i: (b, i, 0)),
        out_shape=jax.ShapeDtypeStruct((nb, nblk * tq, C_WIDTH), BF16),
        scratch_shapes=[pltpu.VMEM((nqb, C_WIDTH, tq), F32)] + [pltpu.VMEM((sum(seg_lens), tq), F32)] * 2,
        compiler_params=_cp(48), name="latent_attention",
    )(qt, *[k for k, _, _ in segs], *[v for _, v, _ in segs])


def _outproj_rows(rows, x_ref, oa_ref, ob_ref, oc_ref, wo_ref, g1_ref, sh_ref, sc_ref, ng_ref,
                  wr_ref, x1_ref, h_ref, aff_ref, asp_ref):
    mix = (_dot(oa_ref[rows], wo_ref[0:A_WIDTH, :])
           + _dot(ob_ref[rows], wo_ref[A_WIDTH:A_WIDTH + B_WIDTH, :])
           + _dot(oc_ref[rows], wo_ref[A_WIDTH + B_WIDTH:, :]))
    x1 = x_ref[rows] + g1_ref[...] * mix
    x1_ref[rows] = x1
    h = _rms(x1) * ng_ref[...] * (1.0 + sc_ref[...]) + sh_ref[...]
    h_hi, h_lo = _split2(h)
    h_ref[rows] = h_hi
    part = _dot(h_hi, wr_ref[...])
    logits = part + pltpu.roll(part, LANES - N_EXPERTS, axis=1) + _dot(h_lo, wr_ref[...])
    lane = lax.broadcasted_iota(jnp.int32, logits.shape, 1)
    logits = jnp.where(lane < N_EXPERTS, logits, -jnp.inf)
    ex = jnp.exp(logits - jnp.max(logits, axis=-1, keepdims=True))
    aff = ex / jnp.sum(ex, axis=-1, keepdims=True)
    aff_ref[rows] = aff
    a_hi = aff.astype(BF16).astype(F32)
    r1 = aff - a_hi
    a_lo = r1.astype(BF16).astype(F32)
    a_lo2 = r1 - a_lo
    packed = a_hi + pltpu.roll(a_lo, N_EXPERTS, axis=1) + pltpu.roll(a_lo2, 2 * N_EXPERTS, axis=1)
    asp_ref[rows] = packed.astype(BF16)


def _outproj_kernel(x_ref, *refs):
    nseq, tm = x_ref.shape[0], x_ref.shape[1]
    for s in range(nseq):
        for r0 in range(0, tm, SUB_ROWS):
            _outproj_rows((s, slice(r0, r0 + SUB_ROWS)), x_ref, *refs)


def _outproj_call(x, oa, ob, oc, modl, row0, row_step, lw, layer):
    nb, n, d = x.shape
    sb, tm = _seq_tiling(n, row_step)
    tok = lambda c: pl.BlockSpec((sb, tm, c), lambda b, i: (b, i, 0))
    const = lambda a: _layer_spec(a, layer)
    modspec = lambda j: pl.BlockSpec((None, None, None, 1, d),
                                     lambda b, i: (layer, row0 + b * sb * row_step, j, 0, 0))
    return pl.pallas_call(
        _outproj_kernel,
        grid=(nb // sb, n // tm),
        in_specs=[tok(d), tok(A_WIDTH), tok(B_WIDTH), tok(C_WIDTH), const(lw["w_out"]),
                  modspec(2), modspec(3), modspec(4), const(lw["norm2_g"]),
                  const(lw["w_router"])],
        out_specs=[tok(d), tok(d), tok(LANES), tok(LANES)],
        out_shape=[jax.ShapeDtypeStruct((nb, n, d), F32), jax.ShapeDtypeStruct((nb, n, d), BF16),
                   jax.ShapeDtypeStruct((nb, n, LANES), F32), jax.ShapeDtypeStruct((nb, n, LANES), BF16)],
        compiler_params=_cp(48), name="out_proj_router",
    )(x, oa, ob, oc, lw["w_out"], modl, modl, modl, lw["norm2_g"], lw["w_router"])


def _route_kernel(aff_ref, pos_ref, post_ref, offs_ref, *, cap):
    nb, n, _ = aff_ref.shape
    c = TOK_TILE
    nck = n // c
    aff = aff_ref[...]
    capf = jnp.float32(cap)

    def search(i, bits):
        cand = bits | jnp.left_shift(jnp.int32(1), jnp.int32(SEARCH_BITS - 1) - i)
        cnt = jnp.sum(jnp.where(aff >= lax.bitcast_convert_type(cand, F32), 1.0, 0.0),
                      axis=1, keepdims=True)
        return jnp.where(cnt >= capf, cand, bits)

    bits = lax.fori_loop(0, SEARCH_BITS, search, jnp.zeros((nb, 1, LANES), jnp.int32))
    thr_all = lax.bitcast_convert_type(bits, F32)
    n_gt = jnp.sum(jnp.where(aff > thr_all, 1.0, 0.0), axis=1, keepdims=True)
    n_tie_all = capf - n_gt
    ri = lax.broadcasted_iota(jnp.int32, (c, c), 0)
    ci = lax.broadcasted_iota(jnp.int32, (c, c), 1)
    tri = jnp.where(ri > ci, 1.0, 0.0).astype(BF16)
    lane_ok = lax.broadcasted_iota(jnp.int32, (c, LANES), 1) < N_EXPERTS
    offs_ref[...] = jnp.zeros(offs_ref.shape, jnp.int32)
    for s in range(nb):
        thr = thr_all[s]
        n_tie = n_tie_all[s]
        tie_seen = jnp.zeros((1, LANES), F32)
        sel_seen = jnp.zeros((1, LANES), F32)
        for k in range(nck):
            kk = aff_ref[s, k * c:(k + 1) * c, :]
            gt = kk > thr
            eq = kk == thr
            eqf = jnp.where(eq, 1.0, 0.0)
            tie_rank = _dot(tri, eqf.astype(BF16)) + tie_seen
            sel = (gt | (eq & (tie_rank < n_tie))) & lane_ok
            self_ = jnp.where(sel, 1.0, 0.0)
            slot = _dot(tri, self_.astype(BF16)) + sel_seen
            pos = jnp.where(sel, slot, -1.0)
            pos_ref[s, k * c:(k + 1) * c, :] = pos
            post_ref[s, :, k * c:(k + 1) * c] = pos.T[0:N_EXPERTS, :]
            offs_ref[s, k:k + 1, :] = sel_seen[:, 0:N_EXPERTS].astype(jnp.int32)
            tie_seen = tie_seen + jnp.sum(eqf, axis=0, keepdims=True)
            sel_seen = sel_seen + jnp.sum(self_, axis=0, keepdims=True)
        offs_ref[s, nck:nck + 1, :] = sel_seen[:, 0:N_EXPERTS].astype(jnp.int32)


def _offs_rows(n):
    return -(-(n // TOK_TILE + 1) // 8) * 8


def _route_call(aff, cap):
    nb, n, _ = aff.shape
    rows = _offs_rows(n)
    return pl.pallas_call(
        functools.partial(_route_kernel, cap=cap),
        grid=(1,),
        in_specs=[pl.BlockSpec((nb, n, LANES), lambda i: (0, 0, 0))],
        out_specs=[pl.BlockSpec((nb, n, LANES), lambda i: (0, 0, 0)),
                   pl.BlockSpec((nb, N_EXPERTS, n), lambda i: (0, 0, 0)),
                   pl.BlockSpec((nb, rows, N_EXPERTS), lambda i: (0, 0, 0))],
        out_shape=[jax.ShapeDtypeStruct((nb, n, LANES), F32),
                   jax.ShapeDtypeStruct((nb, N_EXPERTS, n), F32),
                   jax.ShapeDtypeStruct((nb, rows, N_EXPERTS), jnp.int32)],
        compiler_params=_cp(32), name="ec_route",
    )(aff, )


def _window(offs_ref, b, k, e, rows, win):
    off = offs_ref[(b * rows + k) * N_EXPERTS + e]
    nxt = offs_ref[(b * rows + k + 1) * N_EXPERTS + e]
    base = lax.shift_right_logical(off, SLOT_ALIGN.bit_length() - 1) * SLOT_ALIGN
    nwin = lax.shift_right_logical(nxt - base + (win - 1), win.bit_length() - 1)
    return base, nwin


def _slot_targets(ids, start, cap, win):
    cb = pl.multiple_of(jnp.minimum(start, cap - win), SLOT_ALIGN)
    slot = ids + cb.astype(F32)
    return jnp.where(slot >= start.astype(F32), slot, -2.0), cb


def _gather_kernel(offs_ref, post_ref, h_ref, asp_ref, xs_ref, gs_ref, *, cap, win, rows):
    b = pl.program_id(0)
    k = pl.program_id(1)

    @pl.when(k == 0)
    def _():
        xs_ref[...] = jnp.zeros(xs_ref.shape, xs_ref.dtype)
        gs_ref[...] = jnp.zeros(gs_ref.shape, gs_ref.dtype)

    sub = lax.broadcasted_iota(jnp.int32, (win, 1), 0).astype(F32)
    hk = h_ref[...]
    gk = asp_ref[...]

    def sel_rows(e, start):
        tgt, cb = _slot_targets(sub, start, cap, win)
        return jnp.where(post_ref[e:e + 1, :] == tgt, 1.0, 0.0).astype(BF16), cb

    for g0 in range(0, N_EXPERTS, GATHER_GROUP):
        parts, cbs = [], []
        for e in range(g0, g0 + GATHER_GROUP):
            base, _ = _window(offs_ref, b, k, e, rows, win)
            p, cb = sel_rows(e, base)
            parts.append(p)
            cbs.append(cb)
        p_grp = jnp.concatenate(parts, axis=0)
        r_grp = _dot(p_grp, hk).astype(BF16)
        g_grp = _dot(p_grp, gk)
        for j, cb in enumerate(cbs):
            xs_ref[g0 + j, pl.ds(cb, win), :] += r_grp[j * win:(j + 1) * win, :]
            gs_ref[g0 + j, pl.ds(cb, win), :] += g_grp[j * win:(j + 1) * win, :]

    for e in range(N_EXPERTS):
        base, nwin = _window(offs_ref, b, k, e, rows, win)

        def extra(w, carry, e=e, base=base):
            p, cbw = sel_rows(e, base + w * win)
            xs_ref[e, pl.ds(cbw, win), :] += _dot(p, hk).astype(BF16)
            gs_ref[e, pl.ds(cbw, win), :] += _dot(p, gk)
            return carry

        lax.fori_loop(1, nwin, extra, 0)


def _gather_call(offs, post, h, asp, cap, win):
    nb, n, d = h.shape
    c = TOK_TILE
    rows = _offs_rows(n)
    grid_spec = pltpu.PrefetchScalarGridSpec(
        num_scalar_prefetch=1, grid=(nb, n // c),
        in_specs=[pl.BlockSpec((None, N_EXPERTS, c), lambda b, k, o: (b, 0, k)),
                  pl.BlockSpec((None, c, d), lambda b, k, o: (b, k, 0)),
                  pl.BlockSpec((None, c, LANES), lambda b, k, o: (b, k, 0))],
        out_specs=[pl.BlockSpec((N_EXPERTS, cap, d), lambda b, k, o: (0, b, 0)),
                   pl.BlockSpec((N_EXPERTS, cap, LANES), lambda b, k, o: (0, b, 0))])
    return pl.pallas_call(
        functools.partial(_gather_kernel, cap=cap, win=win, rows=rows),
        grid_spec=grid_spec,
        out_shape=[jax.ShapeDtypeStruct((N_EXPERTS, nb * cap, d), BF16),
                   jax.ShapeDtypeStruct((N_EXPERTS, nb * cap, LANES), F32)],
        compiler_params=_cp(48), name="ec_gather",
    )(offs.reshape(-1), post, h, asp)


def _ffn_kernel(xa_ref, xb_ref, ga_ref, gb_ref, w1_ref, w3_ref, w2_ref, ya_ref, yb_ref,
                acca_ref, accb_ref):
    e = pl.program_id(0)
    f = pl.program_id(1)
    w1 = w1_ref[...].astype(BF16)
    w3 = w3_ref[...].astype(BF16)
    w2 = w2_ref[...].astype(BF16)

    @pl.when((e == 0) & (f == 0))
    def _():
        acca_ref[...] = jnp.zeros(acca_ref.shape, F32)
        accb_ref[...] = jnp.zeros(accb_ref.shape, F32)

    for x_ref, g_ref, y_ref, acc_ref in ((xa_ref, ga_ref, ya_ref, acca_ref),
                                         (xb_ref, gb_ref, yb_ref, accb_ref)):
        x = x_ref[...]
        hid = jax.nn.silu(_dot(x, w1)) * _dot(x, w3)
        acc = jnp.where(f > 0, acc_ref[...], 0.0) + _dot(hid.astype(BF16), w2)
        acc_ref[...] = acc
        g = g_ref[...]
        lane = lax.broadcasted_iota(jnp.int32, g.shape, 1)
        keep = ((lane % N_EXPERTS) == e) & (lane < 3 * N_EXPERTS)
        gate = jnp.sum(jnp.where(keep, g, 0.0), axis=-1, keepdims=True)
        y_ref[...] = (acc * gate).astype(BF16)


def _ffn_call(xa, xb, ga, gb, w1, w3, w2, layer):
    ne, ra, d = xa.shape
    rb = xb.shape[1]
    ff = w1.shape[3]
    tf = FF_TILE
    rowsp = lambda r, c: pl.BlockSpec((None, r, c), lambda e, f: (e, 0, 0))
    return pl.pallas_call(
        _ffn_kernel,
        grid=(ne, ff // tf),
        in_specs=[rowsp(ra, d), rowsp(rb, d), rowsp(ra, LANES), rowsp(rb, LANES),
                  pl.BlockSpec((None, None, d, tf), lambda e, f: (layer, e, 0, f)),
                  pl.BlockSpec((None, None, d, tf), lambda e, f: (layer, e, 0, f)),
                  pl.BlockSpec((None, None, tf, d), lambda e, f: (layer, e, f, 0))],
        out_specs=[rowsp(ra, d), rowsp(rb, d)],
        out_shape=[jax.ShapeDtypeStruct((ne, ra, d), BF16), jax.ShapeDtypeStruct((ne, rb, d), BF16)],
        scratch_shapes=[pltpu.VMEM((ra, d), F32), pltpu.VMEM((rb, d), F32)],
        compiler_params=_cp(56), name="expert_swiglu",
    )(xa, xb, ga, gb, w1, w3, w2)


def _combine_kernel(offs_ref, pos_ref, y_ref, x_ref, g2_ref, ex_ref, *rest, cap, win, rows, last):
    if last:
        fg_ref, o_ref, on_ref = rest
    else:
        (o_ref,) = rest
    b = pl.program_id(0)
    k = pl.program_id(1)
    per = MXU_DEPTH // win
    depth = per * win
    lane = lax.broadcasted_iota(jnp.int32, (1, depth), 1)
    lane_grp = lane // win
    lane_in = (lane % win).astype(F32)
    lane1 = lax.broadcasted_iota(jnp.int32, (1, win), 1).astype(F32)
    pos_wide = _dot(pos_ref[...].astype(BF16), ex_ref[...])
    tot = None
    for gi, g0 in enumerate(range(0, N_EXPERTS, per)):
        tgt = jnp.full((1, depth), -2.0, F32)
        ys = []
        for j in range(per):
            base, _ = _window(offs_ref, b, k, g0 + j, rows, win)
            tgt_e, cb = _slot_targets(lane_in, base, cap, win)
            tgt = jnp.where(lane_grp == j, tgt_e, tgt)
            ys.append(y_ref[g0 + j, pl.ds(cb, win), :])
        pt = jnp.where(pos_wide[:, gi * depth:(gi + 1) * depth] == tgt, 1.0, 0.0).astype(BF16)
        d = _dot(pt, jnp.concatenate(ys, axis=0))
        tot = d if tot is None else tot + d
    o_ref[...] = x_ref[...] + g2_ref[...] * tot

    for e in range(N_EXPERTS):
        base, nwin = _window(offs_ref, b, k, e, rows, win)

        def extra(w, carry, e=e, base=base):
            tgt_w, cbw = _slot_targets(lane1, base + w * win, cap, win)
            pt = jnp.where(pos_ref[:, e:e + 1] == tgt_w, 1.0, 0.0).astype(BF16)
            o_ref[...] += g2_ref[...] * _dot(pt, y_ref[e, pl.ds(cbw, win), :])
            return carry

        lax.fori_loop(1, nwin, extra, 0)
    if last:
        on_ref[...] = _rms(o_ref[...]) * fg_ref[...]


def _combine_call(offs, pos, y, x1, modl, row0, row_step, final_g, cap, win, layer):
    nb, n, d = x1.shape
    c = TOK_TILE
    rows = _offs_rows(n)
    last = final_g is not None
    assert cap <= 256
    spread = np.zeros((LANES, N_EXPERTS * win), np.float32)
    for e in range(N_EXPERTS):
        spread[e, e * win:(e + 1) * win] = 1.0
    spread = jnp.asarray(spread, BF16)
    tok = pl.BlockSpec((None, c, d), lambda b, k, o: (b, k, 0))
    in_specs = [pl.BlockSpec((None, c, LANES), lambda b, k, o: (b, k, 0)),
                pl.BlockSpec((N_EXPERTS, cap, d), lambda b, k, o: (0, b, 0)),
                tok,
                pl.BlockSpec((None, None, None, 1, d), lambda b, k, o: (layer, row0 + b * row_step, 5, 0, 0)),
                pl.BlockSpec(spread.shape, lambda b, k, o: (0, 0))]
    ins = [offs.reshape(-1), pos, y, x1, modl, spread]
    if last:
        in_specs.append(pl.BlockSpec((1, d), lambda b, k, o: (0, 0)))
        ins.append(final_g)
    nout = 2 if last else 1
    grid_spec = pltpu.PrefetchScalarGridSpec(
        num_scalar_prefetch=1, grid=(nb, n // c), in_specs=in_specs, out_specs=[tok] * nout)
    return pl.pallas_call(
        functools.partial(_combine_kernel, cap=cap, win=win, rows=rows, last=last),
        grid_spec=grid_spec,
        out_shape=[jax.ShapeDtypeStruct((nb, n, d), F32)] * nout,
        compiler_params=_cp(56), name="ec_combine_final" if last else "ec_combine",
    )(*ins)


def _prep_weights(w_in, w_s, b_s, w_pool, pool_scale, q_norm_g, w_qb, kv_norm_g, w_kvb, w_out,
                  w_router, norm1_g, norm2_g):
    depth, d, cols = w_in.shape
    row = lambda a: a.reshape(depth, 1, -1)
    w_in_p = jnp.pad(w_in, ((0, 0), (0, 0), (0, PROJ_PAD - cols))).astype(BF16)
    wq = w_qb.reshape(depth, Q_LORA, C_HEADS, QK_NOPE + QK_ROPE)
    wq = jnp.pad(wq, ((0, 0), (0, 0), (0, 0), (0, HEAD_PAD - QK_NOPE - QK_ROPE)))
    wq_t = wq.reshape(depth, Q_LORA, C_HEADS * HEAD_PAD).transpose(0, 2, 1).astype(BF16)
    wkv = w_kvb.reshape(depth, KV_LORA, C_HEADS, QK_NOPE + V_DIM)
    wkn = jnp.pad(wkv[..., :QK_NOPE], ((0, 0), (0, 0), (0, 0), (0, HEAD_PAD - QK_NOPE)))
    wkn = wkn.reshape(depth, KV_LORA, C_HEADS * HEAD_PAD).astype(BF16)
    wv_t = wkv[..., QK_NOPE:].reshape(depth, KV_LORA, C_WIDTH).transpose(0, 2, 1).astype(BF16)
    assert A_HEADS == len(POOL_WINDOWS) and A_HEAD_DIM == POOL_GROUP
    ws_cat = w_s.transpose(0, 2, 1, 3).reshape(depth, CHUNK, A_HEADS * CHUNK).astype(BF16)
    b_rep = jnp.repeat(b_s.transpose(0, 2, 1), A_HEAD_DIM, axis=2)
    eye = jnp.eye(len(POOL_WINDOWS), dtype=F32)
    wp_bd = jnp.einsum("gh,lgcd->lgchd", eye, w_pool).reshape(depth, B_WIDTH, B_WIDTH).astype(BF16)
    wr_hi = w_router.astype(BF16)
    wr_lo = (w_router - wr_hi.astype(F32)).astype(BF16)
    wr = jnp.pad(jnp.concatenate([wr_hi, wr_lo], axis=2), ((0, 0), (0, 0), (0, LANES - 2 * N_EXPERTS)))
    return dict(
        norm1_g=row(norm1_g), norm2_g=row(norm2_g), w_in=w_in_p, q_norm_g=row(q_norm_g),
        kv_norm_g=row(kv_norm_g), wq_t=wq_t, wkn=wkn, wv_t=wv_t, w_s=ws_cat, b_s=b_rep, w_pool=wp_bd,
        pool_scale=row(pool_scale), w_out=w_out.astype(BF16), w_router=wr)


def _rope_tables(n):
    rows = n // GRID_W
    row = jnp.repeat(jnp.arange(rows), GRID_W).astype(F32)
    col = jnp.broadcast_to(jnp.arange(GRID_W), (rows, GRID_W)).reshape(-1).astype(F32)
    inv = 1.0 / (ROPE_THETA ** (jnp.arange(0, ROPE_AXIS, 2, dtype=F32) / ROPE_AXIS))
    ang_r = row[:, None] * inv[None, :]
    ang_c = col[:, None] * inv[None, :]
    cos_r, sin_r, cos_c, sin_c = jnp.cos(ang_r), jnp.sin(ang_r), jnp.cos(ang_c), jnp.sin(ang_c)
    pad = jnp.zeros((n, LANES - QK_ROPE), F32)
    cos_k = jnp.concatenate([cos_r, cos_r, cos_c, cos_c, pad], axis=1)
    sin_k = jnp.concatenate([-sin_r, sin_r, -sin_c, sin_c, pad], axis=1)
    return cos_k, sin_k, cos_r.T, sin_r.T, cos_c.T, sin_c.T


def kernel(x_prompt, x_sample, cache_ckv, cache_krope, c, c_ctx, w_ada, b_ada, norm1_g, norm2_g, w_in, w_s, b_s, w_pool, pool_scale, q_norm_g, w_qb, kv_norm_g, w_kvb, w_out, w_router, w_e1, w_e3, w_e2, final_norm_g):
    depth, d, _ = w_ada.shape
    nb_c, n_c, _ = x_prompt.shape
    nb_s, n_s, _ = x_sample.shape
    assert 1 + nb_s <= 8
    cond8 = jnp.concatenate([c_ctx[None, :], c, jnp.zeros((8 - 1 - nb_s, d), F32)], axis=0)
    mod = _mod_call(cond8, w_ada, b_ada).reshape(depth, 8, 6, 1, d)
    lw = _prep_weights(w_in, w_s, b_s, w_pool, pool_scale, q_norm_g, w_qb, kv_norm_g, w_kvb, w_out,
                       w_router, norm1_g, norm2_g)
    k_cache, vt_cache = _kvexp_call(cache_ckv, cache_krope, lw["wkn"], lw["wv_t"])
    tabs = _rope_tables(n_s)
    band = _pool_bands()
    final_g = final_norm_g.reshape(1, d)
    cap_c = EC_CAPACITY * n_c // N_EXPERTS
    cap_s = EC_CAPACITY * n_s // N_EXPERTS
    win_c = min(cap_c, 64)
    win_s = min(cap_s, 64)

    xc, xs = x_prompt, x_sample
    yc = ys = None
    ckvs, kropes = [], []
    for l in range(depth):
        uc, vac, pinc, qtc, khc, vtc, ckv_c, kr_c = _proj_call(xc, mod, 0, 0, lw, None, l)
        us, vas, pins, qts, khs, vts, _, _ = _proj_call(xs, mod, 1, 1, lw, tabs, l)
        ckvs.append(ckv_c)
        kropes.append(kr_c)
        oac, obc = _mix_call(uc, vac, pinc, lw, band, l)
        oas, obs = _mix_call(us, vas, pins, lw, band, l)
        occ = _attn_call(qtc, [(khc, vtc, None)])
        ocs = _attn_call(qts, [(k_cache, vt_cache, l), (khs, vts, None)])
        x1c, hc, affc, aspc = _outproj_call(xc, oac, obc, occ, mod, 0, 0, lw, l)
        x1s, hs, affs, asps = _outproj_call(xs, oas, obs, ocs, mod, 1, 1, lw, l)
        posc, postc, offc = _route_call(affc, cap_c)
        poss, posts, offs = _route_call(affs, cap_s)
        xgc, ggc = _gather_call(offc, postc, hc, aspc, cap_c, win_c)
        xgs, ggs = _gather_call(offs, posts, hs, asps, cap_s, win_s)
        ygs, ygc = _ffn_call(xgs, xgc, ggs, ggc, w_e1, w_e3, w_e2, l)
        fg = final_g if l == depth - 1 else None
        outc = _combine_call(offc, posc, ygc, x1c, mod, 0, 0, fg, cap_c, win_c, l)
        outs = _combine_call(offs, poss, ygs, x1s, mod, 1, 1, fg, cap_s, win_s, l)
        xc, xs = outc[0], outs[0]
    yc, ys = outc[1], outs[1]
    new_ckv = jnp.stack(ckvs, axis=1)
    new_krope = jnp.stack(kropes, axis=1)
    return (yc, ys, new_ckv, new_krope)
```

```python
import functools
import math

import numpy as np
import jax
import jax.numpy as jnp
from jax import lax
from jax.experimental import pallas as pl
from jax.experimental.pallas import tpu as pltpu

F32 = jnp.float32
BF16 = jnp.bfloat16

GRID_W = 64
A_HEADS = 4
A_HEAD_DIM = 64
A_WIDTH = A_HEADS * A_HEAD_DIM
CHUNK = 128
POOL_WINDOWS = (2, 4, 8, 16)
POOL_GROUP = 64
B_WIDTH = len(POOL_WINDOWS) * POOL_GROUP
C_HEADS = 8
QK_NOPE = 64
QK_ROPE = 32
V_DIM = 64
Q_LORA = 384
KV_LORA = 256
C_WIDTH = C_HEADS * V_DIM
ROPE_AXIS = QK_ROPE // 2
ROPE_THETA = 10000.0
ATTN_SCALE = (QK_NOPE + QK_ROPE) ** -0.5
N_EXPERTS = 16
EC_CAPACITY = 2
NORM_EPS = 1e-6

LANES = 128
HEAD_PAD = 128
PROJ_PAD = 1536
OFF_P = 2 * A_WIDTH
OFF_CQ = OFF_P + B_WIDTH
OFF_CKV = OFF_CQ + Q_LORA
OFF_KR = OFF_CKV + KV_LORA
POOL_HALO = 8
TOK_TILE = 256
ROW_TILE = 1024
EC_ROWS = 512
SUB_ROWS = 256
V_ROWS = V_DIM + 16
KEY_CHUNK = 512
ATTN_Q_BLOCKS = 2
Q_PRESCALE = ATTN_SCALE * math.log2(math.e)
FF_TILE = 512
MOD_TILE = 1536
SEARCH_BITS = 31
SLOT_ALIGN = 16
GATHER_GROUP = 4
MXU_DEPTH = 256


def _cp(vmem_mb):
    return pltpu.CompilerParams(vmem_limit_bytes=vmem_mb * 1024 * 1024)


def _layer_spec(a, layer):
    return pl.BlockSpec((None,) + a.shape[1:], lambda *_: (layer,) + (0,) * (a.ndim - 1))


def _dot(a, b):
    return jnp.dot(a, b, preferred_element_type=F32)


def _dot_nt(a, b):
    return lax.dot_general(a, b, (((1,), (1,)), ((), ())), preferred_element_type=F32)


def _split2(x):
    hi = x.astype(BF16)
    lo = (x - hi.astype(F32)).astype(BF16)
    return hi, lo


def _rms(x):
    return x * lax.rsqrt(jnp.mean(x * x, axis=-1, keepdims=True) + NORM_EPS)


def _load_rows(ref):
    return jnp.concatenate([ref[s] for s in range(ref.shape[0])], axis=0)


def _store_rows(ref, val):
    tm = ref.shape[1]
    for s in range(ref.shape[0]):
        ref[s] = val[s * tm:(s + 1) * tm, :]


def _with_ones_row(vt):
    rows = lax.broadcasted_iota(jnp.int32, (V_ROWS - V_DIM, vt.shape[1]), 0)
    return jnp.concatenate([vt, jnp.where(rows == 0, 1.0, 0.0).astype(BF16)], axis=0)


def _mod_kernel(c_ref, w_ref, b_ref, o_ref):
    a = jax.nn.silu(c_ref[...])
    a_hi, a_lo = _split2(a)
    w_hi, w_lo = _split2(w_ref[...])
    o_ref[...] = _dot(a_hi, w_hi) + _dot(a_lo, w_hi) + _dot(a_hi, w_lo) + b_ref[...]


def _mod_call(cond8, w_ada, b_ada):
    depth, d, n = w_ada.shape
    return pl.pallas_call(
        _mod_kernel,
        grid=(depth, n // MOD_TILE),
        in_specs=[pl.BlockSpec((8, d), lambda l, j: (0, 0)),
                  pl.BlockSpec((None, d, MOD_TILE), lambda l, j: (l, 0, j)),
                  pl.BlockSpec((None, 1, MOD_TILE), lambda l, j: (l, 0, j))],
        out_specs=pl.BlockSpec((None, 8, MOD_TILE), lambda l, j: (l, 0, j)),
        out_shape=jax.ShapeDtypeStruct((depth, 8, n), F32),
        compiler_params=_cp(40),
        name="adaln_mod",
    )(cond8, w_ada, b_ada.reshape(depth, 1, n))


def _proj_kernel(*refs, rope):
    if rope:
        (x_ref, sh_ref, sc_ref, g_ref, win_ref, qg_ref, kvg_ref, wq_ref, wkn_ref, wvt_ref,
         ck_ref, sk_ref, cr_ref, sr_ref, cc_ref, scl_ref,
         u_ref, va_ref, pin_ref, qt_ref, kh_ref, vt_ref, ckv_ref, kr_ref) = refs
    else:
        (x_ref, sh_ref, sc_ref, g_ref, win_ref, qg_ref, kvg_ref, wq_ref, wkn_ref, wvt_ref,
         u_ref, va_ref, pin_ref, qt_ref, kh_ref, vt_ref, ckv_ref, kr_ref) = refs
    nseq, tm = x_ref.shape[0], x_ref.shape[1]
    x = _load_rows(x_ref)
    h = _rms(x) * g_ref[...] * (1.0 + sc_ref[...]) + sh_ref[...]
    proj = _dot(h.astype(BF16), win_ref[...])
    _store_rows(u_ref, jax.nn.gelu(proj[:, 0:A_WIDTH]))
    _store_rows(va_ref, jax.nn.gelu(proj[:, A_WIDTH:OFF_P]).astype(BF16))
    _store_rows(pin_ref, proj[:, OFF_P:OFF_CQ])
    cq = _rms(proj[:, OFF_CQ:OFF_CKV]) * qg_ref[...]
    ckv = _rms(proj[:, OFF_CKV:OFF_KR]) * kvg_ref[...]
    _store_rows(ckv_ref, ckv)
    kr = proj[:, OFF_KR:OFF_KR + LANES]
    _store_rows(kr_ref, kr[:, 0:QK_ROPE])

    qt = _dot_nt(wq_ref[...], cq.astype(BF16))
    ckv_b = ckv.astype(BF16)
    kn = _dot(ckv_b, wkn_ref[...])
    vt = _dot_nt(wvt_ref[...], ckv_b)

    if rope:
        lane = lax.broadcasted_iota(jnp.int32, kr.shape, 1)
        half = ROPE_AXIS // 2
        swapped = jnp.where((lane & half) == 0,
                            pltpu.roll(kr, LANES - half, axis=1), pltpu.roll(kr, half, axis=1))
        kr = kr * ck_ref[...] + swapped * sk_ref[...]
        cr, sr, cc, scl = cr_ref[...], sr_ref[...], cc_ref[...], scl_ref[...]
    kr_shift = pltpu.roll(kr, QK_NOPE, axis=1)

    for hd in range(C_HEADS):
        r0 = hd * HEAD_PAD
        q_h = qt[r0:r0 + HEAD_PAD, :]
        if rope:
            b0 = QK_NOPE
            a, b = q_h[b0:b0 + 8, :], q_h[b0 + 8:b0 + 16, :]
            c, d = q_h[b0 + 16:b0 + 24, :], q_h[b0 + 24:b0 + 32, :]
            q_h = jnp.concatenate(
                [q_h[0:b0, :], a * cr - b * sr, a * sr + b * cr, c * cc - d * scl, c * scl + d * cc,
                 q_h[b0 + 32:, :]], axis=0)
        q_h = (q_h * Q_PRESCALE).astype(BF16)
        k_h = (kn[:, r0:r0 + HEAD_PAD] + kr_shift).astype(BF16)
        v_h = _with_ones_row(vt[hd * V_DIM:(hd + 1) * V_DIM, :].astype(BF16))
        for s in range(nseq):
            for j in range(tm // TOK_TILE):
                c0 = s * tm + j * TOK_TILE
                qt_ref[s, j, hd] = q_h[:, c0:c0 + TOK_TILE]
            kh_ref[s, hd] = k_h[s * tm:(s + 1) * tm, :]
            vt_ref[s, hd] = v_h[:, s * tm:(s + 1) * tm]


def _seq_tiling(n, row_step, rows=None):
    rows = ROW_TILE if rows is None else rows
    tm = min(n, rows)
    return (rows // tm if row_step == 0 else 1), tm


def _proj_call(x, modl, row0, row_step, lw, tabs, layer):
    nb, n, d = x.shape
    sb, tm = _seq_tiling(n, row_step)
    rope = tabs is not None
    assert not rope or sb == 1
    tok = lambda c: pl.BlockSpec((sb, tm, c), lambda b, i: (b, i, 0))
    const = lambda a: _layer_spec(a, layer)
    modspec = lambda j: pl.BlockSpec((None, None, None, 1, d),
                                     lambda b, i: (layer, row0 + b * sb * row_step, j, 0, 0))
    ins = [x, modl, modl, lw["norm1_g"], lw["w_in"], lw["q_norm_g"], lw["kv_norm_g"],
           lw["wq_t"], lw["wkn"], lw["wv_t"]]
    in_specs = [tok(d), modspec(0), modspec(1), const(lw["norm1_g"]), const(lw["w_in"]),
                const(lw["q_norm_g"]), const(lw["kv_norm_g"]), const(lw["wq_t"]), const(lw["wkn"]),
                const(lw["wv_t"])]
    if rope:
        cos_k, sin_k, cos_r, sin_r, cos_c, sin_c = tabs
        ins += [cos_k, sin_k, cos_r, sin_r, cos_c, sin_c]
        in_specs += [pl.BlockSpec((tm, LANES), lambda b, i: (i, 0))] * 2
        in_specs += [pl.BlockSpec((8, tm), lambda b, i: (0, i))] * 4
    out_shape = [jax.ShapeDtypeStruct((nb, n, A_WIDTH), F32),
                 jax.ShapeDtypeStruct((nb, n, A_WIDTH), BF16),
                 jax.ShapeDtypeStruct((nb, n, B_WIDTH), F32),
                 jax.ShapeDtypeStruct((nb, n // TOK_TILE, C_HEADS, HEAD_PAD, TOK_TILE), BF16),
                 jax.ShapeDtypeStruct((nb, C_HEADS, n, HEAD_PAD), BF16),
                 jax.ShapeDtypeStruct((nb, C_HEADS, V_ROWS,n), BF16),
                 jax.ShapeDtypeStruct((nb, n, KV_LORA), F32),
                 jax.ShapeDtypeStruct((nb, n, QK_ROPE), F32)]
    out_specs = [tok(A_WIDTH), tok(A_WIDTH), tok(B_WIDTH),
                 pl.BlockSpec((sb, tm // TOK_TILE, C_HEADS, HEAD_PAD, TOK_TILE), lambda b, i: (b, i, 0, 0, 0)),
                 pl.BlockSpec((sb, C_HEADS, tm, HEAD_PAD), lambda b, i: (b, 0, i, 0)),
                 pl.BlockSpec((sb, C_HEADS, V_ROWS, tm), lambda b, i: (b, 0, 0, i)),
                 tok(KV_LORA), tok(QK_ROPE)]
    return pl.pallas_call(
        functools.partial(_proj_kernel, rope=rope),
        grid=(nb // sb, n // tm), in_specs=in_specs, out_specs=out_specs, out_shape=out_shape,
        compiler_params=_cp(48), name="in_proj_rope" if rope else "in_proj",
    )(*ins)


def _kvexp_kernel(ckv_ref, kr_ref, wkn_ref, wvt_ref, place_ref, kh_ref, vt_ref):
    ckv_b = ckv_ref[...].astype(BF16)
    kn = _dot(ckv_b, wkn_ref[...])
    vt = _dot_nt(wvt_ref[...], ckv_b)
    kr_placed = _dot(kr_ref[...].astype(BF16), place_ref[...])
    for hd in range(C_HEADS):
        r0 = hd * HEAD_PAD
        kh_ref[hd] = (kn[:, r0:r0 + HEAD_PAD] + kr_placed).astype(BF16)
        vt_ref[hd] = _with_ones_row(vt[hd * V_DIM:(hd + 1) * V_DIM, :].astype(BF16))


def _kvexp_call(cache_ckv, cache_krope, wkn, wv_t):
    nb, depth, m, r = cache_ckv.shape
    place = np.zeros((QK_ROPE, HEAD_PAD), np.float32)
    place[np.arange(QK_ROPE), QK_NOPE + np.arange(QK_ROPE)] = 1.0
    place = jnp.asarray(place, BF16)
    return pl.pallas_call(
        _kvexp_kernel,
        grid=(depth, nb),
        in_specs=[pl.BlockSpec((None, None, m, r), lambda l, b: (b, l, 0, 0)),
                  pl.BlockSpec((None, None, m, QK_ROPE), lambda l, b: (b, l, 0, 0)),
                  pl.BlockSpec((None,) + wkn.shape[1:], lambda l, b: (l, 0, 0)),
                  pl.BlockSpec((None,) + wv_t.shape[1:], lambda l, b: (l, 0, 0)),
                  pl.BlockSpec(place.shape, lambda l, b: (0, 0))],
        out_specs=[pl.BlockSpec((None, None, C_HEADS, m, HEAD_PAD), lambda l, b: (l, b, 0, 0, 0)),
                   pl.BlockSpec((None, None, C_HEADS, V_ROWS,m), lambda l, b: (l, b, 0, 0, 0))],
        out_shape=[jax.ShapeDtypeStruct((depth, nb, C_HEADS, m, HEAD_PAD), BF16),
                   jax.ShapeDtypeStruct((depth, nb, C_HEADS, V_ROWS,m), BF16)],
        compiler_params=_cp(32), name="cache_kv_expand",
    )(cache_ckv, cache_krope, wkn, wv_t, place)


def _mix_kernel(u_ref, va_ref, pin_ref, ws_ref, bs_ref, band_ref, wp_ref, ps_ref,
                oa_ref, ob_ref, pad_ref):
    n = u_ref.shape[0]
    width = u_ref.shape[1]
    zero_rows = jnp.zeros((POOL_HALO, width), F32)
    pad_ref[0:POOL_HALO, :] = zero_rows
    pad_ref[n + POOL_HALO:n + 2 * POOL_HALO, :] = zero_rows
    pad_ref[POOL_HALO:n + POOL_HALO, :] = pin_ref[...]
    lane_grp = lax.broadcasted_iota(jnp.int32, (CHUNK, width), 1) // POOL_GROUP
    lane_grp_w = lax.broadcasted_iota(jnp.int32, (CHUNK + 2 * POOL_HALO, width), 1) // POOL_GROUP
    row = lax.broadcasted_iota(jnp.int32, (CHUNK, width), 0)

    def group_stack(x):
        grp = lane_grp if x.shape[0] == CHUNK else lane_grp_w
        return [jnp.where(grp == g, x, 0.0) for g in range(len(POOL_WINDOWS))]

    def body(c, carry):
        r0 = pl.multiple_of(c * CHUNK, CHUNK)
        v = va_ref[pl.ds(r0, CHUNK), :].astype(F32)
        v_stack = jnp.concatenate([x.astype(BF16) for x in group_stack(v)], axis=0)
        s = _dot(ws_ref[...], v_stack) + bs_ref[...]
        oa_ref[pl.ds(r0, CHUNK), :] = (u_ref[pl.ds(r0, CHUNK), :] * s).astype(BF16)
        win = pad_ref[pl.ds(r0, CHUNK + 2 * POOL_HALO), :]
        splits = [_split2(x) for x in group_stack(win)]
        tot = (_dot(band_ref[...], jnp.concatenate([hi for hi, _ in splits], axis=0))
               + _dot(band_ref[...], jnp.concatenate([lo for _, lo in splits], axis=0)))
        t = r0 + row
        cnt = jnp.ones((CHUNK, width), jnp.int32)
        for g, w in enumerate(POOL_WINDOWS):
            left = w // 2
            right = w - 1 - left
            cg = jnp.clip(t + right + 1, 0, n) - jnp.clip(t - left, 0, n)
            cnt = jnp.where(lane_grp == g, cg, cnt)
        p = win[POOL_HALO:POOL_HALO + CHUNK, :]
        diff = (tot / cnt.astype(F32) - p).astype(BF16)
        ob_ref[pl.ds(r0, CHUNK), :] = (_dot(diff, wp_ref[...]) * ps_ref[...]).astype(BF16)
        return carry

    lax.fori_loop(0, n // CHUNK, body, 0)


def _pool_bands():
    rows = CHUNK + 2 * POOL_HALO
    band = np.zeros((len(POOL_WINDOWS), CHUNK, rows), np.float32)
    i = np.arange(CHUNK)[:, None]
    j = np.arange(rows)[None, :]
    for g, w in enumerate(POOL_WINDOWS):
        left = w // 2
        right = w - 1 - left
        band[g] = ((j >= i + POOL_HALO - left) & (j <= i + POOL_HALO + right)).astype(np.float32)
    return jnp.asarray(band.transpose(1, 0, 2).reshape(CHUNK, -1), BF16)


def _mix_call(u, va, pin, lw, band, layer):
    nb, n, w = u.shape
    seq = lambda: pl.BlockSpec((None, n, w), lambda b: (b, 0, 0))
    const = lambda a: _layer_spec(a, layer)
    return pl.pallas_call(
        _mix_kernel,
        grid=(nb,),
        in_specs=[seq(), seq(), seq(), const(lw["w_s"]), const(lw["b_s"]),
                  pl.BlockSpec(band.shape, lambda b: (0, 0)),
                  const(lw["w_pool"]), const(lw["pool_scale"])],
        out_specs=[seq(), seq()],
        out_shape=[jax.ShapeDtypeStruct((nb, n, w), BF16)] * 2,
        scratch_shapes=[pltpu.VMEM((n + 2 * POOL_HALO, w), F32)],
        compiler_params=_cp(48), name="mixers_ab",
    )(u, va, pin, lw["w_s"], lw["b_s"], band, lw["w_pool"], lw["pool_scale"])


def _attn_kernel(*refs, seg_lens):
    nseg = len(seg_lens)
    qt_ref = refs[0]
    k_refs = refs[1:1 + nseg]
    v_refs = refs[1 + nseg:1 + 2 * nseg]
    o_ref, ot_ref, sa_ref, sb_ref = refs[1 + 2 * nseg:]
    nqb, _, _, tq = qt_ref.shape
    nitem = nqb * C_HEADS
    head_bits = C_HEADS.bit_length() - 1
    assert C_HEADS == 1 << head_bits

    def scores_into(it, s_ref):
        qb, hd = it >> head_bits, it & (C_HEADS - 1)
        q = qt_ref[qb, hd]
        mx = None
        r0 = 0
        for k_ref, m in zip(k_refs, seg_lens):
            for c0 in range(0, m, KEY_CHUNK):
                kc = min(KEY_CHUNK, m - c0)
                s = _dot(k_ref[hd, c0:c0 + kc, :], q)
                s_ref[r0:r0 + kc, :] = s
                cm = jnp.max(s, axis=0, keepdims=True)
                mx = cm if mx is None else jnp.maximum(mx, cm)
                r0 += kc
        return mx

    def finish(it, s_ref, mx):
        qb, hd = it >> head_bits, it & (C_HEADS - 1)
        acc = jnp.zeros((V_ROWS, tq), F32)
        r0 = 0
        for v_ref, m in zip(v_refs, seg_lens):
            p = jnp.exp2(s_ref[r0:r0 + m, :] - mx).astype(BF16)
            acc = acc + _dot(v_ref[hd], p)
            r0 += m
        rows = pl.ds(pl.multiple_of(hd * V_DIM, V_DIM), V_DIM)
        ot_ref[qb, rows, :] = acc[0:V_DIM, :] / acc[V_DIM:V_DIM + 1, :]

    def item_pair(j, mx_a):
        i1 = 2 * j + 1
        mx_b = scores_into(i1, sb_ref)
        finish(i1 - 1, sa_ref, mx_a)
        mx_next = scores_into(i1 + 1, sa_ref)
        finish(i1, sb_ref, mx_b)
        return mx_next

    mx_a = lax.fori_loop(0, (nitem - 2) // 2, item_pair, scores_into(0, sa_ref))
    mx_b = scores_into(nitem - 1, sb_ref)
    finish(nitem - 2, sa_ref, mx_a)
    finish(nitem - 1, sb_ref, mx_b)
    for qb in range(nqb):
        o_ref[qb * tq:(qb + 1) * tq, :] = ot_ref[qb].T.astype(BF16)


def _attn_call(qt, segs):
    nb, nblk, _, _, tq = qt.shape
    nqb = min(nblk, ATTN_Q_BLOCKS)
    seg_lens = tuple(k.shape[-2] for k, _, _ in segs)

    def seg_spec(a, layer):
        if layer is None:
            return pl.BlockSpec((None,) + a.shape[1:], lambda b, i: (b, 0, 0, 0))
        return pl.BlockSpec((None, None) + a.shape[2:], lambda b, i: (layer, b, 0, 0, 0))

    in_specs = [pl.BlockSpec((None, nqb, C_HEADS, HEAD_PAD, tq), lambda b, i: (b, i, 0, 0, 0))]
    in_specs += [seg_spec(k, layer) for k, _, layer in segs]
    in_specs += [seg_spec(v, layer) for _, v, layer in segs]
    return pl.pallas_call(
        functools.partial(_attn_kernel, seg_lens=seg_lens),
        grid=(nb, nblk // nqb), in_specs=in_specs,
        out_specs=pl.BlockSpec((None, nqb * tq, C_WIDTH), lambda b, i: (b, i, 0)),
        out_shape=jax.ShapeDtypeStruct((nb, nblk * tq, C_WIDTH), BF16),
        scratch_shapes=[pltpu.VMEM((nqb, C_WIDTH, tq), F32)] + [pltpu.VMEM((sum(seg_lens), tq), F32)] * 2,
        compiler_params=_cp(48), name="latent_attention",
    )(qt, *[k for k, _, _ in segs], *[v for _, v, _ in segs])


def _outproj_rows(rows, x_ref, oa_ref, ob_ref, oc_ref, wo_ref, g1_ref, sh_ref, sc_ref, ng_ref,
                  wr_ref, x1_ref, h_ref, aff_ref, asp_ref):
    mix = (_dot(oa_ref[rows], wo_ref[0:A_WIDTH, :])
           + _dot(ob_ref[rows], wo_ref[A_WIDTH:A_WIDTH + B_WIDTH, :])
           + _dot(oc_ref[rows], wo_ref[A_WIDTH + B_WIDTH:, :]))
    x1 = x_ref[rows] + g1_ref[...] * mix
    x1_ref[rows] = x1
    h = _rms(x1) * ng_ref[...] * (1.0 + sc_ref[...]) + sh_ref[...]
    h_hi, h_lo = _split2(h)
    h_ref[rows] = h_hi
    part = _dot(h_hi, wr_ref[...])
    logits = part + pltpu.roll(part, LANES - N_EXPERTS, axis=1) + _dot(h_lo, wr_ref[...])
    lane = lax.broadcasted_iota(jnp.int32, logits.shape, 1)
    logits = jnp.where(lane < N_EXPERTS, logits, -jnp.inf)
    ex = jnp.exp(logits - jnp.max(logits, axis=-1, keepdims=True))
    aff = ex / jnp.sum(ex, axis=-1, keepdims=True)
    aff_ref[rows] = aff
    a_hi = aff.astype(BF16).astype(F32)
    r1 = aff - a_hi
    a_lo = r1.astype(BF16).astype(F32)
    a_lo2 = r1 - a_lo
    packed = a_hi + pltpu.roll(a_lo, N_EXPERTS, axis=1) + pltpu.roll(a_lo2, 2 * N_EXPERTS, axis=1)
    asp_ref[rows] = packed.astype(BF16)


def _outproj_kernel(x_ref, *refs):
    nseq, tm = x_ref.shape[0], x_ref.shape[1]
    for s in range(nseq):
        for r0 in range(0, tm, SUB_ROWS):
            _outproj_rows((s, slice(r0, r0 + SUB_ROWS)), x_ref, *refs)


def _outproj_call(x, oa, ob, oc, modl, row0, row_step, lw, layer):
    nb, n, d = x.shape
    sb, tm = _seq_tiling(n, row_step)
    tok = lambda c: pl.BlockSpec((sb, tm, c), lambda b, i: (b, i, 0))
    const = lambda a: _layer_spec(a, layer)
    modspec = lambda j: pl.BlockSpec((None, None, None, 1, d),
                                     lambda b, i: (layer, row0 + b * sb * row_step, j, 0, 0))
    return pl.pallas_call(
        _outproj_kernel,
        grid=(nb // sb, n // tm),
        in_specs=[tok(d), tok(A_WIDTH), tok(B_WIDTH), tok(C_WIDTH), const(lw["w_out"]),
                  modspec(2), modspec(3), modspec(4), const(lw["norm2_g"]),
                  const(lw["w_router"])],
        out_specs=[tok(d), tok(d), tok(LANES), tok(LANES)],
        out_shape=[jax.ShapeDtypeStruct((nb, n, d), F32), jax.ShapeDtypeStruct((nb, n, d), BF16),
                   jax.ShapeDtypeStruct((nb, n, LANES), F32), jax.ShapeDtypeStruct((nb, n, LANES), BF16)],
        compiler_params=_cp(48), name="out_proj_router",
    )(x, oa, ob, oc, lw["w_out"], modl, modl, modl, lw["norm2_g"], lw["w_router"])


def _route_kernel(aff_ref, pos_ref, post_ref, offs_ref, *, cap):
    nb, n, _ = aff_ref.shape
    c = TOK_TILE
    nck = n // c
    aff = aff_ref[...]
    capf = jnp.float32(cap)

    def search(i, bits):
        cand = bits | jnp.left_shift(jnp.int32(1), jnp.int32(SEARCH_BITS - 1) - i)
        cnt = jnp.sum(jnp.where(aff >= lax.bitcast_convert_type(cand, F32), 1.0, 0.0),
                      axis=1, keepdims=True)
        return jnp.where(cnt >= capf, cand, bits)

    bits = lax.fori_loop(0, SEARCH_BITS, search, jnp.zeros((nb, 1, LANES), jnp.int32))
    thr_all = lax.bitcast_convert_type(bits, F32)
    n_gt = jnp.sum(jnp.where(aff > thr_all, 1.0, 0.0), axis=1, keepdims=True)
    n_tie_all = capf - n_gt
    ri = lax.broadcasted_iota(jnp.int32, (c, c), 0)
    ci = lax.broadcasted_iota(jnp.int32, (c, c), 1)
    tri = jnp.where(ri > ci, 1.0, 0.0).astype(BF16)
    lane_ok = lax.broadcasted_iota(jnp.int32, (c, LANES), 1) < N_EXPERTS
    offs_ref[...] = jnp.zeros(offs_ref.shape, jnp.int32)
    for s in range(nb):
        thr = thr_all[s]
        n_tie = n_tie_all[s]
        tie_seen = jnp.zeros((1, LANES), F32)
        sel_seen = jnp.zeros((1, LANES), F32)
        for k in range(nck):
            kk = aff_ref[s, k * c:(k + 1) * c, :]
            gt = kk > thr
            eq = kk == thr
            eqf = jnp.where(eq, 1.0, 0.0)
            tie_rank = _dot(tri, eqf.astype(BF16)) + tie_seen
            sel = (gt | (eq & (tie_rank < n_tie))) & lane_ok
            self_ = jnp.where(sel, 1.0, 0.0)
            slot = _dot(tri, self_.astype(BF16)) + sel_seen
            pos = jnp.where(sel, slot, -1.0)
            pos_ref[s, k * c:(k + 1) * c, :] = pos
            post_ref[s, :, k * c:(k + 1) * c] = pos.T[0:N_EXPERTS, :]
            offs_ref[s, k:k + 1, :] = sel_seen[:, 0:N_EXPERTS].astype(jnp.int32)
            tie_seen = tie_seen + jnp.sum(eqf, axis=0, keepdims=True)
            sel_seen = sel_seen + jnp.sum(self_, axis=0, keepdims=True)
        offs_ref[s, nck:nck + 1, :] = sel_seen[:, 0:N_EXPERTS].astype(jnp.int32)


def _offs_rows(n):
    return -(-(n // TOK_TILE + 1) // 8) * 8


def _route_call(aff, cap):
    nb, n, _ = aff.shape
    rows = _offs_rows(n)
    return pl.pallas_call(
        functools.partial(_route_kernel, cap=cap),
        grid=(1,),
        in_specs=[pl.BlockSpec((nb, n, LANES), lambda i: (0, 0, 0))],
        out_specs=[pl.BlockSpec((nb, n, LANES), lambda i: (0, 0, 0)),
                   pl.BlockSpec((nb, N_EXPERTS, n), lambda i: (0, 0, 0)),
                   pl.BlockSpec((nb, rows, N_EXPERTS), lambda i: (0, 0, 0))],
        out_shape=[jax.ShapeDtypeStruct((nb, n, LANES), F32),
                   jax.ShapeDtypeStruct((nb, N_EXPERTS, n), F32),
                   jax.ShapeDtypeStruct((nb, rows, N_EXPERTS), jnp.int32)],
        compiler_params=_cp(32), name="ec_route",
    )(aff, )


def _window(offs_ref, b, k, e, rows, win):
    off = offs_ref[(b * rows + k) * N_EXPERTS + e]
    nxt = offs_ref[(b * rows + k + 1) * N_EXPERTS + e]
    base = lax.shift_right_logical(off, SLOT_ALIGN.bit_length() - 1) * SLOT_ALIGN
    nwin = lax.shift_right_logical(nxt - base + (win - 1), win.bit_length() - 1)
    return base, nwin


def _slot_targets(ids, start, cap, win):
    cb = pl.multiple_of(jnp.minimum(start, cap - win), SLOT_ALIGN)
    slot = ids + cb.astype(F32)
    return jnp.where(slot >= start.astype(F32), slot, -2.0), cb


def _gather_kernel(offs_ref, post_ref, h_ref, asp_ref, xs_ref, gs_ref, *, cap, win, rows):
    nseq, tm = h_ref.shape[0], h_ref.shape[1]
    c = TOK_TILE
    b0 = pl.program_id(0) * nseq
    k0 = pl.program_id(1) * (tm // c)

    @pl.when(pl.program_id(1) == 0)
    def _():
        xs_ref[...] = jnp.zeros(xs_ref.shape, xs_ref.dtype)
        gs_ref[...] = jnp.zeros(gs_ref.shape, gs_ref.dtype)

    sub = lax.broadcasted_iota(jnp.int32, (win, 1), 0).astype(F32)

    def one_chunk(s, cc, overflow):
        b = b0 + s
        k = k0 + cc
        cols = slice(cc * c, (cc + 1) * c)
        hk = h_ref[s, cols, :]
        gk = asp_ref[s, cols, :]

        def sel_rows(e, start):
            tgt, cb = _slot_targets(sub, start, cap, win)
            return jnp.where(post_ref[s, e:e + 1, cols] == tgt, 1.0, 0.0).astype(BF16), s * cap + cb

        if overflow:
            for e in range(N_EXPERTS):
                base, nwin = _window(offs_ref, b, k, e, rows, win)

                def extra(w, carry, e=e, base=base):
                    p, cbw = sel_rows(e, base + w * win)
                    xs_ref[e, pl.ds(cbw, win), :] += _dot(p, hk).astype(BF16)
                    gs_ref[e, pl.ds(cbw, win), :] += _dot(p, gk)
                    return carry

                lax.fori_loop(1, nwin, extra, 0)
            return

        for g0 in range(0, N_EXPERTS, GATHER_GROUP):
            parts, cbs = [], []
            for e in range(g0, g0 + GATHER_GROUP):
                base, _ = _window(offs_ref, b, k, e, rows, win)
                p, cb = sel_rows(e, base)
                parts.append(p)
                cbs.append(cb)
            p_grp = jnp.concatenate(parts, axis=0)
            r_grp = _dot(p_grp, hk).astype(BF16)
            g_grp = _dot(p_grp, gk)
            for j, cb in enumerate(cbs):
                xs_ref[g0 + j, pl.ds(cb, win), :] += r_grp[j * win:(j + 1) * win, :]
                gs_ref[g0 + j, pl.ds(cb, win), :] += g_grp[j * win:(j + 1) * win, :]

    for overflow in (False, True):
        for s in range(nseq):
            for cc in range(tm // c):
                one_chunk(s, cc, overflow)


def _gather_call(offs, post, h, asp, cap, win):
    nb, n, d = h.shape
    sb, tm = _seq_tiling(n, 0, EC_ROWS)
    rows = _offs_rows(n)
    grid_spec = pltpu.PrefetchScalarGridSpec(
        num_scalar_prefetch=1, grid=(nb // sb, n // tm),
        in_specs=[pl.BlockSpec((sb, N_EXPERTS, tm), lambda b, k, o: (b, 0, k)),
                  pl.BlockSpec((sb, tm, d), lambda b, k, o: (b, k, 0)),
                  pl.BlockSpec((sb, tm, LANES), lambda b, k, o: (b, k, 0))],
        out_specs=[pl.BlockSpec((N_EXPERTS, sb * cap, d), lambda b, k, o: (0, b, 0)),
                   pl.BlockSpec((N_EXPERTS, sb * cap, LANES), lambda b, k, o: (0, b, 0))])
    return pl.pallas_call(
        functools.partial(_gather_kernel, cap=cap, win=win, rows=rows),
        grid_spec=grid_spec,
        out_shape=[jax.ShapeDtypeStruct((N_EXPERTS, nb * cap, d), BF16),
                   jax.ShapeDtypeStruct((N_EXPERTS, nb * cap, LANES), F32)],
        compiler_params=_cp(48), name="ec_gather",
    )(offs.reshape(-1), post, h, asp)


def _ffn_kernel(xa_ref, xb_ref, ga_ref, gb_ref, w1_ref, w3_ref, w2_ref, ya_ref, yb_ref,
                acca_ref, accb_ref):
    e = pl.program_id(0)
    f = pl.program_id(1)
    w1 = w1_ref[...].astype(BF16)
    w3 = w3_ref[...].astype(BF16)
    w2 = w2_ref[...].astype(BF16)

    @pl.when((e == 0) & (f == 0))
    def _():
        acca_ref[...] = jnp.zeros(acca_ref.shape, F32)
        accb_ref[...] = jnp.zeros(accb_ref.shape, F32)

    for x_ref, g_ref, y_ref, acc_ref in ((xa_ref, ga_ref, ya_ref, acca_ref),
                                         (xb_ref, gb_ref, yb_ref, accb_ref)):
        x = x_ref[...]
        hid = jax.nn.silu(_dot(x, w1)) * _dot(x, w3)
        acc = jnp.where(f > 0, acc_ref[...], 0.0) + _dot(hid.astype(BF16), w2)
        acc_ref[...] = acc
        g = g_ref[...]
        lane = lax.broadcasted_iota(jnp.int32, g.shape, 1)
        keep = ((lane % N_EXPERTS) == e) & (lane < 3 * N_EXPERTS)
        gate = jnp.sum(jnp.where(keep, g, 0.0), axis=-1, keepdims=True)
        y_ref[...] = (acc * gate).astype(BF16)


def _ffn_call(xa, xb, ga, gb, w1, w3, w2, layer):
    ne, ra, d = xa.shape
    rb = xb.shape[1]
    ff = w1.shape[3]
    tf = FF_TILE
    rowsp = lambda r, c: pl.BlockSpec((None, r, c), lambda e, f: (e, 0, 0))
    return pl.pallas_call(
        _ffn_kernel,
        grid=(ne, ff // tf),
        in_specs=[rowsp(ra, d), rowsp(rb, d), rowsp(ra, LANES), rowsp(rb, LANES),
                  pl.BlockSpec((None, None, d, tf), lambda e, f: (layer, e, 0, f)),
                  pl.BlockSpec((None, None, d, tf), lambda e, f: (layer, e, 0, f)),
                  pl.BlockSpec((None, None, tf, d), lambda e, f: (layer, e, f, 0))],
        out_specs=[rowsp(ra, d), rowsp(rb, d)],
        out_shape=[jax.ShapeDtypeStruct((ne, ra, d), BF16), jax.ShapeDtypeStruct((ne, rb, d), BF16)],
        scratch_shapes=[pltpu.VMEM((ra, d), F32), pltpu.VMEM((rb, d), F32)],
        compiler_params=_cp(56), name="expert_swiglu",
    )(xa, xb, ga, gb, w1, w3, w2)


def _combine_kernel(offs_ref, pos_ref, y_ref, x_ref, g2_ref, ex_ref, *rest, cap, win, rows, last):
    if last:
        fg_ref, o_ref, on_ref = rest
    else:
        (o_ref,) = rest
    nseq, tm = x_ref.shape[0], x_ref.shape[1]
    c = TOK_TILE
    b0 = pl.program_id(0) * nseq
    k0 = pl.program_id(1) * (tm // c)
    per = MXU_DEPTH // win
    depth = per * win
    lane = lax.broadcasted_iota(jnp.int32, (1, depth), 1)
    lane_grp = lane // win
    lane_in = (lane % win).astype(F32)
    lane1 = lax.broadcasted_iota(jnp.int32, (1, win), 1).astype(F32)

    def one_chunk(s, cc, overflow):
        b = b0 + s
        k = k0 + cc
        rws = (s, slice(cc * c, (cc + 1) * c))
        if overflow:
            for e in range(N_EXPERTS):
                base, nwin = _window(offs_ref, b, k, e, rows, win)

                def extra(w, carry, e=e, base=base):
                    tgt_w, cbw = _slot_targets(lane1, base + w * win, cap, win)
                    pt = jnp.where(pos_ref[rws][:, e:e + 1] == tgt_w, 1.0, 0.0).astype(BF16)
                    o_ref[rws] += g2_ref[...] * _dot(pt, y_ref[e, pl.ds(s * cap + cbw, win), :])
                    return carry

                lax.fori_loop(1, nwin, extra, 0)
            if last:
                on_ref[rws] = _rms(o_ref[rws]) * fg_ref[...]
            return
        pos_wide = _dot(pos_ref[rws].astype(BF16), ex_ref[...])
        tot = None
        for gi, g0 in enumerate(range(0, N_EXPERTS, per)):
            tgt = jnp.full((1, depth), -2.0, F32)
            ys = []
            for j in range(per):
                base, _ = _window(offs_ref, b, k, g0 + j, rows, win)
                tgt_e, cb = _slot_targets(lane_in, base, cap, win)
                tgt = jnp.where(lane_grp == j, tgt_e, tgt)
                ys.append(y_ref[g0 + j, pl.ds(s * cap + cb, win), :])
            pt = jnp.where(pos_wide[:, gi * depth:(gi + 1) * depth] == tgt, 1.0, 0.0).astype(BF16)
            d = _dot(pt, jnp.concatenate(ys, axis=0))
            tot = d if tot is None else tot + d
        o_ref[rws] = x_ref[rws] + g2_ref[...] * tot

    for overflow in (False, True):
        for s in range(nseq):
            for cc in range(tm // c):
                one_chunk(s, cc, overflow)


def _combine_call(offs, pos, y, x1, modl, row0, row_step, final_g, cap, win, layer):
    nb, n, d = x1.shape
    sb, tm = _seq_tiling(n, row_step, EC_ROWS)
    rows = _offs_rows(n)
    last = final_g is not None
    assert cap <= 256
    spread = np.zeros((LANES, N_EXPERTS * win), np.float32)
    for e in range(N_EXPERTS):
        spread[e, e * win:(e + 1) * win] = 1.0
    spread = jnp.asarray(spread, BF16)
    tok = pl.BlockSpec((sb, tm, d), lambda b, k, o: (b, k, 0))
    in_specs = [pl.BlockSpec((sb, tm, LANES), lambda b, k, o: (b, k, 0)),
                pl.BlockSpec((N_EXPERTS, sb * cap, d), lambda b, k, o: (0, b, 0)),
                tok,
                pl.BlockSpec((None, None, None, 1, d),
                             lambda b, k, o: (layer, row0 + b * sb * row_step, 5, 0, 0)),
                pl.BlockSpec(spread.shape, lambda b, k, o: (0, 0))]
    ins = [offs.reshape(-1), pos, y, x1, modl, spread]
    if last:
        in_specs.append(pl.BlockSpec((1, d), lambda b, k, o: (0, 0)))
        ins.append(final_g)
    nout = 2 if last else 1
    grid_spec = pltpu.PrefetchScalarGridSpec(
        num_scalar_prefetch=1, grid=(nb // sb, n // tm), in_specs=in_specs, out_specs=[tok] * nout)
    return pl.pallas_call(
        functools.partial(_combine_kernel, cap=cap, win=win, rows=rows, last=last),
        grid_spec=grid_spec,
        out_shape=[jax.ShapeDtypeStruct((nb, n, d), F32)] * nout,
        compiler_params=_cp(56), name="ec_combine_final" if last else "ec_combine",
    )(*ins)


def _prep_weights(w_in, w_s, b_s, w_pool, pool_scale, q_norm_g, w_qb, kv_norm_g, w_kvb, w_out,
                  w_router, norm1_g, norm2_g):
    depth, d, cols = w_in.shape
    row = lambda a: a.reshape(depth, 1, -1)
    w_in_p = jnp.pad(w_in, ((0, 0), (0, 0), (0, PROJ_PAD - cols))).astype(BF16)
    wq = w_qb.reshape(depth, Q_LORA, C_HEADS, QK_NOPE + QK_ROPE)
    wq = jnp.pad(wq, ((0, 0), (0, 0), (0, 0), (0, HEAD_PAD - QK_NOPE - QK_ROPE)))
    wq_t = wq.reshape(depth, Q_LORA, C_HEADS * HEAD_PAD).transpose(0, 2, 1).astype(BF16)
    wkv = w_kvb.reshape(depth, KV_LORA, C_HEADS, QK_NOPE + V_DIM)
    wkn = jnp.pad(wkv[..., :QK_NOPE], ((0, 0), (0, 0), (0, 0), (0, HEAD_PAD - QK_NOPE)))
    wkn = wkn.reshape(depth, KV_LORA, C_HEADS * HEAD_PAD).astype(BF16)
    wv_t = wkv[..., QK_NOPE:].reshape(depth, KV_LORA, C_WIDTH).transpose(0, 2, 1).astype(BF16)
    assert A_HEADS == len(POOL_WINDOWS) and A_HEAD_DIM == POOL_GROUP
    ws_cat = w_s.transpose(0, 2, 1, 3).reshape(depth, CHUNK, A_HEADS * CHUNK).astype(BF16)
    b_rep = jnp.repeat(b_s.transpose(0, 2, 1), A_HEAD_DIM, axis=2)
    eye = jnp.eye(len(POOL_WINDOWS), dtype=F32)
    wp_bd = jnp.einsum("gh,lgcd->lgchd", eye, w_pool).reshape(depth, B_WIDTH, B_WIDTH).astype(BF16)
    wr_hi = w_router.astype(BF16)
    wr_lo = (w_router - wr_hi.astype(F32)).astype(BF16)
    wr = jnp.pad(jnp.concatenate([wr_hi, wr_lo], axis=2), ((0, 0), (0, 0), (0, LANES - 2 * N_EXPERTS)))
    return dict(
        norm1_g=row(norm1_g), norm2_g=row(norm2_g), w_in=w_in_p, q_norm_g=row(q_norm_g),
        kv_norm_g=row(kv_norm_g), wq_t=wq_t, wkn=wkn, wv_t=wv_t, w_s=ws_cat, b_s=b_rep, w_pool=wp_bd,
        pool_scale=row(pool_scale), w_out=w_out.astype(BF16), w_router=wr)


def _rope_tables(n):
    rows = n // GRID_W
    row = jnp.repeat(jnp.arange(rows), GRID_W).astype(F32)
    col = jnp.broadcast_to(jnp.arange(GRID_W), (rows, GRID_W)).reshape(-1).astype(F32)
    inv = 1.0 / (ROPE_THETA ** (jnp.arange(0, ROPE_AXIS, 2, dtype=F32) / ROPE_AXIS))
    ang_r = row[:, None] * inv[None, :]
    ang_c = col[:, None] * inv[None, :]
    cos_r, sin_r, cos_c, sin_c = jnp.cos(ang_r), jnp.sin(ang_r), jnp.cos(ang_c), jnp.sin(ang_c)
    pad = jnp.zeros((n, LANES - QK_ROPE), F32)
    cos_k = jnp.concatenate([cos_r, cos_r, cos_c, cos_c, pad], axis=1)
    sin_k = jnp.concatenate([-sin_r, sin_r, -sin_c, sin_c, pad], axis=1)
    return cos_k, sin_k, cos_r.T, sin_r.T, cos_c.T, sin_c.T


def kernel(x_prompt, x_sample, cache_ckv, cache_krope, c, c_ctx, w_ada, b_ada, norm1_g, norm2_g, w_in, w_s, b_s, w_pool, pool_scale, q_norm_g, w_qb, kv_norm_g, w_kvb, w_out, w_router, w_e1, w_e3, w_e2, final_norm_g):
    depth, d, _ = w_ada.shape
    nb_c, n_c, _ = x_prompt.shape
    nb_s, n_s, _ = x_sample.shape
    assert 1 + nb_s <= 8
    cond8 = jnp.concatenate([c_ctx[None, :], c, jnp.zeros((8 - 1 - nb_s, d), F32)], axis=0)
    mod = _mod_call(cond8, w_ada, b_ada).reshape(depth, 8, 6, 1, d)
    lw = _prep_weights(w_in, w_s, b_s, w_pool, pool_scale, q_norm_g, w_qb, kv_norm_g, w_kvb, w_out,
                       w_router, norm1_g, norm2_g)
    k_cache, vt_cache = _kvexp_call(cache_ckv, cache_krope, lw["wkn"], lw["wv_t"])
    tabs = _rope_tables(n_s)
    band = _pool_bands()
    final_g = final_norm_g.reshape(1, d)
    cap_c = EC_CAPACITY * n_c // N_EXPERTS
    cap_s = EC_CAPACITY * n_s // N_EXPERTS
    win_c = min(cap_c, 64)
    win_s = min(cap_s, 64)

    xc, xs = x_prompt, x_sample
    yc = ys = None
    ckvs, kropes = [], []
    for l in range(depth):
        uc, vac, pinc, qtc, khc, vtc, ckv_c, kr_c = _proj_call(xc, mod, 0, 0, lw, None, l)
        us, vas, pins, qts, khs, vts, _, _ = _proj_call(xs, mod, 1, 1, lw, tabs, l)
        ckvs.append(ckv_c)
        kropes.append(kr_c)
        oac, obc = _mix_call(uc, vac, pinc, lw, band, l)
        oas, obs = _mix_call(us, vas, pins, lw, band, l)
        occ = _attn_call(qtc, [(khc, vtc, None)])
        ocs = _attn_call(qts, [(k_cache, vt_cache, l), (khs, vts, None)])
        x1c, hc, affc, aspc = _outproj_call(xc, oac, obc, occ, mod, 0, 0, lw, l)
        x1s, hs, affs, asps = _outproj_call(xs, oas, obs, ocs, mod, 1, 1, lw, l)
        posc, postc, offc = _route_call(affc, cap_c)
        poss, posts, offs = _route_call(affs, cap_s)
        xgc, ggc = _gather_call(offc, postc, hc, aspc, cap_c, win_c)
        xgs, ggs = _gather_call(offs, posts, hs, asps, cap_s, win_s)
        ygs, ygc = _ffn_call(xgs, xgc, ggs, ggc, w_e1, w_e3, w_e2, l)
        fg = final_g if l == depth - 1 else None
        outc = _combine_call(offc, posc, ygc, x1c, mod, 0, 0, fg, cap_c, win_c, l)
        outs = _combine_call(offs, poss, ygs, x1s, mod, 1, 1, fg, cap_s, win_s, l)
        xc, xs = outc[0], outs[0]
    yc, ys = outc[1], outs[1]
    new_ckv = jnp.stack(ckvs, axis=1)
    new_krope = jnp.stack(kropes, axis=1)
    return (yc, ys, new_ckv, new_krope)
```

```python
import functools
import math

import numpy as np
import jax
import jax.numpy as jnp
from jax import lax
from jax.experimental import pallas as pl
from jax.experimental.pallas import tpu as pltpu

F32 = jnp.float32
BF16 = jnp.bfloat16

GRID_W = 64
A_HEADS = 4
A_HEAD_DIM = 64
A_WIDTH = A_HEADS * A_HEAD_DIM
CHUNK = 128
POOL_WINDOWS = (2, 4, 8, 16)
POOL_GROUP = 64
B_WIDTH = len(POOL_WINDOWS) * POOL_GROUP
C_HEADS = 8
QK_NOPE = 64
QK_ROPE = 32
V_DIM = 64
Q_LORA = 384
KV_LORA = 256
C_WIDTH = C_HEADS * V_DIM
ROPE_AXIS = QK_ROPE // 2
ROPE_THETA = 10000.0
ATTN_SCALE = (QK_NOPE + QK_ROPE) ** -0.5
N_EXPERTS = 16
EC_CAPACITY = 2
NORM_EPS = 1e-6

LANES = 128
HEAD_PAD = 128
PROJ_PAD = 1536
OFF_P = 2 * A_WIDTH
OFF_CQ = OFF_P + B_WIDTH
OFF_CKV = OFF_CQ + Q_LORA
OFF_KR = OFF_CKV + KV_LORA
MIX_UNROLL = 8
POOL_HALO = 8
TOK_TILE = 256
ROW_TILE = 1024
PROJ_SUB = 512
EC_ROWS = 512
SUB_ROWS = 256
V_ROWS = V_DIM + 16
KEY_CHUNK = 512
SCORE_BUFFERS = 2
ATTN_Q_BLOCKS = 2
Q_PRESCALE = ATTN_SCALE * math.log2(math.e)
FF_TILE = 512
MOD_TILE = 1536
SEARCH_BITS = 31
SLOT_ALIGN = 16
GATHER_GROUP = 4
MXU_DEPTH = 256


def _cp(vmem_mb):
    return pltpu.CompilerParams(vmem_limit_bytes=vmem_mb * 1024 * 1024)


def _layer_spec(a, layer):
    return pl.BlockSpec((None,) + a.shape[1:], lambda *_: (layer,) + (0,) * (a.ndim - 1))


def _dot(a, b):
    return jnp.dot(a, b, preferred_element_type=F32)


def _dot_nt(a, b):
    return lax.dot_general(a, b, (((1,), (1,)), ((), ())), preferred_element_type=F32)


def _split2(x):
    hi = x.astype(BF16)
    lo = (x - hi.astype(F32)).astype(BF16)
    return hi, lo


def _rms(x):
    return x * lax.rsqrt(jnp.mean(x * x, axis=-1, keepdims=True) + NORM_EPS)


def _load_rows(ref):
    return jnp.concatenate([ref[s] for s in range(ref.shape[0])], axis=0)


def _store_rows(ref, val):
    tm = ref.shape[1]
    for s in range(ref.shape[0]):
        ref[s] = val[s * tm:(s + 1) * tm, :]


def _with_ones_row(vt):
    rows = lax.broadcasted_iota(jnp.int32, (V_ROWS - V_DIM, vt.shape[1]), 0)
    return jnp.concatenate([vt, jnp.where(rows == 0, 1.0, 0.0).astype(BF16)], axis=0)


def _mod_kernel(c_ref, w_ref, b_ref, o_ref):
    a = jax.nn.silu(c_ref[...])
    a_hi, a_lo = _split2(a)
    w_hi, w_lo = _split2(w_ref[...])
    o_ref[...] = _dot(a_hi, w_hi) + _dot(a_lo, w_hi) + _dot(a_hi, w_lo) + b_ref[...]


def _mod_call(cond8, w_ada, b_ada):
    depth, d, n = w_ada.shape
    return pl.pallas_call(
        _mod_kernel,
        grid=(depth, n // MOD_TILE),
        in_specs=[pl.BlockSpec((8, d), lambda l, j: (0, 0)),
                  pl.BlockSpec((None, d, MOD_TILE), lambda l, j: (l, 0, j)),
                  pl.BlockSpec((None, 1, MOD_TILE), lambda l, j: (l, 0, j))],
        out_specs=pl.BlockSpec((None, 8, MOD_TILE), lambda l, j: (l, 0, j)),
        out_shape=jax.ShapeDtypeStruct((depth, 8, n), F32),
        compiler_params=_cp(40),
        name="adaln_mod",
    )(cond8, w_ada, b_ada.reshape(depth, 1, n))


def _proj_kernel(*refs, rope):
    if rope:
        (x_ref, sh_ref, sc_ref, g_ref, win_ref, qg_ref, kvg_ref, wq_ref, wkn_ref, wvt_ref,
         ck_ref, sk_ref, cr_ref, sr_ref, cc_ref, scl_ref,
         u_ref, va_ref, pin_ref, qt_ref, kh_ref, vt_ref, ckv_ref, kr_ref) = refs
    else:
        (x_ref, sh_ref, sc_ref, g_ref, win_ref, qg_ref, kvg_ref, wq_ref, wkn_ref, wvt_ref,
         u_ref, va_ref, pin_ref, qt_ref, kh_ref, vt_ref, ckv_ref, kr_ref) = refs
    nseq, tm = x_ref.shape[0], x_ref.shape[1]
    ps = min(tm, PROJ_SUB)
    blocks = [(s, t0) for s in range(nseq) for t0 in range(0, tm, ps)]

    projs = []
    for s, t0 in blocks:
        x = x_ref[s, t0:t0 + ps, :]
        h = _rms(x) * g_ref[...] * (1.0 + sc_ref[...]) + sh_ref[...]
        projs.append(_dot(h.astype(BF16), win_ref[...]))

    for (s, t0), proj in zip(blocks, projs):
        rows = (s, slice(t0, t0 + ps))
        u_ref[rows] = jax.nn.gelu(proj[:, 0:A_WIDTH])
        va_ref[rows] = jax.nn.gelu(proj[:, A_WIDTH:OFF_P]).astype(BF16)
        pin_ref[rows] = proj[:, OFF_P:OFF_CQ]
        cq = _rms(proj[:, OFF_CQ:OFF_CKV]) * qg_ref[...]
        ckv = _rms(proj[:, OFF_CKV:OFF_KR]) * kvg_ref[...]
        ckv_ref[rows] = ckv
        kr = proj[:, OFF_KR:OFF_KR + LANES]
        kr_ref[rows] = kr[:, 0:QK_ROPE]

        qt = _dot_nt(wq_ref[...], cq.astype(BF16))
        ckv_b = ckv.astype(BF16)
        kn = _dot(ckv_b, wkn_ref[...])
        vt = _dot_nt(wvt_ref[...], ckv_b)

        if rope:
            lane = lax.broadcasted_iota(jnp.int32, kr.shape, 1)
            half = ROPE_AXIS // 2
            swapped = jnp.where((lane & half) == 0,
                                pltpu.roll(kr, LANES - half, axis=1), pltpu.roll(kr, half, axis=1))
            kr = kr * ck_ref[t0:t0 + ps, :] + swapped * sk_ref[t0:t0 + ps, :]
            cr, sr = cr_ref[:, t0:t0 + ps], sr_ref[:, t0:t0 + ps]
            cc, scl = cc_ref[:, t0:t0 + ps], scl_ref[:, t0:t0 + ps]
        kr_shift = pltpu.roll(kr, QK_NOPE, axis=1)

        for hd in range(C_HEADS):
            r0 = hd * HEAD_PAD
            q_h = qt[r0:r0 + HEAD_PAD, :]
            if rope:
                b0 = QK_NOPE
                a, b = q_h[b0:b0 + 8, :], q_h[b0 + 8:b0 + 16, :]
                c, d = q_h[b0 + 16:b0 + 24, :], q_h[b0 + 24:b0 + 32, :]
                q_h = jnp.concatenate(
                    [q_h[0:b0, :], a * cr - b * sr, a * sr + b * cr, c * cc - d * scl, c * scl + d * cc,
                     q_h[b0 + 32:, :]], axis=0)
            q_h = (q_h * Q_PRESCALE).astype(BF16)
            for j in range(ps // TOK_TILE):
                qt_ref[s, t0 // TOK_TILE + j, hd] = q_h[:, j * TOK_TILE:(j + 1) * TOK_TILE]
            kh_ref[s, hd, t0:t0 + ps, :] = (kn[:, r0:r0 + HEAD_PAD] + kr_shift).astype(BF16)
            vt_ref[s, hd, :, t0:t0 + ps] = _with_ones_row(vt[hd * V_DIM:(hd + 1) * V_DIM, :].astype(BF16))


def _seq_tiling(n, row_step, rows=None):
    rows = ROW_TILE if rows is None else rows
    tm = min(n, rows)
    return (rows // tm if row_step == 0 else 1), tm


def _proj_call(x, modl, row0, row_step, lw, tabs, layer):
    nb, n, d = x.shape
    sb, tm = _seq_tiling(n, row_step)
    rope = tabs is not None
    assert not rope or sb == 1
    tok = lambda c: pl.BlockSpec((sb, tm, c), lambda b, i: (b, i, 0))
    const = lambda a: _layer_spec(a, layer)
    modspec = lambda j: pl.BlockSpec((None, None, None, 1, d),
                                     lambda b, i: (layer, row0 + b * sb * row_step, j, 0, 0))
    ins = [x, modl, modl, lw["norm1_g"], lw["w_in"], lw["q_norm_g"], lw["kv_norm_g"],
           lw["wq_t"], lw["wkn"], lw["wv_t"]]
    in_specs = [tok(d), modspec(0), modspec(1), const(lw["norm1_g"]), const(lw["w_in"]),
                const(lw["q_norm_g"]), const(lw["kv_norm_g"]), const(lw["wq_t"]), const(lw["wkn"]),
                const(lw["wv_t"])]
    if rope:
        cos_k, sin_k, cos_r, sin_r, cos_c, sin_c = tabs
        ins += [cos_k, sin_k, cos_r, sin_r, cos_c, sin_c]
        in_specs += [pl.BlockSpec((tm, LANES), lambda b, i: (i, 0))] * 2
        in_specs += [pl.BlockSpec((8, tm), lambda b, i: (0, i))] * 4
    out_shape = [jax.ShapeDtypeStruct((nb, n, A_WIDTH), F32),
                 jax.ShapeDtypeStruct((nb, n, A_WIDTH), BF16),
                 jax.ShapeDtypeStruct((nb, n, B_WIDTH), F32),
                 jax.ShapeDtypeStruct((nb, n // TOK_TILE, C_HEADS, HEAD_PAD, TOK_TILE), BF16),
                 jax.ShapeDtypeStruct((nb, C_HEADS, n, HEAD_PAD), BF16),
                 jax.ShapeDtypeStruct((nb, C_HEADS, V_ROWS,n), BF16),
                 jax.ShapeDtypeStruct((nb, n, KV_LORA), F32),
                 jax.ShapeDtypeStruct((nb, n, QK_ROPE), F32)]
    out_specs = [tok(A_WIDTH), tok(A_WIDTH), tok(B_WIDTH),
                 pl.BlockSpec((sb, tm // TOK_TILE, C_HEADS, HEAD_PAD, TOK_TILE), lambda b, i: (b, i, 0, 0, 0)),
                 pl.BlockSpec((sb, C_HEADS, tm, HEAD_PAD), lambda b, i: (b, 0, i, 0)),
                 pl.BlockSpec((sb, C_HEADS, V_ROWS, tm), lambda b, i: (b, 0, 0, i)),
                 tok(KV_LORA), tok(QK_ROPE)]
    return pl.pallas_call(
        functools.partial(_proj_kernel, rope=rope),
        grid=(nb // sb, n // tm), in_specs=in_specs, out_specs=out_specs, out_shape=out_shape,
        compiler_params=_cp(48), name="in_proj_rope" if rope else "in_proj",
    )(*ins)


def _kvexp_kernel(ckv_ref, kr_ref, wkn_ref, wvt_ref, place_ref, kh_ref, vt_ref):
    ckv_b = ckv_ref[...].astype(BF16)
    kn = _dot(ckv_b, wkn_ref[...])
    vt = _dot_nt(wvt_ref[...], ckv_b)
    kr_placed = _dot(kr_ref[...].astype(BF16), place_ref[...])
    for hd in range(C_HEADS):
        r0 = hd * HEAD_PAD
        kh_ref[hd] = (kn[:, r0:r0 + HEAD_PAD] + kr_placed).astype(BF16)
        vt_ref[hd] = _with_ones_row(vt[hd * V_DIM:(hd + 1) * V_DIM, :].astype(BF16))


def _kvexp_call(cache_ckv, cache_krope, wkn, wv_t):
    nb, depth, m, r = cache_ckv.shape
    place = np.zeros((QK_ROPE, HEAD_PAD), np.float32)
    place[np.arange(QK_ROPE), QK_NOPE + np.arange(QK_ROPE)] = 1.0
    place = jnp.asarray(place, BF16)
    return pl.pallas_call(
        _kvexp_kernel,
        grid=(depth, nb),
        in_specs=[pl.BlockSpec((None, None, m, r), lambda l, b: (b, l, 0, 0)),
                  pl.BlockSpec((None, None, m, QK_ROPE), lambda l, b: (b, l, 0, 0)),
                  pl.BlockSpec((None,) + wkn.shape[1:], lambda l, b: (l, 0, 0)),
                  pl.BlockSpec((None,) + wv_t.shape[1:], lambda l, b: (l, 0, 0)),
                  pl.BlockSpec(place.shape, lambda l, b: (0, 0))],
        out_specs=[pl.BlockSpec((None, None, C_HEADS, m, HEAD_PAD), lambda l, b: (l, b, 0, 0, 0)),
                   pl.BlockSpec((None, None, C_HEADS, V_ROWS,m), lambda l, b: (l, b, 0, 0, 0))],
        out_shape=[jax.ShapeDtypeStruct((depth, nb, C_HEADS, m, HEAD_PAD), BF16),
                   jax.ShapeDtypeStruct((depth, nb, C_HEADS, V_ROWS,m), BF16)],
        compiler_params=_cp(32), name="cache_kv_expand",
    )(cache_ckv, cache_krope, wkn, wv_t, place)


def _mix_kernel(u_ref, va_ref, pin_ref, ws_ref, bs_ref, band_ref, wp_ref, ps_ref,
                oa_ref, ob_ref, pad_ref):
    n = u_ref.shape[0]
    width = u_ref.shape[1]
    zero_rows = jnp.zeros((POOL_HALO, width), F32)
    pad_ref[0:POOL_HALO, :] = zero_rows
    pad_ref[n + POOL_HALO:n + 2 * POOL_HALO, :] = zero_rows
    pad_ref[POOL_HALO:n + POOL_HALO, :] = pin_ref[...]
    ngrp = len(POOL_WINDOWS)
    unroll = min(MIX_UNROLL, n // CHUNK)
    row = lax.broadcasted_iota(jnp.int32, (CHUNK, width), 0)
    grp_row = lax.broadcasted_iota(jnp.int32, (1, width), 1) // POOL_GROUP
    left = jnp.zeros((1, width), jnp.int32)
    right1 = jnp.zeros((1, width), jnp.int32)
    for g, w in enumerate(POOL_WINDOWS):
        left = jnp.where(grp_row == g, w // 2, left)
        right1 = jnp.where(grp_row == g, w - w // 2, right1)

    def group_masks(nrows):
        grp = lax.broadcasted_iota(jnp.int32, (nrows, width), 1) // POOL_GROUP
        return [jnp.where(grp == g, 1.0, 0.0).astype(BF16) for g in range(ngrp)]

    masks_v = group_masks(CHUNK)
    masks_w = group_masks(CHUNK + 2 * POOL_HALO)

    def group_stack(x, masks):
        return jnp.concatenate([x * m for m in masks], axis=0)

    def first_dots(c):
        r0 = pl.multiple_of(c * CHUNK, CHUNK)
        v_stack = group_stack(va_ref[pl.ds(r0, CHUNK), :], masks_v)
        s = _dot(ws_ref[...], v_stack) + bs_ref[...]
        oa_ref[pl.ds(r0, CHUNK), :] = (u_ref[pl.ds(r0, CHUNK), :] * s).astype(BF16)
        win = pad_ref[pl.ds(r0, CHUNK + 2 * POOL_HALO), :]
        w_hi, w_lo = _split2(win)
        tot = (_dot(band_ref[...], group_stack(w_hi, masks_w))
               + _dot(band_ref[...], group_stack(w_lo, masks_w)))
        t = r0 + row
        cnt = jnp.minimum(t + right1, n) - jnp.maximum(t - left, 0)
        p = win[POOL_HALO:POOL_HALO + CHUNK, :]
        return r0, (tot / cnt.astype(F32) - p).astype(BF16)

    def body(j, carry):
        diffs = [first_dots(j * unroll + r) for r in range(unroll)]
        for r0, diff in diffs:
            ob_ref[pl.ds(r0, CHUNK), :] = (_dot(diff, wp_ref[...]) * ps_ref[...]).astype(BF16)
        return carry

    lax.fori_loop(0, n // (CHUNK * unroll), body, 0)


def _pool_bands():
    rows = CHUNK + 2 * POOL_HALO
    band = np.zeros((len(POOL_WINDOWS), CHUNK, rows), np.float32)
    i = np.arange(CHUNK)[:, None]
    j = np.arange(rows)[None, :]
    for g, w in enumerate(POOL_WINDOWS):
        left = w // 2
        right = w - 1 - left
        band[g] = ((j >= i + POOL_HALO - left) & (j <= i + POOL_HALO + right)).astype(np.float32)
    return jnp.asarray(band.transpose(1, 0, 2).reshape(CHUNK, -1), BF16)


def _mix_call(u, va, pin, lw, band, layer):
    nb, n, w = u.shape
    seq = lambda: pl.BlockSpec((None, n, w), lambda b: (b, 0, 0))
    const = lambda a: _layer_spec(a, layer)
    return pl.pallas_call(
        _mix_kernel,
        grid=(nb,),
        in_specs=[seq(), seq(), seq(), const(lw["w_s"]), const(lw["b_s"]),
                  pl.BlockSpec(band.shape, lambda b: (0, 0)),
                  const(lw["w_pool"]), const(lw["pool_scale"])],
        out_specs=[seq(), seq()],
        out_shape=[jax.ShapeDtypeStruct((nb, n, w), BF16)] * 2,
        scratch_shapes=[pltpu.VMEM((n + 2 * POOL_HALO, w), F32)],
        compiler_params=_cp(48), name="mixers_ab",
    )(u, va, pin, lw["w_s"], lw["b_s"], band, lw["w_pool"], lw["pool_scale"])


def _attn_kernel(*refs, seg_lens):
    nseg = len(seg_lens)
    qt_ref = refs[0]
    k_refs = refs[1:1 + nseg]
    v_refs = refs[1 + nseg:1 + 2 * nseg]
    o_ref, ot_ref = refs[1 + 2 * nseg:3 + 2 * nseg]
    s_bufs = refs[3 + 2 * nseg:]
    nqb, _, _, tq = qt_ref.shape
    nitem = nqb * C_HEADS
    head_bits = C_HEADS.bit_length() - 1
    assert C_HEADS == 1 << head_bits

    def scores_into(it, s_ref):
        qb, hd = it >> head_bits, it & (C_HEADS - 1)
        q = qt_ref[qb, hd]
        mx = None
        r0 = 0
        for k_ref, m in zip(k_refs, seg_lens):
            for c0 in range(0, m, KEY_CHUNK):
                kc = min(KEY_CHUNK, m - c0)
                s = _dot(k_ref[hd, c0:c0 + kc, :], q)
                s_ref[r0:r0 + kc, :] = s
                cm = jnp.max(s, axis=0, keepdims=True)
                mx = cm if mx is None else jnp.maximum(mx, cm)
                r0 += kc
        return mx

    def finish(it, s_ref, mx):
        qb, hd = it >> head_bits, it & (C_HEADS - 1)
        acc = jnp.zeros((V_ROWS, tq), F32)
        r0 = 0
        for v_ref, m in zip(v_refs, seg_lens):
            p = jnp.exp2(s_ref[r0:r0 + m, :] - mx).astype(BF16)
            acc = acc + _dot(v_ref[hd], p)
            r0 += m
        rows = pl.ds(pl.multiple_of(hd * V_DIM, V_DIM), V_DIM)
        ot_ref[qb, rows, :] = acc[0:V_DIM, :] / acc[V_DIM:V_DIM + 1, :]

    nbuf = len(s_bufs)

    def step(t, r, mx):
        mx_next = scores_into(t + 1, s_bufs[(r + 1) % nbuf])
        finish(t, s_bufs[r], mx)
        return mx_next

    def rotation(j, mx):
        for r in range(nbuf):
            mx = step(nbuf * j + r, r, mx)
        return mx

    nsteps = nitem - 1
    mx = lax.fori_loop(0, nsteps // nbuf, rotation, scores_into(0, s_bufs[0]))
    for r in range(nsteps % nbuf):
        mx = step((nsteps // nbuf) * nbuf + r, r, mx)
    finish(nitem - 1, s_bufs[(nitem - 1) % nbuf], mx)
    for qb in range(nqb):
        o_ref[qb * tq:(qb + 1) * tq, :] = ot_ref[qb].T.astype(BF16)


def _attn_call(qt, segs):
    nb, nblk, _, _, tq = qt.shape
    nqb = min(nblk, ATTN_Q_BLOCKS)
    seg_lens = tuple(k.shape[-2] for k, _, _ in segs)

    def seg_spec(a, layer):
        if layer is None:
            return pl.BlockSpec((None,) + a.shape[1:], lambda b, i: (b, 0, 0, 0))
        return pl.BlockSpec((None, None) + a.shape[2:], lambda b, i: (layer, b, 0, 0, 0))

    in_specs = [pl.BlockSpec((None, nqb, C_HEADS, HEAD_PAD, tq), lambda b, i: (b, i, 0, 0, 0))]
    in_specs += [seg_spec(k, layer) for k, _, layer in segs]
    in_specs += [seg_spec(v, layer) for _, v, layer in segs]
    return pl.pallas_call(
        functools.partial(_attn_kernel, seg_lens=seg_lens),
        grid=(nb, nblk // nqb), in_specs=in_specs,
        out_specs=pl.BlockSpec((None, nqb * tq, C_WIDTH), lambda b, i: (b, i, 0)),
        out_shape=jax.ShapeDtypeStruct((nb, nblk * tq, C_WIDTH), BF16),
        scratch_shapes=[pltpu.VMEM((nqb, C_WIDTH, tq), F32)] + [pltpu.VMEM((sum(seg_lens), tq), F32)] * SCORE_BUFFERS,
        compiler_params=_cp(48), name="latent_attention",
    )(qt, *[k for k, _, _ in segs], *[v for _, v, _ in segs])


def _outproj_mix(rows, x_ref, oa_ref, ob_ref, oc_ref, wo_ref, g1_ref, sh_ref, sc_ref, ng_ref,
                 wr_ref, x1_ref, h_ref, aff_ref, asp_ref):
    mix = (_dot(oa_ref[rows], wo_ref[0:A_WIDTH, :])
           + _dot(ob_ref[rows], wo_ref[A_WIDTH:A_WIDTH + B_WIDTH, :])
           + _dot(oc_ref[rows], wo_ref[A_WIDTH + B_WIDTH:, :]))
    x1 = x_ref[rows] + g1_ref[...] * mix
    x1_ref[rows] = x1
    h = _rms(x1) * ng_ref[...] * (1.0 + sc_ref[...]) + sh_ref[...]
    h_hi, h_lo = _split2(h)
    h_ref[rows] = h_hi
    return h_hi, h_lo


def _outproj_route(rows, h_hi, h_lo, wr_ref, aff_ref, asp_ref):
    part = _dot(h_hi, wr_ref[...])
    logits = part + pltpu.roll(part, LANES - N_EXPERTS, axis=1) + _dot(h_lo, wr_ref[...])
    lane = lax.broadcasted_iota(jnp.int32, logits.shape, 1)
    logits = jnp.where(lane < N_EXPERTS, logits, -jnp.inf)
    ex = jnp.exp(logits - jnp.max(logits, axis=-1, keepdims=True))
    aff = ex / jnp.sum(ex, axis=-1, keepdims=True)
    aff_ref[rows] = aff
    a_hi = aff.astype(BF16).astype(F32)
    r1 = aff - a_hi
    a_lo = r1.astype(BF16).astype(F32)
    a_lo2 = r1 - a_lo
    packed = a_hi + pltpu.roll(a_lo, N_EXPERTS, axis=1) + pltpu.roll(a_lo2, 2 * N_EXPERTS, axis=1)
    asp_ref[rows] = packed.astype(BF16)


def _outproj_kernel(x_ref, *refs):
    nseq, tm = x_ref.shape[0], x_ref.shape[1]
    wr_ref, aff_ref, asp_ref = refs[8], refs[11], refs[12]
    blocks = [(s, slice(r0, r0 + SUB_ROWS)) for s in range(nseq) for r0 in range(0, tm, SUB_ROWS)]
    splits = [_outproj_mix(rows, x_ref, *refs) for rows in blocks]
    for rows, (h_hi, h_lo) in zip(blocks, splits):
        _outproj_route(rows, h_hi, h_lo, wr_ref, aff_ref, asp_ref)


def _outproj_call(x, oa, ob, oc, modl, row0, row_step, lw, layer):
    nb, n, d = x.shape
    sb, tm = _seq_tiling(n, row_step)
    tok = lambda c: pl.BlockSpec((sb, tm, c), lambda b, i: (b, i, 0))
    const = lambda a: _layer_spec(a, layer)
    modspec = lambda j: pl.BlockSpec((None, None, None, 1, d),
                                     lambda b, i: (layer, row0 + b * sb * row_step, j, 0, 0))
    return pl.pallas_call(
        _outproj_kernel,
        grid=(nb // sb, n // tm),
        in_specs=[tok(d), tok(A_WIDTH), tok(B_WIDTH), tok(C_WIDTH), const(lw["w_out"]),
                  modspec(2), modspec(3), modspec(4), const(lw["norm2_g"]),
                  const(lw["w_router"])],
        out_specs=[tok(d), tok(d), tok(LANES), tok(LANES)],
        out_shape=[jax.ShapeDtypeStruct((nb, n, d), F32), jax.ShapeDtypeStruct((nb, n, d), BF16),
                   jax.ShapeDtypeStruct((nb, n, LANES), F32), jax.ShapeDtypeStruct((nb, n, LANES), BF16)],
        compiler_params=_cp(48), name="out_proj_router",
    )(x, oa, ob, oc, lw["w_out"], modl, modl, modl, lw["norm2_g"], lw["w_router"])


def _route_kernel(aff_ref, pos_ref, post_ref, offs_ref, *, cap):
    nb, n, _ = aff_ref.shape
    c = TOK_TILE
    nck = n // c
    aff = aff_ref[...]
    capf = jnp.float32(cap)

    def search(i, bits):
        cand = bits | jnp.left_shift(jnp.int32(1), jnp.int32(SEARCH_BITS - 1) - i)
        cnt = jnp.sum(jnp.where(aff >= lax.bitcast_convert_type(cand, F32), 1.0, 0.0),
                      axis=1, keepdims=True)
        return jnp.where(cnt >= capf, cand, bits)

    bits = lax.fori_loop(0, SEARCH_BITS, search, jnp.zeros((nb, 1, LANES), jnp.int32))
    thr_all = lax.bitcast_convert_type(bits, F32)
    n_gt = jnp.sum(jnp.where(aff > thr_all, 1.0, 0.0), axis=1, keepdims=True)
    n_tie_all = capf - n_gt
    ri = lax.broadcasted_iota(jnp.int32, (c, c), 0)
    ci = lax.broadcasted_iota(jnp.int32, (c, c), 1)
    tri = jnp.where(ri > ci, 1.0, 0.0).astype(BF16)
    lane_ok = lax.broadcasted_iota(jnp.int32, (c, LANES), 1) < N_EXPERTS
    offs_ref[...] = jnp.zeros(offs_ref.shape, jnp.int32)
    for s in range(nb):
        thr = thr_all[s]
        n_tie = n_tie_all[s]
        tie_seen = jnp.zeros((1, LANES), F32)
        sel_seen = jnp.zeros((1, LANES), F32)
        for k in range(nck):
            kk = aff_ref[s, k * c:(k + 1) * c, :]
            gt = kk > thr
            eq = kk == thr
            eqf = jnp.where(eq, 1.0, 0.0)
            tie_rank = _dot(tri, eqf.astype(BF16)) + tie_seen
            sel = (gt | (eq & (tie_rank < n_tie))) & lane_ok
            self_ = jnp.where(sel, 1.0, 0.0)
            slot = _dot(tri, self_.astype(BF16)) + sel_seen
            pos = jnp.where(sel, slot, -1.0)
            pos_ref[s, k * c:(k + 1) * c, :] = pos
            post_ref[s, :, k * c:(k + 1) * c] = pos.T[0:N_EXPERTS, :]
            offs_ref[s, k:k + 1, :] = sel_seen[:, 0:N_EXPERTS].astype(jnp.int32)
            tie_seen = tie_seen + jnp.sum(eqf, axis=0, keepdims=True)
            sel_seen = sel_seen + jnp.sum(self_, axis=0, keepdims=True)
        offs_ref[s, nck:nck + 1, :] = sel_seen[:, 0:N_EXPERTS].astype(jnp.int32)


def _offs_rows(n):
    return -(-(n // TOK_TILE + 1) // 8) * 8


def _route_call(aff, cap):
    nb, n, _ = aff.shape
    rows = _offs_rows(n)
    return pl.pallas_call(
        functools.partial(_route_kernel, cap=cap),
        grid=(1,),
        in_specs=[pl.BlockSpec((nb, n, LANES), lambda i: (0, 0, 0))],
        out_specs=[pl.BlockSpec((nb, n, LANES), lambda i: (0, 0, 0)),
                   pl.BlockSpec((nb, N_EXPERTS, n), lambda i: (0, 0, 0)),
                   pl.BlockSpec((nb, rows, N_EXPERTS), lambda i: (0, 0, 0))],
        out_shape=[jax.ShapeDtypeStruct((nb, n, LANES), F32),
                   jax.ShapeDtypeStruct((nb, N_EXPERTS, n), F32),
                   jax.ShapeDtypeStruct((nb, rows, N_EXPERTS), jnp.int32)],
        compiler_params=_cp(32), name="ec_route",
    )(aff, )


def _window(offs_ref, b, k, e, rows, win):
    off = offs_ref[(b * rows + k) * N_EXPERTS + e]
    nxt = offs_ref[(b * rows + k + 1) * N_EXPERTS + e]
    base = lax.shift_right_logical(off, SLOT_ALIGN.bit_length() - 1) * SLOT_ALIGN
    nwin = lax.shift_right_logical(nxt - base + (win - 1), win.bit_length() - 1)
    return base, nwin


def _slot_targets(ids, start, cap, win):
    cb = pl.multiple_of(jnp.minimum(start, cap - win), SLOT_ALIGN)
    slot = ids + cb.astype(F32)
    return jnp.where(slot >= start.astype(F32), slot, -2.0), cb


def _gather_kernel(offs_ref, post_ref, h_ref, asp_ref, xs_ref, gs_ref, *, cap, win, rows):
    nseq, tm = h_ref.shape[0], h_ref.shape[1]
    c = TOK_TILE
    b0 = pl.program_id(0) * nseq
    k0 = pl.program_id(1) * (tm // c)

    @pl.when(pl.program_id(1) == 0)
    def _():
        xs_ref[...] = jnp.zeros(xs_ref.shape, xs_ref.dtype)
        gs_ref[...] = jnp.zeros(gs_ref.shape, gs_ref.dtype)

    sub = lax.broadcasted_iota(jnp.int32, (win, 1), 0).astype(F32)

    def one_chunk(s, cc, overflow):
        b = b0 + s
        k = k0 + cc
        cols = slice(cc * c, (cc + 1) * c)
        hk = h_ref[s, cols, :]
        gk = asp_ref[s, cols, :]

        def sel_rows(e, start):
            tgt, cb = _slot_targets(sub, start, cap, win)
            return jnp.where(post_ref[s, e:e + 1, cols] == tgt, 1.0, 0.0).astype(BF16), s * cap + cb

        if overflow:
            for e in range(N_EXPERTS):
                base, nwin = _window(offs_ref, b, k, e, rows, win)

                def extra(w, carry, e=e, base=base):
                    p, cbw = sel_rows(e, base + w * win)
                    xs_ref[e, pl.ds(cbw, win), :] += _dot(p, hk).astype(BF16)
                    gs_ref[e, pl.ds(cbw, win), :] += _dot(p, gk)
                    return carry

                lax.fori_loop(1, nwin, extra, 0)
            return

        for g0 in range(0, N_EXPERTS, GATHER_GROUP):
            parts, cbs = [], []
            for e in range(g0, g0 + GATHER_GROUP):
                base, _ = _window(offs_ref, b, k, e, rows, win)
                p, cb = sel_rows(e, base)
                parts.append(p)
                cbs.append(cb)
            p_grp = jnp.concatenate(parts, axis=0)
            r_grp = _dot(p_grp, hk).astype(BF16)
            g_grp = _dot(p_grp, gk)
            for j, cb in enumerate(cbs):
                xs_ref[g0 + j, pl.ds(cb, win), :] += r_grp[j * win:(j + 1) * win, :]
                gs_ref[g0 + j, pl.ds(cb, win), :] += g_grp[j * win:(j + 1) * win, :]

    for overflow in (False, True):
        for s in range(nseq):
            for cc in range(tm // c):
                one_chunk(s, cc, overflow)


def _gather_call(offs, post, h, asp, cap, win):
    nb, n, d = h.shape
    sb, tm = _seq_tiling(n, 0, EC_ROWS)
    rows = _offs_rows(n)
    grid_spec = pltpu.PrefetchScalarGridSpec(
        num_scalar_prefetch=1, grid=(nb // sb, n // tm),
        in_specs=[pl.BlockSpec((sb, N_EXPERTS, tm), lambda b, k, o: (b, 0, k)),
                  pl.BlockSpec((sb, tm, d), lambda b, k, o: (b, k, 0)),
                  pl.BlockSpec((sb, tm, LANES), lambda b, k, o: (b, k, 0))],
        out_specs=[pl.BlockSpec((N_EXPERTS, sb * cap, d), lambda b, k, o: (0, b, 0)),
                   pl.BlockSpec((N_EXPERTS, sb * cap, LANES), lambda b, k, o: (0, b, 0))])
    return pl.pallas_call(
        functools.partial(_gather_kernel, cap=cap, win=win, rows=rows),
        grid_spec=grid_spec,
        out_shape=[jax.ShapeDtypeStruct((N_EXPERTS, nb * cap, d), BF16),
                   jax.ShapeDtypeStruct((N_EXPERTS, nb * cap, LANES), F32)],
        compiler_params=_cp(48), name="ec_gather",
    )(offs.reshape(-1), post, h, asp)


def _ffn_kernel(xa_ref, xb_ref, ga_ref, gb_ref, w1_ref, w3_ref, w2_ref, ya_ref, yb_ref,
                acca_ref, accb_ref):
    e = pl.program_id(0)
    f = pl.program_id(1)
    w1 = w1_ref[...].astype(BF16)
    w3 = w3_ref[...].astype(BF16)
    w2 = w2_ref[...].astype(BF16)

    @pl.when((e == 0) & (f == 0))
    def _():
        acca_ref[...] = jnp.zeros(acca_ref.shape, F32)
        accb_ref[...] = jnp.zeros(accb_ref.shape, F32)

    groups = ((xa_ref, ga_ref, ya_ref, acca_ref), (xb_ref, gb_ref, yb_ref, accb_ref))
    hids = [(jax.nn.silu(_dot(x_ref[...], w1)) * _dot(x_ref[...], w3)).astype(BF16)
            for x_ref, _, _, _ in groups]
    for (x_ref, g_ref, y_ref, acc_ref), hid in zip(groups, hids):
        acc = jnp.where(f > 0, acc_ref[...], 0.0) + _dot(hid, w2)
        acc_ref[...] = acc
        g = g_ref[...]
        lane = lax.broadcasted_iota(jnp.int32, g.shape, 1)
        keep = ((lane % N_EXPERTS) == e) & (lane < 3 * N_EXPERTS)
        gate = jnp.sum(jnp.where(keep, g, 0.0), axis=-1, keepdims=True)
        y_ref[...] = (acc * gate).astype(BF16)


def _ffn_call(xa, xb, ga, gb, w1, w3, w2, layer):
    ne, ra, d = xa.shape
    rb = xb.shape[1]
    ff = w1.shape[3]
    tf = FF_TILE
    rowsp = lambda r, c: pl.BlockSpec((None, r, c), lambda e, f: (e, 0, 0))
    return pl.pallas_call(
        _ffn_kernel,
        grid=(ne, ff // tf),
        in_specs=[rowsp(ra, d), rowsp(rb, d), rowsp(ra, LANES), rowsp(rb, LANES),
                  pl.BlockSpec((None, None, d, tf), lambda e, f: (layer, e, 0, f)),
                  pl.BlockSpec((None, None, d, tf), lambda e, f: (layer, e, 0, f)),
                  pl.BlockSpec((None, None, tf, d), lambda e, f: (layer, e, f, 0))],
        out_specs=[rowsp(ra, d), rowsp(rb, d)],
        out_shape=[jax.ShapeDtypeStruct((ne, ra, d), BF16), jax.ShapeDtypeStruct((ne, rb, d), BF16)],
        scratch_shapes=[pltpu.VMEM((ra, d), F32), pltpu.VMEM((rb, d), F32)],
        compiler_params=_cp(56), name="expert_swiglu",
    )(xa, xb, ga, gb, w1, w3, w2)


def _combine_kernel(offs_ref, pos_ref, y_ref, x_ref, g2_ref, ex_ref, *rest, cap, win, rows, last):
    if last:
        fg_ref, o_ref, on_ref = rest
    else:
        (o_ref,) = rest
    nseq, tm = x_ref.shape[0], x_ref.shape[1]
    c = TOK_TILE
    b0 = pl.program_id(0) * nseq
    k0 = pl.program_id(1) * (tm // c)
    per = MXU_DEPTH // win
    depth = per * win
    lane = lax.broadcasted_iota(jnp.int32, (1, depth), 1)
    lane_grp = lane // win
    lane_in = (lane % win).astype(F32)
    lane1 = lax.broadcasted_iota(jnp.int32, (1, win), 1).astype(F32)

    def one_chunk(s, cc, overflow):
        b = b0 + s
        k = k0 + cc
        rws = (s, slice(cc * c, (cc + 1) * c))
        if overflow:
            for e in range(N_EXPERTS):
                base, nwin = _window(offs_ref, b, k, e, rows, win)

                def extra(w, carry, e=e, base=base):
                    tgt_w, cbw = _slot_targets(lane1, base + w * win, cap, win)
                    pt = jnp.where(pos_ref[rws][:, e:e + 1] == tgt_w, 1.0, 0.0).astype(BF16)
                    o_ref[rws] += g2_ref[...] * _dot(pt, y_ref[e, pl.ds(s * cap + cbw, win), :])
                    return carry

                lax.fori_loop(1, nwin, extra, 0)
            if last:
                on_ref[rws] = _rms(o_ref[rws]) * fg_ref[...]
            return
        pos_wide = _dot(pos_ref[rws].astype(BF16), ex_ref[...])
        tot = None
        for gi, g0 in enumerate(range(0, N_EXPERTS, per)):
            tgt = jnp.full((1, depth), -2.0, F32)
            ys = []
            for j in range(per):
                base, _ = _window(offs_ref, b, k, g0 + j, rows, win)
                tgt_e, cb = _slot_targets(lane_in, base, cap, win)
                tgt = jnp.where(lane_grp == j, tgt_e, tgt)
                ys.append(y_ref[g0 + j, pl.ds(s * cap + cb, win), :])
            pt = jnp.where(pos_wide[:, gi * depth:(gi + 1) * depth] == tgt, 1.0, 0.0).astype(BF16)
            d = _dot(pt, jnp.concatenate(ys, axis=0))
            tot = d if tot is None else tot + d
        o_ref[rws] = x_ref[rws] + g2_ref[...] * tot

    for overflow in (False, True):
        for s in range(nseq):
            for cc in range(tm // c):
                one_chunk(s, cc, overflow)


def _combine_call(offs, pos, y, x1, modl, row0, row_step, final_g, cap, win, layer):
    nb, n, d = x1.shape
    sb, tm = _seq_tiling(n, row_step, EC_ROWS)
    rows = _offs_rows(n)
    last = final_g is not None
    assert cap <= 256
    spread = np.zeros((LANES, N_EXPERTS * win), np.float32)
    for e in range(N_EXPERTS):
        spread[e, e * win:(e + 1) * win] = 1.0
    spread = jnp.asarray(spread, BF16)
    tok = pl.BlockSpec((sb, tm, d), lambda b, k, o: (b, k, 0))
    in_specs = [pl.BlockSpec((sb, tm, LANES), lambda b, k, o: (b, k, 0)),
                pl.BlockSpec((N_EXPERTS, sb * cap, d), lambda b, k, o: (0, b, 0)),
                tok,
                pl.BlockSpec((None, None, None, 1, d),
                             lambda b, k, o: (layer, row0 + b * sb * row_step, 5, 0, 0)),
                pl.BlockSpec(spread.shape, lambda b, k, o: (0, 0))]
    ins = [offs.reshape(-1), pos, y, x1, modl, spread]
    if last:
        in_specs.append(pl.BlockSpec((1, d), lambda b, k, o: (0, 0)))
        ins.append(final_g)
    nout = 2 if last else 1
    grid_spec = pltpu.PrefetchScalarGridSpec(
        num_scalar_prefetch=1, grid=(nb // sb, n // tm), in_specs=in_specs, out_specs=[tok] * nout)
    return pl.pallas_call(
        functools.partial(_combine_kernel, cap=cap, win=win, rows=rows, last=last),
        grid_spec=grid_spec,
        out_shape=[jax.ShapeDtypeStruct((nb, n, d), F32)] * nout,
        compiler_params=_cp(56), name="ec_combine_final" if last else "ec_combine",
    )(*ins)


def _prep_weights(w_in, w_s, b_s, w_pool, pool_scale, q_norm_g, w_qb, kv_norm_g, w_kvb, w_out,
                  w_router, norm1_g, norm2_g):
    depth, d, cols = w_in.shape
    row = lambda a: a.reshape(depth, 1, -1)
    w_in_p = jnp.pad(w_in, ((0, 0), (0, 0), (0, PROJ_PAD - cols))).astype(BF16)
    wq = w_qb.reshape(depth, Q_LORA, C_HEADS, QK_NOPE + QK_ROPE)
    wq = jnp.pad(wq, ((0, 0), (0, 0), (0, 0), (0, HEAD_PAD - QK_NOPE - QK_ROPE)))
    wq_t = wq.reshape(depth, Q_LORA, C_HEADS * HEAD_PAD).transpose(0, 2, 1).astype(BF16)
    wkv = w_kvb.reshape(depth, KV_LORA, C_HEADS, QK_NOPE + V_DIM)
    wkn = jnp.pad(wkv[..., :QK_NOPE], ((0, 0), (0, 0), (0, 0), (0, HEAD_PAD - QK_NOPE)))
    wkn = wkn.reshape(depth, KV_LORA, C_HEADS * HEAD_PAD).astype(BF16)
    wv_t = wkv[..., QK_NOPE:].reshape(depth, KV_LORA, C_WIDTH).transpose(0, 2, 1).astype(BF16)
    assert A_HEADS == len(POOL_WINDOWS) and A_HEAD_DIM == POOL_GROUP
    ws_cat = w_s.transpose(0, 2, 1, 3).reshape(depth, CHUNK, A_HEADS * CHUNK).astype(BF16)
    b_rep = jnp.repeat(b_s.transpose(0, 2, 1), A_HEAD_DIM, axis=2)
    eye = jnp.eye(len(POOL_WINDOWS), dtype=F32)
    wp_bd = jnp.einsum("gh,lgcd->lgchd", eye, w_pool).reshape(depth, B_WIDTH, B_WIDTH).astype(BF16)
    wr_hi = w_router.astype(BF16)
    wr_lo = (w_router - wr_hi.astype(F32)).astype(BF16)
    wr = jnp.pad(jnp.concatenate([wr_hi, wr_lo], axis=2), ((0, 0), (0, 0), (0, LANES - 2 * N_EXPERTS)))
    return dict(
        norm1_g=row(norm1_g), norm2_g=row(norm2_g), w_in=w_in_p, q_norm_g=row(q_norm_g),
        kv_norm_g=row(kv_norm_g), wq_t=wq_t, wkn=wkn, wv_t=wv_t, w_s=ws_cat, b_s=b_rep, w_pool=wp_bd,
        pool_scale=row(pool_scale), w_out=w_out.astype(BF16), w_router=wr)


def _rope_tables(n):
    rows = n // GRID_W
    row = jnp.repeat(jnp.arange(rows), GRID_W).astype(F32)
    col = jnp.broadcast_to(jnp.arange(GRID_W), (rows, GRID_W)).reshape(-1).astype(F32)
    inv = 1.0 / (ROPE_THETA ** (jnp.arange(0, ROPE_AXIS, 2, dtype=F32) / ROPE_AXIS))
    ang_r = row[:, None] * inv[None, :]
    ang_c = col[:, None] * inv[None, :]
    cos_r, sin_r, cos_c, sin_c = jnp.cos(ang_r), jnp.sin(ang_r), jnp.cos(ang_c), jnp.sin(ang_c)
    pad = jnp.zeros((n, LANES - QK_ROPE), F32)
    cos_k = jnp.concatenate([cos_r, cos_r, cos_c, cos_c, pad], axis=1)
    sin_k = jnp.concatenate([-sin_r, sin_r, -sin_c, sin_c, pad], axis=1)
    return cos_k, sin_k, cos_r.T, sin_r.T, cos_c.T, sin_c.T


def kernel(x_prompt, x_sample, cache_ckv, cache_krope, c, c_ctx, w_ada, b_ada, norm1_g, norm2_g, w_in, w_s, b_s, w_pool, pool_scale, q_norm_g, w_qb, kv_norm_g, w_kvb, w_out, w_router, w_e1, w_e3, w_e2, final_norm_g):
    depth, d, _ = w_ada.shape
    nb_c, n_c, _ = x_prompt.shape
    nb_s, n_s, _ = x_sample.shape
    assert 1 + nb_s <= 8
    cond8 = jnp.concatenate([c_ctx[None, :], c, jnp.zeros((8 - 1 - nb_s, d), F32)], axis=0)
    mod = _mod_call(cond8, w_ada, b_ada).reshape(depth, 8, 6, 1, d)
    lw = _prep_weights(w_in, w_s, b_s, w_pool, pool_scale, q_norm_g, w_qb, kv_norm_g, w_kvb, w_out,
                       w_router, norm1_g, norm2_g)
    k_cache, vt_cache = _kvexp_call(cache_ckv, cache_krope, lw["wkn"], lw["wv_t"])
    tabs = _rope_tables(n_s)
    band = _pool_bands()
    final_g = final_norm_g.reshape(1, d)
    cap_c = EC_CAPACITY * n_c // N_EXPERTS
    cap_s = EC_CAPACITY * n_s // N_EXPERTS
    win_c = min(cap_c, 64)
    win_s = min(cap_s, 64)

    xc, xs = x_prompt, x_sample
    yc = ys = None
    ckvs, kropes = [], []
    for l in range(depth):
        uc, vac, pinc, qtc, khc, vtc, ckv_c, kr_c = _proj_call(xc, mod, 0, 0, lw, None, l)
        us, vas, pins, qts, khs, vts, _, _ = _proj_call(xs, mod, 1, 1, lw, tabs, l)
        ckvs.append(ckv_c)
        kropes.append(kr_c)
        oac, obc = _mix_call(uc, vac, pinc, lw, band, l)
        oas, obs = _mix_call(us, vas, pins, lw, band, l)
        occ = _attn_call(qtc, [(khc, vtc, None)])
        ocs = _attn_call(qts, [(k_cache, vt_cache, l), (khs, vts, None)])
        x1c, hc, affc, aspc = _outproj_call(xc, oac, obc, occ, mod, 0, 0, lw, l)
        x1s, hs, affs, asps = _outproj_call(xs, oas, obs, ocs, mod, 1, 1, lw, l)
        posc, postc, offc = _route_call(affc, cap_c)
        poss, posts, offs = _route_call(affs, cap_s)
        xgc, ggc = _gather_call(offc, postc, hc, aspc, cap_c, win_c)
        xgs, ggs = _gather_call(offs, posts, hs, asps, cap_s, win_s)
        ygs, ygc = _ffn_call(xgs, xgc, ggs, ggc, w_e1, w_e3, w_e2, l)
        fg = final_g if l == depth - 1 else None
        outc = _combine_call(offc, posc, ygc, x1c, mod, 0, 0, fg, cap_c, win_c, l)
        outs = _combine_call(offs, poss, ygs, x1s, mod, 1, 1, fg, cap_s, win_s, l)
        xc, xs = outc[0], outs[0]
    yc, ys = outc[1], outs[1]
    new_ckv = jnp.stack(ckvs, axis=1)
    new_krope = jnp.stack(kropes, axis=1)
    return (yc, ys, new_ckv, new_krope)
```

```python
import functools
import math

import numpy as np
import jax
import jax.numpy as jnp
from jax import lax
from jax.experimental import pallas as pl
from jax.experimental.pallas import tpu as pltpu

F32 = jnp.float32
BF16 = jnp.bfloat16

GRID_W = 64
A_HEADS = 4
A_HEAD_DIM = 64
A_WIDTH = A_HEADS * A_HEAD_DIM
CHUNK = 128
POOL_WINDOWS = (2, 4, 8, 16)
POOL_GROUP = 64
B_WIDTH = len(POOL_WINDOWS) * POOL_GROUP
C_HEADS = 8
QK_NOPE = 64
QK_ROPE = 32
V_DIM = 64
Q_LORA = 384
KV_LORA = 256
C_WIDTH = C_HEADS * V_DIM
ROPE_AXIS = QK_ROPE // 2
ROPE_THETA = 10000.0
ATTN_SCALE = (QK_NOPE + QK_ROPE) ** -0.5
N_EXPERTS = 16
EC_CAPACITY = 2
NORM_EPS = 1e-6

LANES = 128
HEAD_PAD = 128
PROJ_PAD = 1536
OFF_P = 2 * A_WIDTH
OFF_CQ = OFF_P + B_WIDTH
OFF_CKV = OFF_CQ + Q_LORA
OFF_KR = OFF_CKV + KV_LORA
MIX_UNROLL = 8
POOL_HALO = 8
TOK_TILE = 256
ROW_TILE = 1024
PROJ_SUB = 512
EC_ROWS = 512
SUB_ROWS = 256
V_ROWS = V_DIM + 16
KEY_CHUNK = 512
SCORE_BUFFERS = 2
ATTN_Q_BLOCKS = 2
Q_PRESCALE = ATTN_SCALE * math.log2(math.e)
FF_TILE = 512
MOD_TILE = 1536
SEARCH_BITS = 31
SLOT_ALIGN = 16
GATHER_GROUP = 4
MXU_DEPTH = 256


def _cp(vmem_mb):
    return pltpu.CompilerParams(vmem_limit_bytes=vmem_mb * 1024 * 1024)


def _layer_spec(a, layer):
    return pl.BlockSpec((None,) + a.shape[1:], lambda *_: (layer,) + (0,) * (a.ndim - 1))


def _dot(a, b):
    return jnp.dot(a, b, preferred_element_type=F32)


def _dot_nt(a, b):
    return lax.dot_general(a, b, (((1,), (1,)), ((), ())), preferred_element_type=F32)


def _split2(x):
    hi = x.astype(BF16)
    lo = (x - hi.astype(F32)).astype(BF16)
    return hi, lo


def _rms(x):
    return x * lax.rsqrt(jnp.mean(x * x, axis=-1, keepdims=True) + NORM_EPS)


def _load_rows(ref):
    return jnp.concatenate([ref[s] for s in range(ref.shape[0])], axis=0)


def _store_rows(ref, val):
    tm = ref.shape[1]
    for s in range(ref.shape[0]):
        ref[s] = val[s * tm:(s + 1) * tm, :]


def _with_ones_row(vt):
    rows = lax.broadcasted_iota(jnp.int32, (V_ROWS - V_DIM, vt.shape[1]), 0)
    return jnp.concatenate([vt, jnp.where(rows == 0, 1.0, 0.0).astype(BF16)], axis=0)


def _mod_kernel(c_ref, w_ref, b_ref, o_ref):
    a = jax.nn.silu(c_ref[...])
    a_hi, a_lo = _split2(a)
    w_hi, w_lo = _split2(w_ref[...])
    o_ref[...] = _dot(a_hi, w_hi) + _dot(a_lo, w_hi) + _dot(a_hi, w_lo) + b_ref[...]


def _mod_call(cond8, w_ada, b_ada):
    depth, d, n = w_ada.shape
    return pl.pallas_call(
        _mod_kernel,
        grid=(depth, n // MOD_TILE),
        in_specs=[pl.BlockSpec((8, d), lambda l, j: (0, 0)),
                  pl.BlockSpec((None, d, MOD_TILE), lambda l, j: (l, 0, j)),
                  pl.BlockSpec((None, 1, MOD_TILE), lambda l, j: (l, 0, j))],
        out_specs=pl.BlockSpec((None, 8, MOD_TILE), lambda l, j: (l, 0, j)),
        out_shape=jax.ShapeDtypeStruct((depth, 8, n), F32),
        compiler_params=_cp(40),
        name="adaln_mod",
    )(cond8, w_ada, b_ada.reshape(depth, 1, n))


def _proj_kernel(*refs, rope):
    if rope:
        (x_ref, sh_ref, sc_ref, g_ref, win_ref, qg_ref, kvg_ref, wq_ref, wkn_ref, wvt_ref,
         ck_ref, sk_ref, cr_ref, sr_ref, cc_ref, scl_ref,
         u_ref, va_ref, pin_ref, qt_ref, kh_ref, vt_ref, ckv_ref, kr_ref) = refs
    else:
        (x_ref, sh_ref, sc_ref, g_ref, win_ref, qg_ref, kvg_ref, wq_ref, wkn_ref, wvt_ref,
         u_ref, va_ref, pin_ref, qt_ref, kh_ref, vt_ref, ckv_ref, kr_ref) = refs
    nseq, tm = x_ref.shape[0], x_ref.shape[1]
    ps = min(tm, PROJ_SUB)
    blocks = [(s, t0) for s in range(nseq) for t0 in range(0, tm, ps)]

    projs = []
    for s, t0 in blocks:
        x = x_ref[s, t0:t0 + ps, :]
        h = _rms(x) * g_ref[...] * (1.0 + sc_ref[...]) + sh_ref[...]
        projs.append(_dot(h.astype(BF16), win_ref[...]))

    for (s, t0), proj in zip(blocks, projs):
        rows = (s, slice(t0, t0 + ps))
        u_ref[rows] = jax.nn.gelu(proj[:, 0:A_WIDTH])
        va_ref[rows] = jax.nn.gelu(proj[:, A_WIDTH:OFF_P]).astype(BF16)
        pin_ref[rows] = proj[:, OFF_P:OFF_CQ]
        cq = _rms(proj[:, OFF_CQ:OFF_CKV]) * qg_ref[...]
        ckv = _rms(proj[:, OFF_CKV:OFF_KR]) * kvg_ref[...]
        ckv_ref[rows] = ckv
        kr = proj[:, OFF_KR:OFF_KR + LANES]
        kr_ref[rows] = kr[:, 0:QK_ROPE]

        qt = _dot_nt(wq_ref[...], cq.astype(BF16))
        ckv_b = ckv.astype(BF16)
        kn = _dot(ckv_b, wkn_ref[...])
        vt = _dot_nt(wvt_ref[...], ckv_b)

        if rope:
            lane = lax.broadcasted_iota(jnp.int32, kr.shape, 1)
            half = ROPE_AXIS // 2
            swapped = jnp.where((lane & half) == 0,
                                pltpu.roll(kr, LANES - half, axis=1), pltpu.roll(kr, half, axis=1))
            kr = kr * ck_ref[t0:t0 + ps, :] + swapped * sk_ref[t0:t0 + ps, :]
            cr, sr = cr_ref[:, t0:t0 + ps], sr_ref[:, t0:t0 + ps]
            cc, scl = cc_ref[:, t0:t0 + ps], scl_ref[:, t0:t0 + ps]
        kr_shift = pltpu.roll(kr, QK_NOPE, axis=1)

        for hd in range(C_HEADS):
            r0 = hd * HEAD_PAD
            q_h = qt[r0:r0 + HEAD_PAD, :]
            if rope:
                b0 = QK_NOPE
                a, b = q_h[b0:b0 + 8, :], q_h[b0 + 8:b0 + 16, :]
                c, d = q_h[b0 + 16:b0 + 24, :], q_h[b0 + 24:b0 + 32, :]
                q_h = jnp.concatenate(
                    [q_h[0:b0, :], a * cr - b * sr, a * sr + b * cr, c * cc - d * scl, c * scl + d * cc,
                     q_h[b0 + 32:, :]], axis=0)
            q_h = (q_h * Q_PRESCALE).astype(BF16)
            for j in range(ps // TOK_TILE):
                qt_ref[s, t0 // TOK_TILE + j, hd] = q_h[:, j * TOK_TILE:(j + 1) * TOK_TILE]
            kh_ref[s, hd, t0:t0 + ps, :] = (kn[:, r0:r0 + HEAD_PAD] + kr_shift).astype(BF16)
            vt_ref[s, hd, :, t0:t0 + ps] = _with_ones_row(vt[hd * V_DIM:(hd + 1) * V_DIM, :].astype(BF16))


def _seq_tiling(n, row_step, rows=None):
    rows = ROW_TILE if rows is None else rows
    tm = min(n, rows)
    return (rows // tm if row_step == 0 else 1), tm


def _proj_call(x, modl, row0, row_step, lw, tabs, layer):
    nb, n, d = x.shape
    sb, tm = _seq_tiling(n, row_step)
    rope = tabs is not None
    assert not rope or sb == 1
    tok = lambda c: pl.BlockSpec((sb, tm, c), lambda b, i: (b, i, 0))
    const = lambda a: _layer_spec(a, layer)
    modspec = lambda j: pl.BlockSpec((None, None, None, 1, d),
                                     lambda b, i: (layer, row0 + b * sb * row_step, j, 0, 0))
    ins = [x, modl, modl, lw["norm1_g"], lw["w_in"], lw["q_norm_g"], lw["kv_norm_g"],
           lw["wq_t"], lw["wkn"], lw["wv_t"]]
    in_specs = [tok(d), modspec(0), modspec(1), const(lw["norm1_g"]), const(lw["w_in"]),
                const(lw["q_norm_g"]), const(lw["kv_norm_g"]), const(lw["wq_t"]), const(lw["wkn"]),
                const(lw["wv_t"])]
    if rope:
        cos_k, sin_k, cos_r, sin_r, cos_c, sin_c = tabs
        ins += [cos_k, sin_k, cos_r, sin_r, cos_c, sin_c]
        in_specs += [pl.BlockSpec((tm, LANES), lambda b, i: (i, 0))] * 2
        in_specs += [pl.BlockSpec((8, tm), lambda b, i: (0, i))] * 4
    out_shape = [jax.ShapeDtypeStruct((nb, n, A_WIDTH), F32),
                 jax.ShapeDtypeStruct((nb, n, A_WIDTH), BF16),
                 jax.ShapeDtypeStruct((nb, n, B_WIDTH), F32),
                 jax.ShapeDtypeStruct((nb, n // TOK_TILE, C_HEADS, HEAD_PAD, TOK_TILE), BF16),
                 jax.ShapeDtypeStruct((nb, C_HEADS, n, HEAD_PAD), BF16),
                 jax.ShapeDtypeStruct((nb, C_HEADS, V_ROWS,n), BF16),
                 jax.ShapeDtypeStruct((nb, n, KV_LORA), F32),
                 jax.ShapeDtypeStruct((nb, n, QK_ROPE), F32)]
    out_specs = [tok(A_WIDTH), tok(A_WIDTH), tok(B_WIDTH),
                 pl.BlockSpec((sb, tm // TOK_TILE, C_HEADS, HEAD_PAD, TOK_TILE), lambda b, i: (b, i, 0, 0, 0)),
                 pl.BlockSpec((sb, C_HEADS, tm, HEAD_PAD), lambda b, i: (b, 0, i, 0)),
                 pl.BlockSpec((sb, C_HEADS, V_ROWS, tm), lambda b, i: (b, 0, 0, i)),
                 tok(KV_LORA), tok(QK_ROPE)]
    return pl.pallas_call(
        functools.partial(_proj_kernel, rope=rope),
        grid=(nb // sb, n // tm), in_specs=in_specs, out_specs=out_specs, out_shape=out_shape,
        compiler_params=_cp(48), name="in_proj_rope" if rope else "in_proj",
    )(*ins)


def _kvexp_kernel(ckv_ref, kr_ref, wkn_ref, wvt_ref, place_ref, kh_ref, vt_ref):
    ckv_b = ckv_ref[...].astype(BF16)
    kn = _dot(ckv_b, wkn_ref[...])
    vt = _dot_nt(wvt_ref[...], ckv_b)
    kr_placed = _dot(kr_ref[...].astype(BF16), place_ref[...])
    for hd in range(C_HEADS):
        r0 = hd * HEAD_PAD
        kh_ref[hd] = (kn[:, r0:r0 + HEAD_PAD] + kr_placed).astype(BF16)
        vt_ref[hd] = _with_ones_row(vt[hd * V_DIM:(hd + 1) * V_DIM, :].astype(BF16))


def _kvexp_call(cache_ckv, cache_krope, wkn, wv_t):
    nb, depth, m, r = cache_ckv.shape
    place = np.zeros((QK_ROPE, HEAD_PAD), np.float32)
    place[np.arange(QK_ROPE), QK_NOPE + np.arange(QK_ROPE)] = 1.0
    place = jnp.asarray(place, BF16)
    return pl.pallas_call(
        _kvexp_kernel,
        grid=(depth, nb),
        in_specs=[pl.BlockSpec((None, None, m, r), lambda l, b: (b, l, 0, 0)),
                  pl.BlockSpec((None, None, m, QK_ROPE), lambda l, b: (b, l, 0, 0)),
                  pl.BlockSpec((None,) + wkn.shape[1:], lambda l, b: (l, 0, 0)),
                  pl.BlockSpec((None,) + wv_t.shape[1:], lambda l, b: (l, 0, 0)),
                  pl.BlockSpec(place.shape, lambda l, b: (0, 0))],
        out_specs=[pl.BlockSpec((None, None, C_HEADS, m, HEAD_PAD), lambda l, b: (l, b, 0, 0, 0)),
                   pl.BlockSpec((None, None, C_HEADS, V_ROWS,m), lambda l, b: (l, b, 0, 0, 0))],
        out_shape=[jax.ShapeDtypeStruct((depth, nb, C_HEADS, m, HEAD_PAD), BF16),
                   jax.ShapeDtypeStruct((depth, nb, C_HEADS, V_ROWS,m), BF16)],
        compiler_params=_cp(32), name="cache_kv_expand",
    )(cache_ckv, cache_krope, wkn, wv_t, place)


def _mix_kernel(u_ref, va_ref, pin_ref, ws_ref, bs_ref, band_ref, wp_ref, ps_ref,
                oa_ref, ob_ref, pad_ref):
    n = u_ref.shape[0]
    width = u_ref.shape[1]
    zero_rows = jnp.zeros((POOL_HALO, width), F32)
    pad_ref[0:POOL_HALO, :] = zero_rows
    pad_ref[n + POOL_HALO:n + 2 * POOL_HALO, :] = zero_rows
    pad_ref[POOL_HALO:n + POOL_HALO, :] = pin_ref[...]
    ngrp = len(POOL_WINDOWS)
    unroll = min(MIX_UNROLL, n // CHUNK)
    row = lax.broadcasted_iota(jnp.int32, (CHUNK, width), 0)
    grp_row = lax.broadcasted_iota(jnp.int32, (1, width), 1) // POOL_GROUP
    left = jnp.zeros((1, width), jnp.int32)
    right1 = jnp.zeros((1, width), jnp.int32)
    for g, w in enumerate(POOL_WINDOWS):
        left = jnp.where(grp_row == g, w // 2, left)
        right1 = jnp.where(grp_row == g, w - w // 2, right1)

    def group_masks(nrows):
        grp = lax.broadcasted_iota(jnp.int32, (nrows, width), 1) // POOL_GROUP
        return [jnp.where(grp == g, 1.0, 0.0).astype(BF16) for g in range(ngrp)]

    masks_v = group_masks(CHUNK)
    masks_w = group_masks(CHUNK + 2 * POOL_HALO)

    def group_stack(x, masks):
        return jnp.concatenate([x * m for m in masks], axis=0)

    def first_dots(c):
        r0 = pl.multiple_of(c * CHUNK, CHUNK)
        v_stack = group_stack(va_ref[pl.ds(r0, CHUNK), :], masks_v)
        s = _dot(ws_ref[...], v_stack) + bs_ref[...]
        oa_ref[pl.ds(r0, CHUNK), :] = (u_ref[pl.ds(r0, CHUNK), :] * s).astype(BF16)
        win = pad_ref[pl.ds(r0, CHUNK + 2 * POOL_HALO), :]
        w_hi, w_lo = _split2(win)
        tot = (_dot(band_ref[...], group_stack(w_hi, masks_w))
               + _dot(band_ref[...], group_stack(w_lo, masks_w)))
        t = r0 + row
        cnt = jnp.minimum(t + right1, n) - jnp.maximum(t - left, 0)
        p = win[POOL_HALO:POOL_HALO + CHUNK, :]
        return r0, (tot / cnt.astype(F32) - p).astype(BF16)

    def body(j, carry):
        diffs = [first_dots(j * unroll + r) for r in range(unroll)]
        for r0, diff in diffs:
            ob_ref[pl.ds(r0, CHUNK), :] = (_dot(diff, wp_ref[...]) * ps_ref[...]).astype(BF16)
        return carry

    lax.fori_loop(0, n // (CHUNK * unroll), body, 0)


def _pool_bands():
    rows = CHUNK + 2 * POOL_HALO
    band = np.zeros((len(POOL_WINDOWS), CHUNK, rows), np.float32)
    i = np.arange(CHUNK)[:, None]
    j = np.arange(rows)[None, :]
    for g, w in enumerate(POOL_WINDOWS):
        left = w // 2
        right = w - 1 - left
        band[g] = ((j >= i + POOL_HALO - left) & (j <= i + POOL_HALO + right)).astype(np.float32)
    return jnp.asarray(band.transpose(1, 0, 2).reshape(CHUNK, -1), BF16)


def _mix_call(u, va, pin, lw, band, layer):
    nb, n, w = u.shape
    seq = lambda: pl.BlockSpec((None, n, w), lambda b: (b, 0, 0))
    const = lambda a: _layer_spec(a, layer)
    return pl.pallas_call(
        _mix_kernel,
        grid=(nb,),
        in_specs=[seq(), seq(), seq(), const(lw["w_s"]), const(lw["b_s"]),
                  pl.BlockSpec(band.shape, lambda b: (0, 0)),
                  const(lw["w_pool"]), const(lw["pool_scale"])],
        out_specs=[seq(), seq()],
        out_shape=[jax.ShapeDtypeStruct((nb, n, w), BF16)] * 2,
        scratch_shapes=[pltpu.VMEM((n + 2 * POOL_HALO, w), F32)],
        compiler_params=_cp(48), name="mixers_ab",
    )(u, va, pin, lw["w_s"], lw["b_s"], band, lw["w_pool"], lw["pool_scale"])


def _attn_kernel(*refs, seg_lens):
    nseg = len(seg_lens)
    qt_ref = refs[0]
    k_refs = refs[1:1 + nseg]
    v_refs = refs[1 + nseg:1 + 2 * nseg]
    o_ref, ot_ref = refs[1 + 2 * nseg:3 + 2 * nseg]
    s_bufs = refs[3 + 2 * nseg:]
    nsq, nqb, _, _, tq = qt_ref.shape
    nitem = nsq * nqb * C_HEADS
    head_bits = C_HEADS.bit_length() - 1
    qb_bits = nqb.bit_length() - 1
    assert C_HEADS == 1 << head_bits and nqb == 1 << qb_bits

    def unpack(it):
        blk = it >> head_bits
        return blk >> qb_bits, blk, blk & (nqb - 1), it & (C_HEADS - 1)

    def scores_into(it, s_ref):
        sq, _, qb, hd = unpack(it)
        q = qt_ref[sq, qb, hd]
        mx = None
        r0 = 0
        for k_ref, m in zip(k_refs, seg_lens):
            for c0 in range(0, m, KEY_CHUNK):
                kc = min(KEY_CHUNK, m - c0)
                s = _dot(k_ref[sq, hd, c0:c0 + kc, :], q)
                s_ref[r0:r0 + kc, :] = s
                cm = jnp.max(s, axis=0, keepdims=True)
                mx = cm if mx is None else jnp.maximum(mx, cm)
                r0 += kc
        return mx

    def finish(it, s_ref, mx):
        sq, blk, _, hd = unpack(it)
        acc = jnp.zeros((V_ROWS, tq), F32)
        r0 = 0
        for v_ref, m in zip(v_refs, seg_lens):
            for c0 in range(0, m, KEY_CHUNK):
                kc = min(KEY_CHUNK, m - c0)
                p = jnp.exp2(s_ref[r0:r0 + kc, :] - mx).astype(BF16)
                acc = acc + _dot(v_ref[sq, hd, :, c0:c0 + kc], p)
                r0 += kc
        rows = pl.ds(pl.multiple_of(hd * V_DIM, V_DIM), V_DIM)
        ot_ref[blk, rows, :] = acc[0:V_DIM, :] / acc[V_DIM:V_DIM + 1, :]

    nbuf = len(s_bufs)

    def step(t, r, mx):
        mx_next = scores_into(t + 1, s_bufs[(r + 1) % nbuf])
        finish(t, s_bufs[r], mx)
        return mx_next

    def rotation(j, mx):
        for r in range(nbuf):
            mx = step(nbuf * j + r, r, mx)
        return mx

    nsteps = nitem - 1
    mx = lax.fori_loop(0, nsteps // nbuf, rotation, scores_into(0, s_bufs[0]))
    for r in range(nsteps % nbuf):
        mx = step((nsteps // nbuf) * nbuf + r, r, mx)
    finish(nitem - 1, s_bufs[(nitem - 1) % nbuf], mx)
    for sq in range(nsq):
        for qb in range(nqb):
            o_ref[sq, qb * tq:(qb + 1) * tq, :] = ot_ref[sq * nqb + qb].T.astype(BF16)


def _attn_call(qt, segs):
    nb, nblk, _, _, tq = qt.shape
    nqb = min(nblk, ATTN_Q_BLOCKS)
    sb = ATTN_Q_BLOCKS // nqb
    seg_lens = tuple(k.shape[-2] for k, _, _ in segs)

    def seg_spec(a, layer):
        if layer is None:
            return pl.BlockSpec((sb,) + a.shape[1:], lambda b, i: (b, 0, 0, 0))
        return pl.BlockSpec((None, sb) + a.shape[2:], lambda b, i: (layer, b, 0, 0, 0))

    in_specs = [pl.BlockSpec((sb, nqb, C_HEADS, HEAD_PAD, tq), lambda b, i: (b, i, 0, 0, 0))]
    in_specs += [seg_spec(k, layer) for k, _, layer in segs]
    in_specs += [seg_spec(v, layer) for _, v, layer in segs]
    return pl.pallas_call(
        functools.partial(_attn_kernel, seg_lens=seg_lens),
        grid=(nb // sb, nblk // nqb), in_specs=in_specs,
        out_specs=pl.BlockSpec((sb, nqb * tq, C_WIDTH), lambda b, i: (b, i, 0)),
        out_shape=jax.ShapeDtypeStruct((nb, nblk * tq, C_WIDTH), BF16),
        scratch_shapes=([pltpu.VMEM((sb * nqb, C_WIDTH, tq), F32)]
                        + [pltpu.VMEM((sum(seg_lens), tq), F32)] * SCORE_BUFFERS),
        compiler_params=_cp(48), name="latent_attention",
    )(qt, *[k for k, _, _ in segs], *[v for _, v, _ in segs])


def _outproj_mix(rows, x_ref, oa_ref, ob_ref, oc_ref, wo_ref, g1_ref, sh_ref, sc_ref, ng_ref,
                 wr_ref, x1_ref, h_ref, aff_ref, asp_ref):
    mix = (_dot(oa_ref[rows], wo_ref[0:A_WIDTH, :])
           + _dot(ob_ref[rows], wo_ref[A_WIDTH:A_WIDTH + B_WIDTH, :])
           + _dot(oc_ref[rows], wo_ref[A_WIDTH + B_WIDTH:, :]))
    x1 = x_ref[rows] + g1_ref[...] * mix
    x1_ref[rows] = x1
    h = _rms(x1) * ng_ref[...] * (1.0 + sc_ref[...]) + sh_ref[...]
    h_hi, h_lo = _split2(h)
    h_ref[rows] = h_hi
    return h_hi, h_lo


def _outproj_route(rows, h_hi, h_lo, wr_ref, aff_ref, asp_ref):
    part = _dot(h_hi, wr_ref[...])
    logits = part + pltpu.roll(part, LANES - N_EXPERTS, axis=1) + _dot(h_lo, wr_ref[...])
    lane = lax.broadcasted_iota(jnp.int32, logits.shape, 1)
    logits = jnp.where(lane < N_EXPERTS, logits, -jnp.inf)
    ex = jnp.exp(logits - jnp.max(logits, axis=-1, keepdims=True))
    aff = ex / jnp.sum(ex, axis=-1, keepdims=True)
    aff_ref[rows] = aff
    a_hi = aff.astype(BF16).astype(F32)
    r1 = aff - a_hi
    a_lo = r1.astype(BF16).astype(F32)
    a_lo2 = r1 - a_lo
    packed = a_hi + pltpu.roll(a_lo, N_EXPERTS, axis=1) + pltpu.roll(a_lo2, 2 * N_EXPERTS, axis=1)
    asp_ref[rows] = packed.astype(BF16)


def _outproj_kernel(x_ref, *refs):
    nseq, tm = x_ref.shape[0], x_ref.shape[1]
    wr_ref, aff_ref, asp_ref = refs[8], refs[11], refs[12]
    blocks = [(s, slice(r0, r0 + SUB_ROWS)) for s in range(nseq) for r0 in range(0, tm, SUB_ROWS)]
    splits = [_outproj_mix(rows, x_ref, *refs) for rows in blocks]
    for rows, (h_hi, h_lo) in zip(blocks, splits):
        _outproj_route(rows, h_hi, h_lo, wr_ref, aff_ref, asp_ref)


def _outproj_call(x, oa, ob, oc, modl, row0, row_step, lw, layer):
    nb, n, d = x.shape
    sb, tm = _seq_tiling(n, row_step)
    tok = lambda c: pl.BlockSpec((sb, tm, c), lambda b, i: (b, i, 0))
    const = lambda a: _layer_spec(a, layer)
    modspec = lambda j: pl.BlockSpec((None, None, None, 1, d),
                                     lambda b, i: (layer, row0 + b * sb * row_step, j, 0, 0))
    return pl.pallas_call(
        _outproj_kernel,
        grid=(nb // sb, n // tm),
        in_specs=[tok(d), tok(A_WIDTH), tok(B_WIDTH), tok(C_WIDTH), const(lw["w_out"]),
                  modspec(2), modspec(3), modspec(4), const(lw["norm2_g"]),
                  const(lw["w_router"])],
        out_specs=[tok(d), tok(d), tok(LANES), tok(LANES)],
        out_shape=[jax.ShapeDtypeStruct((nb, n, d), F32), jax.ShapeDtypeStruct((nb, n, d), BF16),
                   jax.ShapeDtypeStruct((nb, n, LANES), F32), jax.ShapeDtypeStruct((nb, n, LANES), BF16)],
        compiler_params=_cp(48), name="out_proj_router",
    )(x, oa, ob, oc, lw["w_out"], modl, modl, modl, lw["norm2_g"], lw["w_router"])


def _route_kernel(aff_ref, pos_ref, post_ref, offs_ref, *, cap):
    nb, n, _ = aff_ref.shape
    c = TOK_TILE
    nck = n // c
    aff = aff_ref[...]
    capf = jnp.float32(cap)

    def search(i, bits):
        cand = bits | jnp.left_shift(jnp.int32(1), jnp.int32(SEARCH_BITS - 1) - i)
        cnt = jnp.sum(jnp.where(aff >= lax.bitcast_convert_type(cand, F32), 1.0, 0.0),
                      axis=1, keepdims=True)
        return jnp.where(cnt >= capf, cand, bits)

    bits = lax.fori_loop(0, SEARCH_BITS, search, jnp.zeros((nb, 1, LANES), jnp.int32))
    thr_all = lax.bitcast_convert_type(bits, F32)
    n_gt = jnp.sum(jnp.where(aff > thr_all, 1.0, 0.0), axis=1, keepdims=True)
    n_tie_all = capf - n_gt
    ri = lax.broadcasted_iota(jnp.int32, (c, c), 0)
    ci = lax.broadcasted_iota(jnp.int32, (c, c), 1)
    tri = jnp.where(ri > ci, 1.0, 0.0).astype(BF16)
    lane_ok = lax.broadcasted_iota(jnp.int32, (c, LANES), 1) < N_EXPERTS
    offs_ref[...] = jnp.zeros(offs_ref.shape, jnp.int32)
    for s in range(nb):
        thr = thr_all[s]
        n_tie = n_tie_all[s]
        tie_seen = jnp.zeros((1, LANES), F32)
        sel_seen = jnp.zeros((1, LANES), F32)
        for k in range(nck):
            kk = aff_ref[s, k * c:(k + 1) * c, :]
            gt = kk > thr
            eq = kk == thr
            eqf = jnp.where(eq, 1.0, 0.0)
            tie_rank = _dot(tri, eqf.astype(BF16)) + tie_seen
            sel = (gt | (eq & (tie_rank < n_tie))) & lane_ok
            self_ = jnp.where(sel, 1.0, 0.0)
            slot = _dot(tri, self_.astype(BF16)) + sel_seen
            pos = jnp.where(sel, slot, -1.0)
            pos_ref[s, k * c:(k + 1) * c, :] = pos
            post_ref[s, :, k * c:(k + 1) * c] = pos.T[0:N_EXPERTS, :]
            offs_ref[s, k:k + 1, :] = sel_seen[:, 0:N_EXPERTS].astype(jnp.int32)
            tie_seen = tie_seen + jnp.sum(eqf, axis=0, keepdims=True)
            sel_seen = sel_seen + jnp.sum(self_, axis=0, keepdims=True)
        offs_ref[s, nck:nck + 1, :] = sel_seen[:, 0:N_EXPERTS].astype(jnp.int32)


def _offs_rows(n):
    return -(-(n // TOK_TILE + 1) // 8) * 8


def _route_call(aff, cap):
    nb, n, _ = aff.shape
    rows = _offs_rows(n)
    return pl.pallas_call(
        functools.partial(_route_kernel, cap=cap),
        grid=(1,),
        in_specs=[pl.BlockSpec((nb, n, LANES), lambda i: (0, 0, 0))],
        out_specs=[pl.BlockSpec((nb, n, LANES), lambda i: (0, 0, 0)),
                   pl.BlockSpec((nb, N_EXPERTS, n), lambda i: (0, 0, 0)),
                   pl.BlockSpec((nb, rows, N_EXPERTS), lambda i: (0, 0, 0))],
        out_shape=[jax.ShapeDtypeStruct((nb, n, LANES), F32),
                   jax.ShapeDtypeStruct((nb, N_EXPERTS, n), F32),
                   jax.ShapeDtypeStruct((nb, rows, N_EXPERTS), jnp.int32)],
        compiler_params=_cp(32), name="ec_route",
    )(aff, )


def _window(offs_ref, b, k, e, rows, win):
    off = offs_ref[(b * rows + k) * N_EXPERTS + e]
    nxt = offs_ref[(b * rows + k + 1) * N_EXPERTS + e]
    base = lax.shift_right_logical(off, SLOT_ALIGN.bit_length() - 1) * SLOT_ALIGN
    nwin = lax.shift_right_logical(nxt - base + (win - 1), win.bit_length() - 1)
    return base, nwin


def _slot_targets(ids, start, cap, win):
    cb = pl.multiple_of(jnp.minimum(start, cap - win), SLOT_ALIGN)
    slot = ids + cb.astype(F32)
    return jnp.where(slot >= start.astype(F32), slot, -2.0), cb


def _gather_kernel(offs_ref, post_ref, h_ref, asp_ref, xs_ref, gs_ref, *, cap, win, rows):
    nseq, tm = h_ref.shape[0], h_ref.shape[1]
    c = TOK_TILE
    b0 = pl.program_id(0) * nseq
    k0 = pl.program_id(1) * (tm // c)

    @pl.when(pl.program_id(1) == 0)
    def _():
        xs_ref[...] = jnp.zeros(xs_ref.shape, xs_ref.dtype)
        gs_ref[...] = jnp.zeros(gs_ref.shape, gs_ref.dtype)

    sub = lax.broadcasted_iota(jnp.int32, (win, 1), 0).astype(F32)

    def one_chunk(s, cc, overflow):
        b = b0 + s
        k = k0 + cc
        cols = slice(cc * c, (cc + 1) * c)
        hk = h_ref[s, cols, :]
        gk = asp_ref[s, cols, :]

        def sel_rows(e, start):
            tgt, cb = _slot_targets(sub, start, cap, win)
            return jnp.where(post_ref[s, e:e + 1, cols] == tgt, 1.0, 0.0).astype(BF16), s * cap + cb

        if overflow:
            for e in range(N_EXPERTS):
                base, nwin = _window(offs_ref, b, k, e, rows, win)

                def extra(w, carry, e=e, base=base):
                    p, cbw = sel_rows(e, base + w * win)
                    xs_ref[e, pl.ds(cbw, win), :] += _dot(p, hk).astype(BF16)
                    gs_ref[e, pl.ds(cbw, win), :] += _dot(p, gk)
                    return carry

                lax.fori_loop(1, nwin, extra, 0)
            return

        for g0 in range(0, N_EXPERTS, GATHER_GROUP):
            parts, cbs = [], []
            for e in range(g0, g0 + GATHER_GROUP):
                base, _ = _window(offs_ref, b, k, e, rows, win)
                p, cb = sel_rows(e, base)
                parts.append(p)
                cbs.append(cb)
            p_grp = jnp.concatenate(parts, axis=0)
            r_grp = _dot(p_grp, hk).astype(BF16)
            g_grp = _dot(p_grp, gk)
            for j, cb in enumerate(cbs):
                xs_ref[g0 + j, pl.ds(cb, win), :] += r_grp[j * win:(j + 1) * win, :]
                gs_ref[g0 + j, pl.ds(cb, win), :] += g_grp[j * win:(j + 1) * win, :]

    for overflow in (False, True):
        for s in range(nseq):
            for cc in range(tm // c):
                one_chunk(s, cc, overflow)


def _gather_call(offs, post, h, asp, cap, win):
    nb, n, d = h.shape
    sb, tm = _seq_tiling(n, 0, EC_ROWS)
    rows = _offs_rows(n)
    grid_spec = pltpu.PrefetchScalarGridSpec(
        num_scalar_prefetch=1, grid=(nb // sb, n // tm),
        in_specs=[pl.BlockSpec((sb, N_EXPERTS, tm), lambda b, k, o: (b, 0, k)),
                  pl.BlockSpec((sb, tm, d), lambda b, k, o: (b, k, 0)),
                  pl.BlockSpec((sb, tm, LANES), lambda b, k, o: (b, k, 0))],
        out_specs=[pl.BlockSpec((N_EXPERTS, sb * cap, d), lambda b, k, o: (0, b, 0)),
                   pl.BlockSpec((N_EXPERTS, sb * cap, LANES), lambda b, k, o: (0, b, 0))])
    return pl.pallas_call(
        functools.partial(_gather_kernel, cap=cap, win=win, rows=rows),
        grid_spec=grid_spec,
        out_shape=[jax.ShapeDtypeStruct((N_EXPERTS, nb * cap, d), BF16),
                   jax.ShapeDtypeStruct((N_EXPERTS, nb * cap, LANES), F32)],
        compiler_params=_cp(48), name="ec_gather",
    )(offs.reshape(-1), post, h, asp)


def _ffn_kernel(xa_ref, xb_ref, ga_ref, gb_ref, w1_ref, w3_ref, w2_ref, ya_ref, yb_ref,
                acca_ref, accb_ref):
    e = pl.program_id(0)
    f = pl.program_id(1)

    @pl.when((e == 0) & (f == 0))
    def _():
        acca_ref[...] = jnp.zeros(acca_ref.shape, F32)
        accb_ref[...] = jnp.zeros(accb_ref.shape, F32)

    w1 = w1_ref[...].astype(BF16)
    w3 = w3_ref[...].astype(BF16)
    w2 = w2_ref[...].astype(BF16)
    groups = ((xa_ref, ga_ref, ya_ref, acca_ref), (xb_ref, gb_ref, yb_ref, accb_ref))
    hids = [(jax.nn.silu(_dot(x_ref[...], w1)) * _dot(x_ref[...], w3)).astype(BF16)
            for x_ref, _, _, _ in groups]
    for (x_ref, g_ref, y_ref, acc_ref), hid in zip(groups, hids):
        acc = jnp.where(f > 0, acc_ref[...], 0.0) + _dot(hid, w2)
        acc_ref[...] = acc
        g = g_ref[...]
        lane = lax.broadcasted_iota(jnp.int32, g.shape, 1)
        keep = ((lane % N_EXPERTS) == e) & (lane < 3 * N_EXPERTS)
        gate = jnp.sum(jnp.where(keep, g, 0.0), axis=-1, keepdims=True)
        y_ref[...] = (acc * gate).astype(BF16)


def _ffn_call(xa, xb, ga, gb, w1, w3, w2, layer):
    ne, ra, d = xa.shape
    rb = xb.shape[1]
    ff = w1.shape[3]
    tf = FF_TILE
    rowsp = lambda r, c: pl.BlockSpec((None, r, c), lambda e, f: (e, 0, 0))
    return pl.pallas_call(
        _ffn_kernel,
        grid=(ne, ff // tf),
        in_specs=[rowsp(ra, d), rowsp(rb, d), rowsp(ra, LANES), rowsp(rb, LANES),
                  pl.BlockSpec((None, None, d, tf), lambda e, f: (layer, e, 0, f)),
                  pl.BlockSpec((None, None, d, tf), lambda e, f: (layer, e, 0, f)),
                  pl.BlockSpec((None, None, tf, d), lambda e, f: (layer, e, f, 0))],
        out_specs=[rowsp(ra, d), rowsp(rb, d)],
        out_shape=[jax.ShapeDtypeStruct((ne, ra, d), BF16), jax.ShapeDtypeStruct((ne, rb, d), BF16)],
        scratch_shapes=[pltpu.VMEM((ra, d), F32), pltpu.VMEM((rb, d), F32)],
        compiler_params=_cp(56), name="expert_swiglu",
    )(xa, xb, ga, gb, w1, w3, w2)


def _combine_kernel(offs_ref, pos_ref, y_ref, x_ref, g2_ref, ex_ref, *rest, cap, win, rows, last):
    if last:
        fg_ref, o_ref, on_ref = rest
    else:
        (o_ref,) = rest
    nseq, tm = x_ref.shape[0], x_ref.shape[1]
    c = TOK_TILE
    b0 = pl.program_id(0) * nseq
    k0 = pl.program_id(1) * (tm // c)
    per = MXU_DEPTH // win
    depth = per * win
    lane = lax.broadcasted_iota(jnp.int32, (1, depth), 1)
    lane_grp = lane // win
    lane_in = (lane % win).astype(F32)
    lane1 = lax.broadcasted_iota(jnp.int32, (1, win), 1).astype(F32)

    def one_chunk(s, cc, overflow, pos_wide=None):
        b = b0 + s
        k = k0 + cc
        rws = (s, slice(cc * c, (cc + 1) * c))
        if overflow:
            for e in range(N_EXPERTS):
                base, nwin = _window(offs_ref, b, k, e, rows, win)

                def extra(w, carry, e=e, base=base):
                    tgt_w, cbw = _slot_targets(lane1, base + w * win, cap, win)
                    pt = jnp.where(pos_ref[rws][:, e:e + 1] == tgt_w, 1.0, 0.0).astype(BF16)
                    o_ref[rws] += g2_ref[...] * _dot(pt, y_ref[e, pl.ds(s * cap + cbw, win), :])
                    return carry

                lax.fori_loop(1, nwin, extra, 0)
            if last:
                on_ref[rws] = _rms(o_ref[rws]) * fg_ref[...]
            return
        tot = None
        for gi, g0 in enumerate(range(0, N_EXPERTS, per)):
            tgt = jnp.full((1, depth), -2.0, F32)
            ys = []
            for j in range(per):
                base, _ = _window(offs_ref, b, k, g0 + j, rows, win)
                tgt_e, cb = _slot_targets(lane_in, base, cap, win)
                tgt = jnp.where(lane_grp == j, tgt_e, tgt)
                ys.append(y_ref[g0 + j, pl.ds(s * cap + cb, win), :])
            pt = jnp.where(pos_wide[:, gi * depth:(gi + 1) * depth] == tgt, 1.0, 0.0).astype(BF16)
            d = _dot(pt, jnp.concatenate(ys, axis=0))
            tot = d if tot is None else tot + d
        o_ref[rws] = x_ref[rws] + g2_ref[...] * tot

    chunks = [(s, cc) for s in range(nseq) for cc in range(tm // c)]
    wides = [_dot(pos_ref[s, cc * c:(cc + 1) * c, :].astype(BF16), ex_ref[...]) for s, cc in chunks]
    for (s, cc), pos_wide in zip(chunks, wides):
        one_chunk(s, cc, False, pos_wide)
    for s, cc in chunks:
        one_chunk(s, cc, True)


def _combine_call(offs, pos, y, x1, modl, row0, row_step, final_g, cap, win, layer):
    nb, n, d = x1.shape
    sb, tm = _seq_tiling(n, row_step, EC_ROWS)
    rows = _offs_rows(n)
    last = final_g is not None
    assert cap <= 256
    spread = np.zeros((LANES, N_EXPERTS * win), np.float32)
    for e in range(N_EXPERTS):
        spread[e, e * win:(e + 1) * win] = 1.0
    spread = jnp.asarray(spread, BF16)
    tok = pl.BlockSpec((sb, tm, d), lambda b, k, o: (b, k, 0))
    in_specs = [pl.BlockSpec((sb, tm, LANES), lambda b, k, o: (b, k, 0)),
                pl.BlockSpec((N_EXPERTS, sb * cap, d), lambda b, k, o: (0, b, 0)),
                tok,
                pl.BlockSpec((None, None, None, 1, d),
                             lambda b, k, o: (layer, row0 + b * sb * row_step, 5, 0, 0)),
                pl.BlockSpec(spread.shape, lambda b, k, o: (0, 0))]
    ins = [offs.reshape(-1), pos, y, x1, modl, spread]
    if last:
        in_specs.append(pl.BlockSpec((1, d), lambda b, k, o: (0, 0)))
        ins.append(final_g)
    nout = 2 if last else 1
    grid_spec = pltpu.PrefetchScalarGridSpec(
        num_scalar_prefetch=1, grid=(nb // sb, n // tm), in_specs=in_specs, out_specs=[tok] * nout)
    return pl.pallas_call(
        functools.partial(_combine_kernel, cap=cap, win=win, rows=rows, last=last),
        grid_spec=grid_spec,
        out_shape=[jax.ShapeDtypeStruct((nb, n, d), F32)] * nout,
        compiler_params=_cp(56), name="ec_combine_final" if last else "ec_combine",
    )(*ins)


def _prep_weights(w_in, w_s, b_s, w_pool, pool_scale, q_norm_g, w_qb, kv_norm_g, w_kvb, w_out,
                  w_router, norm1_g, norm2_g):
    depth, d, cols = w_in.shape
    row = lambda a: a.reshape(depth, 1, -1)
    w_in_p = jnp.pad(w_in, ((0, 0), (0, 0), (0, PROJ_PAD - cols))).astype(BF16)
    wq = w_qb.reshape(depth, Q_LORA, C_HEADS, QK_NOPE + QK_ROPE)
    wq = jnp.pad(wq, ((0, 0), (0, 0), (0, 0), (0, HEAD_PAD - QK_NOPE - QK_ROPE)))
    wq_t = wq.reshape(depth, Q_LORA, C_HEADS * HEAD_PAD).transpose(0, 2, 1).astype(BF16)
    wkv = w_kvb.reshape(depth, KV_LORA, C_HEADS, QK_NOPE + V_DIM)
    wkn = jnp.pad(wkv[..., :QK_NOPE], ((0, 0), (0, 0), (0, 0), (0, HEAD_PAD - QK_NOPE)))
    wkn = wkn.reshape(depth, KV_LORA, C_HEADS * HEAD_PAD).astype(BF16)
    wv_t = wkv[..., QK_NOPE:].reshape(depth, KV_LORA, C_WIDTH).transpose(0, 2, 1).astype(BF16)
    assert A_HEADS == len(POOL_WINDOWS) and A_HEAD_DIM == POOL_GROUP
    ws_cat = w_s.transpose(0, 2, 1, 3).reshape(depth, CHUNK, A_HEADS * CHUNK).astype(BF16)
    b_rep = jnp.repeat(b_s.transpose(0, 2, 1), A_HEAD_DIM, axis=2)
    eye = jnp.eye(len(POOL_WINDOWS), dtype=F32)
    wp_bd = jnp.einsum("gh,lgcd->lgchd", eye, w_pool).reshape(depth, B_WIDTH, B_WIDTH).astype(BF16)
    wr_hi = w_router.astype(BF16)
    wr_lo = (w_router - wr_hi.astype(F32)).astype(BF16)
    wr = jnp.pad(jnp.concatenate([wr_hi, wr_lo], axis=2), ((0, 0), (0, 0), (0, LANES - 2 * N_EXPERTS)))
    return dict(
        norm1_g=row(norm1_g), norm2_g=row(norm2_g), w_in=w_in_p, q_norm_g=row(q_norm_g),
        kv_norm_g=row(kv_norm_g), wq_t=wq_t, wkn=wkn, wv_t=wv_t, w_s=ws_cat, b_s=b_rep, w_pool=wp_bd,
        pool_scale=row(pool_scale), w_out=w_out.astype(BF16), w_router=wr)


def _rope_tables(n):
    rows = n // GRID_W
    row = jnp.repeat(jnp.arange(rows), GRID_W).astype(F32)
    col = jnp.broadcast_to(jnp.arange(GRID_W), (rows, GRID_W)).reshape(-1).astype(F32)
    inv = 1.0 / (ROPE_THETA ** (jnp.arange(0, ROPE_AXIS, 2, dtype=F32) / ROPE_AXIS))
    ang_r = row[:, None] * inv[None, :]
    ang_c = col[:, None] * inv[None, :]
    cos_r, sin_r, cos_c, sin_c = jnp.cos(ang_r), jnp.sin(ang_r), jnp.cos(ang_c), jnp.sin(ang_c)
    pad = jnp.zeros((n, LANES - QK_ROPE), F32)
    cos_k = jnp.concatenate([cos_r, cos_r, cos_c, cos_c, pad], axis=1)
    sin_k = jnp.concatenate([-sin_r, sin_r, -sin_c, sin_c, pad], axis=1)
    return cos_k, sin_k, cos_r.T, sin_r.T, cos_c.T, sin_c.T


def kernel(x_prompt, x_sample, cache_ckv, cache_krope, c, c_ctx, w_ada, b_ada, norm1_g, norm2_g, w_in, w_s, b_s, w_pool, pool_scale, q_norm_g, w_qb, kv_norm_g, w_kvb, w_out, w_router, w_e1, w_e3, w_e2, final_norm_g):
    depth, d, _ = w_ada.shape
    nb_c, n_c, _ = x_prompt.shape
    nb_s, n_s, _ = x_sample.shape
    assert 1 + nb_s <= 8
    cond8 = jnp.concatenate([c_ctx[None, :], c, jnp.zeros((8 - 1 - nb_s, d), F32)], axis=0)
    mod = _mod_call(cond8, w_ada, b_ada).reshape(depth, 8, 6, 1, d)
    lw = _prep_weights(w_in, w_s, b_s, w_pool, pool_scale, q_norm_g, w_qb, kv_norm_g, w_kvb, w_out,
                       w_router, norm1_g, norm2_g)
    k_cache, vt_cache = _kvexp_call(cache_ckv, cache_krope, lw["wkn"], lw["wv_t"])
    tabs = _rope_tables(n_s)
    band = _pool_bands()
    final_g = final_norm_g.reshape(1, d)
    cap_c = EC_CAPACITY * n_c // N_EXPERTS
    cap_s = EC_CAPACITY * n_s // N_EXPERTS
    win_c = min(cap_c, 64)
    win_s = min(cap_s, 64)

    xc, xs = x_prompt, x_sample
    yc = ys = None
    ckvs, kropes = [], []
    for l in range(depth):
        uc, vac, pinc, qtc, khc, vtc, ckv_c, kr_c = _proj_call(xc, mod, 0, 0, lw, None, l)
        us, vas, pins, qts, khs, vts, _, _ = _proj_call(xs, mod, 1, 1, lw, tabs, l)
        ckvs.append(ckv_c)
        kropes.append(kr_c)
        oac, obc = _mix_call(uc, vac, pinc, lw, band, l)
        oas, obs = _mix_call(us, vas, pins, lw, band, l)
        occ = _attn_call(qtc, [(khc, vtc, None)])
        ocs = _attn_call(qts, [(k_cache, vt_cache, l), (khs, vts, None)])
        x1c, hc, affc, aspc = _outproj_call(xc, oac, obc, occ, mod, 0, 0, lw, l)
        x1s, hs, affs, asps = _outproj_call(xs, oas, obs, ocs, mod, 1, 1, lw, l)
        posc, postc, offc = _route_call(affc, cap_c)
        poss, posts, offs = _route_call(affs, cap_s)
        xgc, ggc = _gather_call(offc, postc, hc, aspc, cap_c, win_c)
        xgs, ggs = _gather_call(offs, posts, hs, asps, cap_s, win_s)
        ygs, ygc = _ffn_call(xgs, xgc, ggs, ggc, w_e1, w_e3, w_e2, l)
        fg = final_g if l == depth - 1 else None
        outc = _combine_call(offc, posc, ygc, x1c, mod, 0, 0, fg, cap_c, win_c, l)
        outs = _combine_call(offs, poss, ygs, x1s, mod, 1, 1, fg, cap_s, win_s, l)
        xc, xs = outc[0], outs[0]
    yc, ys = outc[1], outs[1]
    new_ckv = jnp.stack(ckvs, axis=1)
    new_krope = jnp.stack(kropes, axis=1)
    return (yc, ys, new_ckv, new_krope)
```

```python
import functools
import math

import numpy as np
import jax
import jax.numpy as jnp
from jax import lax
from jax.experimental import pallas as pl
from jax.experimental.pallas import tpu as pltpu

F32 = jnp.float32
BF16 = jnp.bfloat16

GRID_W = 64
A_HEADS = 4
A_HEAD_DIM = 64
A_WIDTH = A_HEADS * A_HEAD_DIM
CHUNK = 128
POOL_WINDOWS = (2, 4, 8, 16)
POOL_GROUP = 64
B_WIDTH = len(POOL_WINDOWS) * POOL_GROUP
C_HEADS = 8
QK_NOPE = 64
QK_ROPE = 32
V_DIM = 64
Q_LORA = 384
KV_LORA = 256
C_WIDTH = C_HEADS * V_DIM
ROPE_AXIS = QK_ROPE // 2
ROPE_THETA = 10000.0
ATTN_SCALE = (QK_NOPE + QK_ROPE) ** -0.5
N_EXPERTS = 16
EC_CAPACITY = 2
NORM_EPS = 1e-6

LANES = 128
HEAD_PAD = 128
PROJ_PAD = 1536
OFF_P = 2 * A_WIDTH
OFF_CQ = OFF_P + B_WIDTH
OFF_CKV = OFF_CQ + Q_LORA
OFF_KR = OFF_CKV + KV_LORA
MIX_UNROLL = 8
POOL_HALO = 8
TOK_TILE = 256
ATTN_TQ = 256
ROW_TILE = 1024
PROJ_SUB = 512
EC_ROWS = 1024
SUB_ROWS = 256
V_ROWS = V_DIM + 16
KEY_CHUNK = 512
SCORE_BUFFERS = 2
ATTN_Q_BLOCKS = 2
Q_PRESCALE = ATTN_SCALE * math.log2(math.e)
FF_TILE = 512
MOD_TILE = 1536
SEARCH_BITS = 31
SLOT_ALIGN = 16
GATHER_GROUP = 4
MXU_DEPTH = 256


def _cp(vmem_mb):
    return pltpu.CompilerParams(vmem_limit_bytes=vmem_mb * 1024 * 1024)


def _layer_spec(a, layer):
    return pl.BlockSpec((None,) + a.shape[1:], lambda *_: (layer,) + (0,) * (a.ndim - 1))


def _dot(a, b):
    return jnp.dot(a, b, preferred_element_type=F32)


def _dot_nt(a, b):
    return lax.dot_general(a, b, (((1,), (1,)), ((), ())), preferred_element_type=F32)


def _split2(x):
    hi = x.astype(BF16)
    lo = (x - hi.astype(F32)).astype(BF16)
    return hi, lo


def _rms(x):
    return x * lax.rsqrt(jnp.mean(x * x, axis=-1, keepdims=True) + NORM_EPS)


def _load_rows(ref):
    return jnp.concatenate([ref[s] for s in range(ref.shape[0])], axis=0)


def _store_rows(ref, val):
    tm = ref.shape[1]
    for s in range(ref.shape[0]):
        ref[s] = val[s * tm:(s + 1) * tm, :]


def _with_ones_row(vt):
    rows = lax.broadcasted_iota(jnp.int32, (V_ROWS - V_DIM, vt.shape[1]), 0)
    return jnp.concatenate([vt, jnp.where(rows == 0, 1.0, 0.0).astype(BF16)], axis=0)


def _mod_kernel(c_ref, w_ref, b_ref, o_ref):
    a = jax.nn.silu(c_ref[...])
    a_hi, a_lo = _split2(a)
    w_hi, w_lo = _split2(w_ref[...])
    o_ref[...] = _dot(a_hi, w_hi) + _dot(a_lo, w_hi) + _dot(a_hi, w_lo) + b_ref[...]


def _mod_call(cond8, w_ada, b_ada):
    depth, d, n = w_ada.shape
    return pl.pallas_call(
        _mod_kernel,
        grid=(depth, n // MOD_TILE),
        in_specs=[pl.BlockSpec((8, d), lambda l, j: (0, 0)),
                  pl.BlockSpec((None, d, MOD_TILE), lambda l, j: (l, 0, j)),
                  pl.BlockSpec((None, 1, MOD_TILE), lambda l, j: (l, 0, j))],
        out_specs=pl.BlockSpec((None, 8, MOD_TILE), lambda l, j: (l, 0, j)),
        out_shape=jax.ShapeDtypeStruct((depth, 8, n), F32),
        compiler_params=_cp(40),
        name="adaln_mod",
    )(cond8, w_ada, b_ada.reshape(depth, 1, n))


def _proj_kernel(*refs, rope):
    if rope:
        (x_ref, sh_ref, sc_ref, g_ref, win_ref, qg_ref, kvg_ref, wq_ref, wkn_ref, wvt_ref,
         ck_ref, sk_ref, cr_ref, sr_ref, cc_ref, scl_ref,
         u_ref, va_ref, pin_ref, qt_ref, kh_ref, vt_ref) = refs
    else:
        (x_ref, sh_ref, sc_ref, g_ref, win_ref, qg_ref, kvg_ref, wq_ref, wkn_ref, wvt_ref,
         u_ref, va_ref, pin_ref, qt_ref, kh_ref, vt_ref, ckv_ref, kr_ref) = refs
    nseq, tm = x_ref.shape[0], x_ref.shape[1]
    ps = min(tm, PROJ_SUB)
    blocks = [(s, t0) for s in range(nseq) for t0 in range(0, tm, ps)]

    projs = []
    for s, t0 in blocks:
        x = x_ref[s, t0:t0 + ps, :]
        h = _rms(x) * g_ref[...] * (1.0 + sc_ref[...]) + sh_ref[...]
        projs.append(_dot(h.astype(BF16), win_ref[...]))

    for (s, t0), proj in zip(blocks, projs):
        rows = (s, slice(t0, t0 + ps))
        u_ref[rows] = jax.nn.gelu(proj[:, 0:A_WIDTH])
        va_ref[rows] = jax.nn.gelu(proj[:, A_WIDTH:OFF_P]).astype(BF16)
        pin_ref[rows] = proj[:, OFF_P:OFF_CQ]
        cq = _rms(proj[:, OFF_CQ:OFF_CKV]) * qg_ref[...]
        ckv = _rms(proj[:, OFF_CKV:OFF_KR]) * kvg_ref[...]
        kr = proj[:, OFF_KR:OFF_KR + LANES]
        if not rope:
            ckv_ref[rows] = ckv
            kr_ref[rows] = kr[:, 0:QK_ROPE]

        qt = _dot_nt(wq_ref[...], cq.astype(BF16))
        ckv_b = ckv.astype(BF16)
        kn = _dot(ckv_b, wkn_ref[...])
        vt = _dot_nt(wvt_ref[...], ckv_b)

        if rope:
            lane = lax.broadcasted_iota(jnp.int32, kr.shape, 1)
            half = ROPE_AXIS // 2
            swapped = jnp.where((lane & half) == 0,
                                pltpu.roll(kr, LANES - half, axis=1), pltpu.roll(kr, half, axis=1))
            kr = kr * ck_ref[t0:t0 + ps, :] + swapped * sk_ref[t0:t0 + ps, :]
            cr, sr = cr_ref[:, t0:t0 + ps], sr_ref[:, t0:t0 + ps]
            cc, scl = cc_ref[:, t0:t0 + ps], scl_ref[:, t0:t0 + ps]
        kr_shift = pltpu.roll(kr, QK_NOPE, axis=1)

        for hd in range(C_HEADS):
            r0 = hd * HEAD_PAD
            q_h = qt[r0:r0 + HEAD_PAD, :]
            if rope:
                b0 = QK_NOPE
                a, b = q_h[b0:b0 + 8, :], q_h[b0 + 8:b0 + 16, :]
                c, d = q_h[b0 + 16:b0 + 24, :], q_h[b0 + 24:b0 + 32, :]
                q_h = jnp.concatenate(
                    [q_h[0:b0, :], a * cr - b * sr, a * sr + b * cr, c * cc - d * scl, c * scl + d * cc,
                     q_h[b0 + 32:, :]], axis=0)
            q_h = (q_h * Q_PRESCALE).astype(BF16)
            qw = qt_ref.shape[-1]
            for j in range(ps // qw):
                qt_ref[s, t0 // qw + j, hd] = q_h[:, j * qw:(j + 1) * qw]
            kh_ref[s, hd, t0:t0 + ps, :] = (kn[:, r0:r0 + HEAD_PAD] + kr_shift).astype(BF16)
            vt_ref[s, hd, :, t0:t0 + ps] = _with_ones_row(vt[hd * V_DIM:(hd + 1) * V_DIM, :].astype(BF16))


def _seq_tiling(n, row_step, rows=None):
    rows = ROW_TILE if rows is None else rows
    tm = min(n, rows)
    return (rows // tm if row_step == 0 else 1), tm


def _proj_call(x, modl, row0, row_step, lw, tabs, layer):
    nb, n, d = x.shape
    sb, tm = _seq_tiling(n, row_step)
    rope = tabs is not None
    assert not rope or sb == 1
    qw = min(n, ATTN_TQ)
    tok = lambda c: pl.BlockSpec((sb, tm, c), lambda b, i: (b, i, 0))
    const = lambda a: _layer_spec(a, layer)
    modspec = lambda j: pl.BlockSpec((None, None, None, 1, d),
                                     lambda b, i: (layer, row0 + b * sb * row_step, j, 0, 0))
    ins = [x, modl, modl, lw["norm1_g"], lw["w_in"], lw["q_norm_g"], lw["kv_norm_g"],
           lw["wq_t"], lw["wkn"], lw["wv_t"]]
    in_specs = [tok(d), modspec(0), modspec(1), const(lw["norm1_g"]), const(lw["w_in"]),
                const(lw["q_norm_g"]), const(lw["kv_norm_g"]), const(lw["wq_t"]), const(lw["wkn"]),
                const(lw["wv_t"])]
    if rope:
        cos_k, sin_k, cos_r, sin_r, cos_c, sin_c = tabs
        ins += [cos_k, sin_k, cos_r, sin_r, cos_c, sin_c]
        in_specs += [pl.BlockSpec((tm, LANES), lambda b, i: (i, 0))] * 2
        in_specs += [pl.BlockSpec((8, tm), lambda b, i: (0, i))] * 4
    out_shape = [jax.ShapeDtypeStruct((nb, n, A_WIDTH), F32),
                 jax.ShapeDtypeStruct((nb, n, A_WIDTH), BF16),
                 jax.ShapeDtypeStruct((nb, n, B_WIDTH), F32),
                 jax.ShapeDtypeStruct((nb, n // qw, C_HEADS, HEAD_PAD, qw), BF16),
                 jax.ShapeDtypeStruct((nb, C_HEADS, n, HEAD_PAD), BF16),
                 jax.ShapeDtypeStruct((nb, C_HEADS, V_ROWS,n), BF16),
                 jax.ShapeDtypeStruct((nb, n, KV_LORA), F32),
                 jax.ShapeDtypeStruct((nb, n, QK_ROPE), F32)]
    out_specs = [tok(A_WIDTH), tok(A_WIDTH), tok(B_WIDTH),
                 pl.BlockSpec((sb, tm // qw, C_HEADS, HEAD_PAD, qw), lambda b, i: (b, i, 0, 0, 0)),
                 pl.BlockSpec((sb, C_HEADS, tm, HEAD_PAD), lambda b, i: (b, 0, i, 0)),
                 pl.BlockSpec((sb, C_HEADS, V_ROWS, tm), lambda b, i: (b, 0, 0, i)),
                 tok(KV_LORA), tok(QK_ROPE)]
    if rope:
        out_shape, out_specs = out_shape[:-2], out_specs[:-2]
    return pl.pallas_call(
        functools.partial(_proj_kernel, rope=rope),
        grid=(nb // sb, n // tm), in_specs=in_specs, out_specs=out_specs, out_shape=out_shape,
        compiler_params=_cp(48), name="in_proj_rope" if rope else "in_proj",
    )(*ins)


def _kvexp_kernel(ckv_ref, kr_ref, wkn_ref, wvt_ref, place_ref, kh_ref, vt_ref):
    ckv_b = ckv_ref[...].astype(BF16)
    kn = _dot(ckv_b, wkn_ref[...])
    vt = _dot_nt(wvt_ref[...], ckv_b)
    kr_placed = _dot(kr_ref[...].astype(BF16), place_ref[...])
    for hd in range(C_HEADS):
        r0 = hd * HEAD_PAD
        kh_ref[hd] = (kn[:, r0:r0 + HEAD_PAD] + kr_placed).astype(BF16)
        vt_ref[hd] = _with_ones_row(vt[hd * V_DIM:(hd + 1) * V_DIM, :].astype(BF16))


def _kvexp_call(cache_ckv, cache_krope, wkn, wv_t):
    nb, depth, m, r = cache_ckv.shape
    place = np.zeros((QK_ROPE, HEAD_PAD), np.float32)
    place[np.arange(QK_ROPE), QK_NOPE + np.arange(QK_ROPE)] = 1.0
    place = jnp.asarray(place, BF16)
    return pl.pallas_call(
        _kvexp_kernel,
        grid=(depth, nb),
        in_specs=[pl.BlockSpec((None, None, m, r), lambda l, b: (b, l, 0, 0)),
                  pl.BlockSpec((None, None, m, QK_ROPE), lambda l, b: (b, l, 0, 0)),
                  pl.BlockSpec((None,) + wkn.shape[1:], lambda l, b: (l, 0, 0)),
                  pl.BlockSpec((None,) + wv_t.shape[1:], lambda l, b: (l, 0, 0)),
                  pl.BlockSpec(place.shape, lambda l, b: (0, 0))],
        out_specs=[pl.BlockSpec((None, None, C_HEADS, m, HEAD_PAD), lambda l, b: (l, b, 0, 0, 0)),
                   pl.BlockSpec((None, None, C_HEADS, V_ROWS,m), lambda l, b: (l, b, 0, 0, 0))],
        out_shape=[jax.ShapeDtypeStruct((depth, nb, C_HEADS, m, HEAD_PAD), BF16),
                   jax.ShapeDtypeStruct((depth, nb, C_HEADS, V_ROWS,m), BF16)],
        compiler_params=_cp(32), name="cache_kv_expand",
    )(cache_ckv, cache_krope, wkn, wv_t, place)


def _mix_kernel(u_ref, va_ref, pin_ref, ws_ref, bs_ref, band_ref, wp_ref, ps_ref,
                oa_ref, ob_ref, pad_ref):
    n = u_ref.shape[0]
    width = u_ref.shape[1]
    zero_rows = jnp.zeros((POOL_HALO, width), F32)
    pad_ref[0:POOL_HALO, :] = zero_rows
    pad_ref[n + POOL_HALO:n + 2 * POOL_HALO, :] = zero_rows
    pad_ref[POOL_HALO:n + POOL_HALO, :] = pin_ref[...]
    ngrp = len(POOL_WINDOWS)
    unroll = min(MIX_UNROLL, n // CHUNK)
    row = lax.broadcasted_iota(jnp.int32, (CHUNK, width), 0)
    grp_row = lax.broadcasted_iota(jnp.int32, (1, width), 1) // POOL_GROUP
    left = jnp.zeros((1, width), jnp.int32)
    right1 = jnp.zeros((1, width), jnp.int32)
    for g, w in enumerate(POOL_WINDOWS):
        left = jnp.where(grp_row == g, w // 2, left)
        right1 = jnp.where(grp_row == g, w - w // 2, right1)

    def group_masks(nrows):
        grp = lax.broadcasted_iota(jnp.int32, (nrows, width), 1) // POOL_GROUP
        return [jnp.where(grp == g, 1.0, 0.0).astype(BF16) for g in range(ngrp)]

    masks_v = group_masks(CHUNK)
    masks_w = group_masks(CHUNK + 2 * POOL_HALO)

    def group_stack(x, masks):
        return jnp.concatenate([x * m for m in masks], axis=0)

    def first_dots(c):
        r0 = pl.multiple_of(c * CHUNK, CHUNK)
        v_stack = group_stack(va_ref[pl.ds(r0, CHUNK), :], masks_v)
        s = _dot(ws_ref[...], v_stack) + bs_ref[...]
        oa_ref[pl.ds(r0, CHUNK), :] = (u_ref[pl.ds(r0, CHUNK), :] * s).astype(BF16)
        win = pad_ref[pl.ds(r0, CHUNK + 2 * POOL_HALO), :]
        w_hi, w_lo = _split2(win)
        tot = (_dot(band_ref[...], group_stack(w_hi, masks_w))
               + _dot(band_ref[...], group_stack(w_lo, masks_w)))
        t = r0 + row
        cnt = jnp.minimum(t + right1, n) - jnp.maximum(t - left, 0)
        p = win[POOL_HALO:POOL_HALO + CHUNK, :]
        return r0, (tot / cnt.astype(F32) - p).astype(BF16)

    def body(j, carry):
        diffs = [first_dots(j * unroll + r) for r in range(unroll)]
        for r0, diff in diffs:
            ob_ref[pl.ds(r0, CHUNK), :] = (_dot(diff, wp_ref[...]) * ps_ref[...]).astype(BF16)
        return carry

    lax.fori_loop(0, n // (CHUNK * unroll), body, 0)


def _pool_bands():
    rows = CHUNK + 2 * POOL_HALO
    band = np.zeros((len(POOL_WINDOWS), CHUNK, rows), np.float32)
    i = np.arange(CHUNK)[:, None]
    j = np.arange(rows)[None, :]
    for g, w in enumerate(POOL_WINDOWS):
        left = w // 2
        right = w - 1 - left
        band[g] = ((j >= i + POOL_HALO - left) & (j <= i + POOL_HALO + right)).astype(np.float32)
    return jnp.asarray(band.transpose(1, 0, 2).reshape(CHUNK, -1), BF16)


def _mix_call(u, va, pin, lw, band, layer):
    nb, n, w = u.shape
    seq = lambda: pl.BlockSpec((None, n, w), lambda b: (b, 0, 0))
    const = lambda a: _layer_spec(a, layer)
    return pl.pallas_call(
        _mix_kernel,
        grid=(nb,),
        in_specs=[seq(), seq(), seq(), const(lw["w_s"]), const(lw["b_s"]),
                  pl.BlockSpec(band.shape, lambda b: (0, 0)),
                  const(lw["w_pool"]), const(lw["pool_scale"])],
        out_specs=[seq(), seq()],
        out_shape=[jax.ShapeDtypeStruct((nb, n, w), BF16)] * 2,
        scratch_shapes=[pltpu.VMEM((n + 2 * POOL_HALO, w), F32)],
        compiler_params=_cp(48), name="mixers_ab",
    )(u, va, pin, lw["w_s"], lw["b_s"], band, lw["w_pool"], lw["pool_scale"])


def _attn_kernel(*refs, seg_lens):
    nseg = len(seg_lens)
    qt_ref = refs[0]
    k_refs = refs[1:1 + nseg]
    v_refs = refs[1 + nseg:1 + 2 * nseg]
    o_ref, ot_ref = refs[1 + 2 * nseg:3 + 2 * nseg]
    s_bufs = refs[3 + 2 * nseg:]
    nsq, nqb, _, _, tq = qt_ref.shape
    nitem = nsq * nqb * C_HEADS
    head_bits = C_HEADS.bit_length() - 1
    qb_bits = nqb.bit_length() - 1
    assert C_HEADS == 1 << head_bits and nqb == 1 << qb_bits

    def unpack(it):
        blk = it >> head_bits
        return blk >> qb_bits, blk, blk & (nqb - 1), it & (C_HEADS - 1)

    def scores_into(it, s_ref):
        sq, _, qb, hd = unpack(it)
        q = qt_ref[sq, qb, hd]
        mx = None
        r0 = 0
        for k_ref, m in zip(k_refs, seg_lens):
            for c0 in range(0, m, KEY_CHUNK):
                kc = min(KEY_CHUNK, m - c0)
                s = _dot(k_ref[sq, hd, c0:c0 + kc, :], q)
                s_ref[r0:r0 + kc, :] = s
                cm = jnp.max(s, axis=0, keepdims=True)
                mx = cm if mx is None else jnp.maximum(mx, cm)
                r0 += kc
        return mx

    def finish(it, s_ref, mx):
        sq, blk, _, hd = unpack(it)
        acc = jnp.zeros((V_ROWS, tq), F32)
        r0 = 0
        for v_ref, m in zip(v_refs, seg_lens):
            for c0 in range(0, m, KEY_CHUNK):
                kc = min(KEY_CHUNK, m - c0)
                p = jnp.exp2(s_ref[r0:r0 + kc, :] - mx).astype(BF16)
                acc = acc + _dot(v_ref[sq, hd, :, c0:c0 + kc], p)
                r0 += kc
        rows = pl.ds(pl.multiple_of(hd * V_DIM, V_DIM), V_DIM)
        ot_ref[blk, rows, :] = acc[0:V_DIM, :] / acc[V_DIM:V_DIM + 1, :]

    nbuf = len(s_bufs)

    def step(t, r, mx):
        mx_next = scores_into(t + 1, s_bufs[(r + 1) % nbuf])
        finish(t, s_bufs[r], mx)
        return mx_next

    def rotation(j, mx):
        for r in range(nbuf):
            mx = step(nbuf * j + r, r, mx)
        return mx

    nsteps = nitem - 1
    mx = lax.fori_loop(0, nsteps // nbuf, rotation, scores_into(0, s_bufs[0]))
    for r in range(nsteps % nbuf):
        mx = step((nsteps // nbuf) * nbuf + r, r, mx)
    finish(nitem - 1, s_bufs[(nitem - 1) % nbuf], mx)
    for sq in range(nsq):
        for qb in range(nqb):
            o_ref[sq, qb * tq:(qb + 1) * tq, :] = ot_ref[sq * nqb + qb].T.astype(BF16)


def _attn_call(qt, segs):
    nb, nblk, _, _, tq = qt.shape
    nqb = min(nblk, ATTN_Q_BLOCKS)
    sb = ATTN_Q_BLOCKS // nqb
    seg_lens = tuple(k.shape[-2] for k, _, _ in segs)

    def seg_spec(a, layer):
        if layer is None:
            return pl.BlockSpec((sb,) + a.shape[1:], lambda b, i: (b, 0, 0, 0))
        return pl.BlockSpec((None, sb) + a.shape[2:], lambda b, i: (layer, b, 0, 0, 0))

    in_specs = [pl.BlockSpec((sb, nqb, C_HEADS, HEAD_PAD, tq), lambda b, i: (b, i, 0, 0, 0))]
    in_specs += [seg_spec(k, layer) for k, _, layer in segs]
    in_specs += [seg_spec(v, layer) for _, v, layer in segs]
    return pl.pallas_call(
        functools.partial(_attn_kernel, seg_lens=seg_lens),
        grid=(nb // sb, nblk // nqb), in_specs=in_specs,
        out_specs=pl.BlockSpec((sb, nqb * tq, C_WIDTH), lambda b, i: (b, i, 0)),
        out_shape=jax.ShapeDtypeStruct((nb, nblk * tq, C_WIDTH), BF16),
        scratch_shapes=([pltpu.VMEM((sb * nqb, C_WIDTH, tq), F32)]
                        + [pltpu.VMEM((sum(seg_lens), tq), F32)] * SCORE_BUFFERS),
        compiler_params=_cp(48), name="latent_attention",
    )(qt, *[k for k, _, _ in segs], *[v for _, v, _ in segs])


def _outproj_mix(rows, x_ref, oa_ref, ob_ref, oc_ref, wo_ref, g1_ref, sh_ref, sc_ref, ng_ref,
                 wr_ref, x1_ref, h_ref, aff_ref, asp_ref):
    mix = (_dot(oa_ref[rows], wo_ref[0:A_WIDTH, :])
           + _dot(ob_ref[rows], wo_ref[A_WIDTH:A_WIDTH + B_WIDTH, :])
           + _dot(oc_ref[rows], wo_ref[A_WIDTH + B_WIDTH:, :]))
    x1 = x_ref[rows] + g1_ref[...] * mix
    x1_ref[rows] = x1
    h = _rms(x1) * ng_ref[...] * (1.0 + sc_ref[...]) + sh_ref[...]
    h_hi, h_lo = _split2(h)
    h_ref[rows] = h_hi
    return h_hi, h_lo


def _outproj_route(rows, h_hi, h_lo, wr_ref, aff_ref, asp_ref):
    part = _dot(h_hi, wr_ref[...])
    logits = part + pltpu.roll(part, LANES - N_EXPERTS, axis=1) + _dot(h_lo, wr_ref[...])
    lane = lax.broadcasted_iota(jnp.int32, logits.shape, 1)
    logits = jnp.where(lane < N_EXPERTS, logits, -jnp.inf)
    ex = jnp.exp(logits - jnp.max(logits, axis=-1, keepdims=True))
    aff = ex / jnp.sum(ex, axis=-1, keepdims=True)
    aff_ref[rows] = aff
    a_hi = aff.astype(BF16).astype(F32)
    r1 = aff - a_hi
    a_lo = r1.astype(BF16).astype(F32)
    a_lo2 = r1 - a_lo
    packed = a_hi + pltpu.roll(a_lo, N_EXPERTS, axis=1) + pltpu.roll(a_lo2, 2 * N_EXPERTS, axis=1)
    asp_ref[rows] = packed.astype(BF16)


def _outproj_kernel(x_ref, *refs):
    nseq, tm = x_ref.shape[0], x_ref.shape[1]
    wr_ref, aff_ref, asp_ref = refs[8], refs[11], refs[12]
    blocks = [(s, slice(r0, r0 + SUB_ROWS)) for s in range(nseq) for r0 in range(0, tm, SUB_ROWS)]
    splits = [_outproj_mix(rows, x_ref, *refs) for rows in blocks]
    for rows, (h_hi, h_lo) in zip(blocks, splits):
        _outproj_route(rows, h_hi, h_lo, wr_ref, aff_ref, asp_ref)


def _outproj_call(x, oa, ob, oc, modl, row0, row_step, lw, layer):
    nb, n, d = x.shape
    sb, tm = _seq_tiling(n, row_step)
    tok = lambda c: pl.BlockSpec((sb, tm, c), lambda b, i: (b, i, 0))
    const = lambda a: _layer_spec(a, layer)
    modspec = lambda j: pl.BlockSpec((None, None, None, 1, d),
                                     lambda b, i: (layer, row0 + b * sb * row_step, j, 0, 0))
    return pl.pallas_call(
        _outproj_kernel,
        grid=(nb // sb, n // tm),
        in_specs=[tok(d), tok(A_WIDTH), tok(B_WIDTH), tok(C_WIDTH), const(lw["w_out"]),
                  modspec(2), modspec(3), modspec(4), const(lw["norm2_g"]),
                  const(lw["w_router"])],
        out_specs=[tok(d), tok(d), tok(LANES), tok(LANES)],
        out_shape=[jax.ShapeDtypeStruct((nb, n, d), F32), jax.ShapeDtypeStruct((nb, n, d), BF16),
                   jax.ShapeDtypeStruct((nb, n, LANES), F32), jax.ShapeDtypeStruct((nb, n, LANES), BF16)],
        compiler_params=_cp(48), name="out_proj_router",
    )(x, oa, ob, oc, lw["w_out"], modl, modl, modl, lw["norm2_g"], lw["w_router"])


def _route_kernel(aff_ref, pos_ref, post_ref, offs_ref, *, cap):
    nb, n, _ = aff_ref.shape
    c = TOK_TILE
    nck = n // c
    aff = aff_ref[...]
    capf = jnp.float32(cap)

    def search(i, bits):
        cand = bits | jnp.left_shift(jnp.int32(1), jnp.int32(SEARCH_BITS - 1) - i)
        cnt = jnp.sum(jnp.where(aff >= lax.bitcast_convert_type(cand, F32), 1.0, 0.0),
                      axis=1, keepdims=True)
        return jnp.where(cnt >= capf, cand, bits)

    bits = lax.fori_loop(0, SEARCH_BITS, search, jnp.zeros((nb, 1, LANES), jnp.int32))
    thr_all = lax.bitcast_convert_type(bits, F32)
    n_gt = jnp.sum(jnp.where(aff > thr_all, 1.0, 0.0), axis=1, keepdims=True)
    n_tie_all = capf - n_gt
    ri = lax.broadcasted_iota(jnp.int32, (c, c), 0)
    ci = lax.broadcasted_iota(jnp.int32, (c, c), 1)
    tri = jnp.where(ri > ci, 1.0, 0.0).astype(BF16)
    lane_ok = lax.broadcasted_iota(jnp.int32, (c, LANES), 1) < N_EXPERTS
    offs_ref[...] = jnp.zeros(offs_ref.shape, jnp.int32)
    for s in range(nb):
        thr = thr_all[s]
        n_tie = n_tie_all[s]
        tie_seen = jnp.zeros((1, LANES), F32)
        sel_seen = jnp.zeros((1, LANES), F32)
        for k in range(nck):
            kk = aff_ref[s, k * c:(k + 1) * c, :]
            gt = kk > thr
            eq = kk == thr
            eqf = jnp.where(eq, 1.0, 0.0)
            tie_rank = _dot(tri, eqf.astype(BF16)) + tie_seen
            sel = (gt | (eq & (tie_rank < n_tie))) & lane_ok
            self_ = jnp.where(sel, 1.0, 0.0)
            slot = _dot(tri, self_.astype(BF16)) + sel_seen
            pos = jnp.where(sel, slot, -1.0)
            pos_ref[s, k * c:(k + 1) * c, :] = pos
            post_ref[s, :, k * c:(k + 1) * c] = pos.T[0:N_EXPERTS, :]
            offs_ref[s, k:k + 1, :] = sel_seen[:, 0:N_EXPERTS].astype(jnp.int32)
            tie_seen = tie_seen + jnp.sum(eqf, axis=0, keepdims=True)
            sel_seen = sel_seen + jnp.sum(self_, axis=0, keepdims=True)
        offs_ref[s, nck:nck + 1, :] = sel_seen[:, 0:N_EXPERTS].astype(jnp.int32)


def _offs_rows(n):
    return -(-(n // TOK_TILE + 1) // 8) * 8


def _route_call(aff, cap):
    nb, n, _ = aff.shape
    rows = _offs_rows(n)
    return pl.pallas_call(
        functools.partial(_route_kernel, cap=cap),
        grid=(1,),
        in_specs=[pl.BlockSpec((nb, n, LANES), lambda i: (0, 0, 0))],
        out_specs=[pl.BlockSpec((nb, n, LANES), lambda i: (0, 0, 0)),
                   pl.BlockSpec((nb, N_EXPERTS, n), lambda i: (0, 0, 0)),
                   pl.BlockSpec((nb, rows, N_EXPERTS), lambda i: (0, 0, 0))],
        out_shape=[jax.ShapeDtypeStruct((nb, n, LANES), F32),
                   jax.ShapeDtypeStruct((nb, N_EXPERTS, n), F32),
                   jax.ShapeDtypeStruct((nb, rows, N_EXPERTS), jnp.int32)],
        compiler_params=_cp(32), name="ec_route",
    )(aff, )


def _window(offs_ref, b, k, e, rows, win):
    off = offs_ref[(b * rows + k) * N_EXPERTS + e]
    nxt = offs_ref[(b * rows + k + 1) * N_EXPERTS + e]
    base = lax.shift_right_logical(off, SLOT_ALIGN.bit_length() - 1) * SLOT_ALIGN
    nwin = lax.shift_right_logical(nxt - base + (win - 1), win.bit_length() - 1)
    return base, nwin


def _slot_targets(ids, start, cap, win):
    cb = pl.multiple_of(jnp.minimum(start, cap - win), SLOT_ALIGN)
    slot = ids + cb.astype(F32)
    return jnp.where(slot >= start.astype(F32), slot, -2.0), cb


def _gather_kernel(offs_ref, post_ref, h_ref, asp_ref, xs_ref, gs_ref, *, cap, win, rows):
    nseq, tm = h_ref.shape[0], h_ref.shape[1]
    c = TOK_TILE
    b0 = pl.program_id(0) * nseq
    k0 = pl.program_id(1) * (tm // c)

    @pl.when(pl.program_id(1) == 0)
    def _():
        xs_ref[...] = jnp.zeros(xs_ref.shape, xs_ref.dtype)
        gs_ref[...] = jnp.zeros(gs_ref.shape, gs_ref.dtype)

    sub = lax.broadcasted_iota(jnp.int32, (win, 1), 0).astype(F32)

    def one_chunk(s, cc, overflow):
        b = b0 + s
        k = k0 + cc
        cols = slice(cc * c, (cc + 1) * c)
        hk = h_ref[s, cols, :]
        gk = asp_ref[s, cols, :]

        def sel_rows(e, start):
            tgt, cb = _slot_targets(sub, start, cap, win)
            return jnp.where(post_ref[s, e:e + 1, cols] == tgt, 1.0, 0.0).astype(BF16), s * cap + cb

        if overflow:
            for e in range(N_EXPERTS):
                base, nwin = _window(offs_ref, b, k, e, rows, win)

                def extra(w, carry, e=e, base=base):
                    p, cbw = sel_rows(e, base + w * win)
                    xs_ref[e, pl.ds(cbw, win), :] += _dot(p, hk).astype(BF16)
                    gs_ref[e, pl.ds(cbw, win), :] += _dot(p, gk)
                    return carry

                lax.fori_loop(1, nwin, extra, 0)
            return

        for g0 in range(0, N_EXPERTS, GATHER_GROUP):
            parts, cbs = [], []
            for e in range(g0, g0 + GATHER_GROUP):
                base, _ = _window(offs_ref, b, k, e, rows, win)
                p, cb = sel_rows(e, base)
                parts.append(p)
                cbs.append(cb)
            p_grp = jnp.concatenate(parts, axis=0)
            r_grp = _dot(p_grp, hk).astype(BF16)
            g_grp = _dot(p_grp, gk)
            for j, cb in enumerate(cbs):
                xs_ref[g0 + j, pl.ds(cb, win), :] += r_grp[j * win:(j + 1) * win, :]
                gs_ref[g0 + j, pl.ds(cb, win), :] += g_grp[j * win:(j + 1) * win, :]

    for overflow in (False, True):
        for s in range(nseq):
            for cc in range(tm // c):
                one_chunk(s, cc, overflow)


def _gather_call(offs, post, h, asp, cap, win):
    nb, n, d = h.shape
    sb, tm = _seq_tiling(n, 0, EC_ROWS)
    rows = _offs_rows(n)
    grid_spec = pltpu.PrefetchScalarGridSpec(
        num_scalar_prefetch=1, grid=(nb // sb, n // tm),
        in_specs=[pl.BlockSpec((sb, N_EXPERTS, tm), lambda b, k, o: (b, 0, k)),
                  pl.BlockSpec((sb, tm, d), lambda b, k, o: (b, k, 0)),
                  pl.BlockSpec((sb, tm, LANES), lambda b, k, o: (b, k, 0))],
        out_specs=[pl.BlockSpec((N_EXPERTS, sb * cap, d), lambda b, k, o: (0, b, 0)),
                   pl.BlockSpec((N_EXPERTS, sb * cap, LANES), lambda b, k, o: (0, b, 0))])
    return pl.pallas_call(
        functools.partial(_gather_kernel, cap=cap, win=win, rows=rows),
        grid_spec=grid_spec,
        out_shape=[jax.ShapeDtypeStruct((N_EXPERTS, nb * cap, d), BF16),
                   jax.ShapeDtypeStruct((N_EXPERTS, nb * cap, LANES), F32)],
        compiler_params=_cp(48), name="ec_gather",
    )(offs.reshape(-1), post, h, asp)


def _ffn_kernel(xa_ref, xb_ref, ga_ref, gb_ref, w1_ref, w3_ref, w2_ref, ya_ref, yb_ref,
                acca_ref, accb_ref):
    e = pl.program_id(0)
    f = pl.program_id(1)

    @pl.when((e == 0) & (f == 0))
    def _():
        acca_ref[...] = jnp.zeros(acca_ref.shape, F32)
        accb_ref[...] = jnp.zeros(accb_ref.shape, F32)

    w1 = w1_ref[...].astype(BF16)
    w3 = w3_ref[...].astype(BF16)
    w2 = w2_ref[...].astype(BF16)
    groups = ((xa_ref, ga_ref, ya_ref, acca_ref), (xb_ref, gb_ref, yb_ref, accb_ref))
    hids = [(jax.nn.silu(_dot(x_ref[...], w1)) * _dot(x_ref[...], w3)).astype(BF16)
            for x_ref, _, _, _ in groups]
    for (x_ref, g_ref, y_ref, acc_ref), hid in zip(groups, hids):
        acc = jnp.where(f > 0, acc_ref[...], 0.0) + _dot(hid, w2)
        acc_ref[...] = acc
        g = g_ref[...]
        lane = lax.broadcasted_iota(jnp.int32, g.shape, 1)
        keep = ((lane % N_EXPERTS) == e) & (lane < 3 * N_EXPERTS)
        gate = jnp.sum(jnp.where(keep, g, 0.0), axis=-1, keepdims=True)
        y_ref[...] = (acc * gate).astype(BF16)


def _ffn_call(xa, xb, ga, gb, w1, w3, w2, layer):
    ne, ra, d = xa.shape
    rb = xb.shape[1]
    ff = w1.shape[3]
    tf = FF_TILE
    rowsp = lambda r, c: pl.BlockSpec((None, r, c), lambda e, f: (e, 0, 0))
    return pl.pallas_call(
        _ffn_kernel,
        grid=(ne, ff // tf),
        in_specs=[rowsp(ra, d), rowsp(rb, d), rowsp(ra, LANES), rowsp(rb, LANES),
                  pl.BlockSpec((None, None, d, tf), lambda e, f: (layer, e, 0, f)),
                  pl.BlockSpec((None, None, d, tf), lambda e, f: (layer, e, 0, f)),
                  pl.BlockSpec((None, None, tf, d), lambda e, f: (layer, e, f, 0))],
        out_specs=[rowsp(ra, d), rowsp(rb, d)],
        out_shape=[jax.ShapeDtypeStruct((ne, ra, d), BF16), jax.ShapeDtypeStruct((ne, rb, d), BF16)],
        scratch_shapes=[pltpu.VMEM((ra, d), F32), pltpu.VMEM((rb, d), F32)],
        compiler_params=_cp(56), name="expert_swiglu",
    )(xa, xb, ga, gb, w1, w3, w2)


def _combine_kernel(offs_ref, pos_ref, y_ref, x_ref, g2_ref, ex_ref, *rest, cap, win, rows, last):
    if last:
        fg_ref, o_ref, on_ref = rest
    else:
        (o_ref,) = rest
    nseq, tm = x_ref.shape[0], x_ref.shape[1]
    c = TOK_TILE
    b0 = pl.program_id(0) * nseq
    k0 = pl.program_id(1) * (tm // c)
    per = MXU_DEPTH // win
    depth = per * win
    lane = lax.broadcasted_iota(jnp.int32, (1, depth), 1)
    lane_grp = lane // win
    lane_in = (lane % win).astype(F32)
    lane1 = lax.broadcasted_iota(jnp.int32, (1, win), 1).astype(F32)

    def one_chunk(s, cc, overflow, pos_wide=None):
        b = b0 + s
        k = k0 + cc
        rws = (s, slice(cc * c, (cc + 1) * c))
        if overflow:
            for e in range(N_EXPERTS):
                base, nwin = _window(offs_ref, b, k, e, rows, win)

                def extra(w, carry, e=e, base=base):
                    tgt_w, cbw = _slot_targets(lane1, base + w * win, cap, win)
                    pt = jnp.where(pos_ref[rws][:, e:e + 1] == tgt_w, 1.0, 0.0).astype(BF16)
                    o_ref[rws] += g2_ref[...] * _dot(pt, y_ref[e, pl.ds(s * cap + cbw, win), :])
                    return carry

                lax.fori_loop(1, nwin, extra, 0)
            if last:
                on_ref[rws] = _rms(o_ref[rws]) * fg_ref[...]
            return
        tot = None
        for gi, g0 in enumerate(range(0, N_EXPERTS, per)):
            tgt = jnp.full((1, depth), -2.0, F32)
            ys = []
            for j in range(per):
                base, _ = _window(offs_ref, b, k, g0 + j, rows, win)
                tgt_e, cb = _slot_targets(lane_in, base, cap, win)
                tgt = jnp.where(lane_grp == j, tgt_e, tgt)
                ys.append(y_ref[g0 + j, pl.ds(s * cap + cb, win), :])
            pt = jnp.where(pos_wide[:, gi * depth:(gi + 1) * depth] == tgt, 1.0, 0.0).astype(BF16)
            d = _dot(pt, jnp.concatenate(ys, axis=0))
            tot = d if tot is None else tot + d
        o_ref[rws] = x_ref[rws] + g2_ref[...] * tot

    chunks = [(s, cc) for s in range(nseq) for cc in range(tm // c)]
    wides = [_dot(pos_ref[s, cc * c:(cc + 1) * c, :].astype(BF16), ex_ref[...]) for s, cc in chunks]
    for (s, cc), pos_wide in zip(chunks, wides):
        one_chunk(s, cc, False, pos_wide)
    for s, cc in chunks:
        one_chunk(s, cc, True)


def _combine_call(offs, pos, y, x1, modl, row0, row_step, final_g, cap, win, layer):
    nb, n, d = x1.shape
    sb, tm = _seq_tiling(n, row_step, EC_ROWS)
    rows = _offs_rows(n)
    last = final_g is not None
    assert cap <= 256
    spread = np.zeros((LANES, N_EXPERTS * win), np.float32)
    for e in range(N_EXPERTS):
        spread[e, e * win:(e + 1) * win] = 1.0
    spread = jnp.asarray(spread, BF16)
    tok = pl.BlockSpec((sb, tm, d), lambda b, k, o: (b, k, 0))
    in_specs = [pl.BlockSpec((sb, tm, LANES), lambda b, k, o: (b, k, 0)),
                pl.BlockSpec((N_EXPERTS, sb * cap, d), lambda b, k, o: (0, b, 0)),
                tok,
                pl.BlockSpec((None, None, None, 1, d),
                             lambda b, k, o: (layer, row0 + b * sb * row_step, 5, 0, 0)),
                pl.BlockSpec(spread.shape, lambda b, k, o: (0, 0))]
    ins = [offs.reshape(-1), pos, y, x1, modl, spread]
    if last:
        in_specs.append(pl.BlockSpec((1, d), lambda b, k, o: (0, 0)))
        ins.append(final_g)
    nout = 2 if last else 1
    grid_spec = pltpu.PrefetchScalarGridSpec(
        num_scalar_prefetch=1, grid=(nb // sb, n // tm), in_specs=in_specs, out_specs=[tok] * nout)
    return pl.pallas_call(
        functools.partial(_combine_kernel, cap=cap, win=win, rows=rows, last=last),
        grid_spec=grid_spec,
        out_shape=[jax.ShapeDtypeStruct((nb, n, d), F32)] * nout,
        compiler_params=_cp(56), name="ec_combine_final" if last else "ec_combine",
    )(*ins)


def _prep_weights(w_in, w_s, b_s, w_pool, pool_scale, q_norm_g, w_qb, kv_norm_g, w_kvb, w_out,
                  w_router, norm1_g, norm2_g):
    depth, d, cols = w_in.shape
    row = lambda a: a.reshape(depth, 1, -1)
    w_in_p = jnp.pad(w_in, ((0, 0), (0, 0), (0, PROJ_PAD - cols))).astype(BF16)
    wq = w_qb.reshape(depth, Q_LORA, C_HEADS, QK_NOPE + QK_ROPE)
    wq = jnp.pad(wq, ((0, 0), (0, 0), (0, 0), (0, HEAD_PAD - QK_NOPE - QK_ROPE)))
    wq_t = wq.reshape(depth, Q_LORA, C_HEADS * HEAD_PAD).transpose(0, 2, 1).astype(BF16)
    wkv = w_kvb.reshape(depth, KV_LORA, C_HEADS, QK_NOPE + V_DIM)
    wkn = jnp.pad(wkv[..., :QK_NOPE], ((0, 0), (0, 0), (0, 0), (0, HEAD_PAD - QK_NOPE)))
    wkn = wkn.reshape(depth, KV_LORA, C_HEADS * HEAD_PAD).astype(BF16)
    wv_t = wkv[..., QK_NOPE:].reshape(depth, KV_LORA, C_WIDTH).transpose(0, 2, 1).astype(BF16)
    assert A_HEADS == len(POOL_WINDOWS) and A_HEAD_DIM == POOL_GROUP
    ws_cat = w_s.transpose(0, 2, 1, 3).reshape(depth, CHUNK, A_HEADS * CHUNK).astype(BF16)
    b_rep = jnp.repeat(b_s.transpose(0, 2, 1), A_HEAD_DIM, axis=2)
    eye = jnp.eye(len(POOL_WINDOWS), dtype=F32)
    wp_bd = jnp.einsum("gh,lgcd->lgchd", eye, w_pool).reshape(depth, B_WIDTH, B_WIDTH).astype(BF16)
    wr_hi = w_router.astype(BF16)
    wr_lo = (w_router - wr_hi.astype(F32)).astype(BF16)
    wr = jnp.pad(jnp.concatenate([wr_hi, wr_lo], axis=2), ((0, 0), (0, 0), (0, LANES - 2 * N_EXPERTS)))
    return dict(
        norm1_g=row(norm1_g), norm2_g=row(norm2_g), w_in=w_in_p, q_norm_g=row(q_norm_g),
        kv_norm_g=row(kv_norm_g), wq_t=wq_t, wkn=wkn, wv_t=wv_t, w_s=ws_cat, b_s=b_rep, w_pool=wp_bd,
        pool_scale=row(pool_scale), w_out=w_out.astype(BF16), w_router=wr)


def _rope_tables(n):
    rows = n // GRID_W
    row = jnp.repeat(jnp.arange(rows), GRID_W).astype(F32)
    col = jnp.broadcast_to(jnp.arange(GRID_W), (rows, GRID_W)).reshape(-1).astype(F32)
    inv = 1.0 / (ROPE_THETA ** (jnp.arange(0, ROPE_AXIS, 2, dtype=F32) / ROPE_AXIS))
    ang_r = row[:, None] * inv[None, :]
    ang_c = col[:, None] * inv[None, :]
    cos_r, sin_r, cos_c, sin_c = jnp.cos(ang_r), jnp.sin(ang_r), jnp.cos(ang_c), jnp.sin(ang_c)
    pad = jnp.zeros((n, LANES - QK_ROPE), F32)
    cos_k = jnp.concatenate([cos_r, cos_r, cos_c, cos_c, pad], axis=1)
    sin_k = jnp.concatenate([-sin_r, sin_r, -sin_c, sin_c, pad], axis=1)
    return cos_k, sin_k, cos_r.T, sin_r.T, cos_c.T, sin_c.T


def kernel(x_prompt, x_sample, cache_ckv, cache_krope, c, c_ctx, w_ada, b_ada, norm1_g, norm2_g, w_in, w_s, b_s, w_pool, pool_scale, q_norm_g, w_qb, kv_norm_g, w_kvb, w_out, w_router, w_e1, w_e3, w_e2, final_norm_g):
    depth, d, _ = w_ada.shape
    nb_c, n_c, _ = x_prompt.shape
    nb_s, n_s, _ = x_sample.shape
    assert 1 + nb_s <= 8
    cond8 = jnp.concatenate([c_ctx[None, :], c, jnp.zeros((8 - 1 - nb_s, d), F32)], axis=0)
    mod = _mod_call(cond8, w_ada, b_ada).reshape(depth, 8, 6, 1, d)
    lw = _prep_weights(w_in, w_s, b_s, w_pool, pool_scale, q_norm_g, w_qb, kv_norm_g, w_kvb, w_out,
                       w_router, norm1_g, norm2_g)
    k_cache, vt_cache = _kvexp_call(cache_ckv, cache_krope, lw["wkn"], lw["wv_t"])
    tabs = _rope_tables(n_s)
    band = _pool_bands()
    final_g = final_norm_g.reshape(1, d)
    cap_c = EC_CAPACITY * n_c // N_EXPERTS
    cap_s = EC_CAPACITY * n_s // N_EXPERTS
    win_c = min(cap_c, 64)
    win_s = min(cap_s, 64)

    xc, xs = x_prompt, x_sample
    yc = ys = None
    ckvs, kropes = [], []
    for l in range(depth):
        uc, vac, pinc, qtc, khc, vtc, ckv_c, kr_c = _proj_call(xc, mod, 0, 0, lw, None, l)
        us, vas, pins, qts, khs, vts = _proj_call(xs, mod, 1, 1, lw, tabs, l)
        ckvs.append(ckv_c)
        kropes.append(kr_c)
        oac, obc = _mix_call(uc, vac, pinc, lw, band, l)
        oas, obs = _mix_call(us, vas, pins, lw, band, l)
        occ = _attn_call(qtc, [(khc, vtc, None)])
        ocs = _attn_call(qts, [(k_cache, vt_cache, l), (khs, vts, None)])
        x1c, hc, affc, aspc = _outproj_call(xc, oac, obc, occ, mod, 0, 0, lw, l)
        x1s, hs, affs, asps = _outproj_call(xs, oas, obs, ocs, mod, 1, 1, lw, l)
        posc, postc, offc = _route_call(affc, cap_c)
        poss, posts, offs = _route_call(affs, cap_s)
        xgc, ggc = _gather_call(offc, postc, hc, aspc, cap_c, win_c)
        xgs, ggs = _gather_call(offs, posts, hs, asps, cap_s, win_s)
        ygs, ygc = _ffn_call(xgs, xgc, ggs, ggc, w_e1, w_e3, w_e2, l)
        fg = final_g if l == depth - 1 else None
        outc = _combine_call(offc, posc, ygc, x1c, mod, 0, 0, fg, cap_c, win_c, l)
        outs = _combine_call(offs, poss, ygs, x1s, mod, 1, 1, fg, cap_s, win_s, l)
        xc, xs = outc[0], outs[0]
    yc, ys = outc[1], outs[1]
    new_ckv = jnp.stack(ckvs, axis=1)
    new_krope = jnp.stack(kropes, axis=1)
    return (yc, ys, new_ckv, new_krope)
```

```python
import functools
import math

import numpy as np
import jax
import jax.numpy as jnp
from jax import lax
from jax.experimental import pallas as pl
from jax.experimental.pallas import tpu as pltpu

F32 = jnp.float32
BF16 = jnp.bfloat16

GRID_W = 64
A_HEADS = 4
A_HEAD_DIM = 64
A_WIDTH = A_HEADS * A_HEAD_DIM
CHUNK = 128
POOL_WINDOWS = (2, 4, 8, 16)
POOL_GROUP = 64
B_WIDTH = len(POOL_WINDOWS) * POOL_GROUP
C_HEADS = 8
QK_NOPE = 64
QK_ROPE = 32
V_DIM = 64
Q_LORA = 384
KV_LORA = 256
C_WIDTH = C_HEADS * V_DIM
ROPE_AXIS = QK_ROPE // 2
ROPE_THETA = 10000.0
ATTN_SCALE = (QK_NOPE + QK_ROPE) ** -0.5
N_EXPERTS = 16
EC_CAPACITY = 2
NORM_EPS = 1e-6

LANES = 128
HEAD_PAD = 128
PROJ_PAD = 1536
OFF_P = 2 * A_WIDTH
OFF_CQ = OFF_P + B_WIDTH
OFF_CKV = OFF_CQ + Q_LORA
OFF_KR = OFF_CKV + KV_LORA
MIX_UNROLL = 16
POOL_HALO = 8
TOK_TILE = 256
ATTN_TQ = 256
ROW_TILE = 1024
PROJ_SUB = 512
EC_ROWS = 1024
SUB_ROWS = 256
V_ROWS = V_DIM + 16
KEY_CHUNK = 512
SCORE_BUFFERS = 2
ATTN_Q_BLOCKS = 2
Q_PRESCALE = ATTN_SCALE * math.log2(math.e)
FF_TILE = 512
MOD_TILE = 1536
SEARCH_BITS = 31
SLOT_ALIGN = 16
GATHER_GROUP = 4
MXU_DEPTH = 256


def _cp(vmem_mb):
    return pltpu.CompilerParams(vmem_limit_bytes=vmem_mb * 1024 * 1024)


def _layer_spec(a, layer):
    return pl.BlockSpec((None,) + a.shape[1:], lambda *_: (layer,) + (0,) * (a.ndim - 1))


def _dot(a, b):
    return jnp.dot(a, b, preferred_element_type=F32)


def _dot_nt(a, b):
    return lax.dot_general(a, b, (((1,), (1,)), ((), ())), preferred_element_type=F32)


def _split2(x):
    hi = x.astype(BF16)
    lo = (x - hi.astype(F32)).astype(BF16)
    return hi, lo


def _rms(x):
    return x * lax.rsqrt(jnp.mean(x * x, axis=-1, keepdims=True) + NORM_EPS)


def _load_rows(ref):
    return jnp.concatenate([ref[s] for s in range(ref.shape[0])], axis=0)


def _store_rows(ref, val):
    tm = ref.shape[1]
    for s in range(ref.shape[0]):
        ref[s] = val[s * tm:(s + 1) * tm, :]


def _with_ones_row(vt):
    rows = lax.broadcasted_iota(jnp.int32, (V_ROWS - V_DIM, vt.shape[1]), 0)
    return jnp.concatenate([vt, jnp.where(rows == 0, 1.0, 0.0).astype(BF16)], axis=0)


def _mod_kernel(c_ref, w_ref, b_ref, o_ref):
    a = jax.nn.silu(c_ref[...])
    a_hi, a_lo = _split2(a)
    w_hi, w_lo = _split2(w_ref[...])
    o_ref[...] = _dot(a_hi, w_hi) + _dot(a_lo, w_hi) + _dot(a_hi, w_lo) + b_ref[...]


def _mod_call(cond8, w_ada, b_ada):
    depth, d, n = w_ada.shape
    return pl.pallas_call(
        _mod_kernel,
        grid=(depth, n // MOD_TILE),
        in_specs=[pl.BlockSpec((8, d), lambda l, j: (0, 0)),
                  pl.BlockSpec((None, d, MOD_TILE), lambda l, j: (l, 0, j)),
                  pl.BlockSpec((None, 1, MOD_TILE), lambda l, j: (l, 0, j))],
        out_specs=pl.BlockSpec((None, 8, MOD_TILE), lambda l, j: (l, 0, j)),
        out_shape=jax.ShapeDtypeStruct((depth, 8, n), F32),
        compiler_params=_cp(40),
        name="adaln_mod",
    )(cond8, w_ada, b_ada.reshape(depth, 1, n))


def _proj_kernel(*refs, rope):
    if rope:
        (x_ref, sh_ref, sc_ref, g_ref, win_ref, qg_ref, kvg_ref, wq_ref, wkn_ref, wvt_ref,
         ck_ref, sk_ref, cr_ref, sr_ref, cc_ref, scl_ref,
         u_ref, va_ref, pin_ref, qt_ref, kh_ref, vt_ref) = refs
    else:
        (x_ref, sh_ref, sc_ref, g_ref, win_ref, qg_ref, kvg_ref, wq_ref, wkn_ref, wvt_ref,
         u_ref, va_ref, pin_ref, qt_ref, kh_ref, vt_ref, ckv_ref, kr_ref) = refs
    nseq, tm = x_ref.shape[0], x_ref.shape[1]
    ps = min(tm, PROJ_SUB)
    blocks = [(s, t0) for s in range(nseq) for t0 in range(0, tm, ps)]

    projs = []
    for s, t0 in blocks:
        x = x_ref[s, t0:t0 + ps, :]
        h = _rms(x) * g_ref[...] * (1.0 + sc_ref[...]) + sh_ref[...]
        projs.append(_dot(h.astype(BF16), win_ref[...]))

    for (s, t0), proj in zip(blocks, projs):
        rows = (s, slice(t0, t0 + ps))
        u_ref[rows] = jax.nn.gelu(proj[:, 0:A_WIDTH])
        va_ref[rows] = jax.nn.gelu(proj[:, A_WIDTH:OFF_P]).astype(BF16)
        pin_ref[rows] = proj[:, OFF_P:OFF_CQ]
        cq = _rms(proj[:, OFF_CQ:OFF_CKV]) * qg_ref[...]
        ckv = _rms(proj[:, OFF_CKV:OFF_KR]) * kvg_ref[...]
        kr = proj[:, OFF_KR:OFF_KR + LANES]
        if not rope:
            ckv_ref[rows] = ckv
            kr_ref[rows] = kr[:, 0:QK_ROPE]

        qt = _dot_nt(wq_ref[...], cq.astype(BF16))
        ckv_b = ckv.astype(BF16)
        kn = _dot(ckv_b, wkn_ref[...])
        vt = _dot_nt(wvt_ref[...], ckv_b)

        if rope:
            lane = lax.broadcasted_iota(jnp.int32, kr.shape, 1)
            half = ROPE_AXIS // 2
            swapped = jnp.where((lane & half) == 0,
                                pltpu.roll(kr, LANES - half, axis=1), pltpu.roll(kr, half, axis=1))
            kr = kr * ck_ref[t0:t0 + ps, :] + swapped * sk_ref[t0:t0 + ps, :]
            cr, sr = cr_ref[:, t0:t0 + ps], sr_ref[:, t0:t0 + ps]
            cc, scl = cc_ref[:, t0:t0 + ps], scl_ref[:, t0:t0 + ps]
        kr_shift = pltpu.roll(kr, QK_NOPE, axis=1)

        for hd in range(C_HEADS):
            r0 = hd * HEAD_PAD
            q_h = qt[r0:r0 + HEAD_PAD, :]
            if rope:
                b0 = QK_NOPE
                a, b = q_h[b0:b0 + 8, :], q_h[b0 + 8:b0 + 16, :]
                c, d = q_h[b0 + 16:b0 + 24, :], q_h[b0 + 24:b0 + 32, :]
                q_h = jnp.concatenate(
                    [q_h[0:b0, :], a * cr - b * sr, a * sr + b * cr, c * cc - d * scl, c * scl + d * cc,
                     q_h[b0 + 32:, :]], axis=0)
            q_h = (q_h * Q_PRESCALE).astype(BF16)
            qw = qt_ref.shape[-1]
            for j in range(ps // qw):
                qt_ref[s, t0 // qw + j, hd] = q_h[:, j * qw:(j + 1) * qw]
            kh_ref[s, hd, t0:t0 + ps, :] = (kn[:, r0:r0 + HEAD_PAD] + kr_shift).astype(BF16)
            vt_ref[s, hd, :, t0:t0 + ps] = _with_ones_row(vt[hd * V_DIM:(hd + 1) * V_DIM, :].astype(BF16))


def _seq_tiling(n, row_step, rows=None):
    rows = ROW_TILE if rows is None else rows
    tm = min(n, rows)
    return (rows // tm if row_step == 0 else 1), tm


def _proj_call(x, modl, row0, row_step, lw, tabs, layer):
    nb, n, d = x.shape
    sb, tm = _seq_tiling(n, row_step)
    rope = tabs is not None
    assert not rope or sb == 1
    qw = min(n, ATTN_TQ)
    tok = lambda c: pl.BlockSpec((sb, tm, c), lambda b, i: (b, i, 0))
    const = lambda a: _layer_spec(a, layer)
    modspec = lambda j: pl.BlockSpec((None, None, None, 1, d),
                                     lambda b, i: (layer, row0 + b * sb * row_step, j, 0, 0))
    ins = [x, modl, modl, lw["norm1_g"], lw["w_in"], lw["q_norm_g"], lw["kv_norm_g"],
           lw["wq_t"], lw["wkn"], lw["wv_t"]]
    in_specs = [tok(d), modspec(0), modspec(1), const(lw["norm1_g"]), const(lw["w_in"]),
                const(lw["q_norm_g"]), const(lw["kv_norm_g"]), const(lw["wq_t"]), const(lw["wkn"]),
                const(lw["wv_t"])]
    if rope:
        cos_k, sin_k, cos_r, sin_r, cos_c, sin_c = tabs
        ins += [cos_k, sin_k, cos_r, sin_r, cos_c, sin_c]
        in_specs += [pl.BlockSpec((tm, LANES), lambda b, i: (i, 0))] * 2
        in_specs += [pl.BlockSpec((8, tm), lambda b, i: (0, i))] * 4
    out_shape = [jax.ShapeDtypeStruct((nb, n, A_WIDTH), F32),
                 jax.ShapeDtypeStruct((nb, n, A_WIDTH), BF16),
                 jax.ShapeDtypeStruct((nb, n, B_WIDTH), F32),
                 jax.ShapeDtypeStruct((nb, n // qw, C_HEADS, HEAD_PAD, qw), BF16),
                 jax.ShapeDtypeStruct((nb, C_HEADS, n, HEAD_PAD), BF16),
                 jax.ShapeDtypeStruct((nb, C_HEADS, V_ROWS,n), BF16),
                 jax.ShapeDtypeStruct((nb, n, KV_LORA), F32),
                 jax.ShapeDtypeStruct((nb, n, QK_ROPE), F32)]
    out_specs = [tok(A_WIDTH), tok(A_WIDTH), tok(B_WIDTH),
                 pl.BlockSpec((sb, tm // qw, C_HEADS, HEAD_PAD, qw), lambda b, i: (b, i, 0, 0, 0)),
                 pl.BlockSpec((sb, C_HEADS, tm, HEAD_PAD), lambda b, i: (b, 0, i, 0)),
                 pl.BlockSpec((sb, C_HEADS, V_ROWS, tm), lambda b, i: (b, 0, 0, i)),
                 tok(KV_LORA), tok(QK_ROPE)]
    if rope:
        out_shape, out_specs = out_shape[:-2], out_specs[:-2]
    return pl.pallas_call(
        functools.partial(_proj_kernel, rope=rope),
        grid=(nb // sb, n // tm), in_specs=in_specs, out_specs=out_specs, out_shape=out_shape,
        compiler_params=_cp(48), name="in_proj_rope" if rope else "in_proj",
    )(*ins)


def _kvexp_kernel(ckv_ref, kr_ref, wkn_ref, wvt_ref, place_ref, kh_ref, vt_ref):
    ckv_b = ckv_ref[...].astype(BF16)
    kn = _dot(ckv_b, wkn_ref[...])
    vt = _dot_nt(wvt_ref[...], ckv_b)
    kr_placed = _dot(kr_ref[...].astype(BF16), place_ref[...])
    for hd in range(C_HEADS):
        r0 = hd * HEAD_PAD
        kh_ref[hd] = (kn[:, r0:r0 + HEAD_PAD] + kr_placed).astype(BF16)
        vt_ref[hd] = _with_ones_row(vt[hd * V_DIM:(hd + 1) * V_DIM, :].astype(BF16))


def _kvexp_call(cache_ckv, cache_krope, wkn, wv_t):
    nb, depth, m, r = cache_ckv.shape
    place = np.zeros((QK_ROPE, HEAD_PAD), np.float32)
    place[np.arange(QK_ROPE), QK_NOPE + np.arange(QK_ROPE)] = 1.0
    place = jnp.asarray(place, BF16)
    return pl.pallas_call(
        _kvexp_kernel,
        grid=(depth, nb),
        in_specs=[pl.BlockSpec((None, None, m, r), lambda l, b: (b, l, 0, 0)),
                  pl.BlockSpec((None, None, m, QK_ROPE), lambda l, b: (b, l, 0, 0)),
                  pl.BlockSpec((None,) + wkn.shape[1:], lambda l, b: (l, 0, 0)),
                  pl.BlockSpec((None,) + wv_t.shape[1:], lambda l, b: (l, 0, 0)),
                  pl.BlockSpec(place.shape, lambda l, b: (0, 0))],
        out_specs=[pl.BlockSpec((None, None, C_HEADS, m, HEAD_PAD), lambda l, b: (l, b, 0, 0, 0)),
                   pl.BlockSpec((None, None, C_HEADS, V_ROWS,m), lambda l, b: (l, b, 0, 0, 0))],
        out_shape=[jax.ShapeDtypeStruct((depth, nb, C_HEADS, m, HEAD_PAD), BF16),
                   jax.ShapeDtypeStruct((depth, nb, C_HEADS, V_ROWS,m), BF16)],
        compiler_params=_cp(32), name="cache_kv_expand",
    )(cache_ckv, cache_krope, wkn, wv_t, place)


def _mix_kernel(u_ref, va_ref, pin_ref, ws_ref, bs_ref, band_ref, wp_ref, ps_ref,
                oa_ref, ob_ref, pad_ref):
    n = u_ref.shape[0]
    width = u_ref.shape[1]
    zero_rows = jnp.zeros((POOL_HALO, width), F32)
    pad_ref[0:POOL_HALO, :] = zero_rows
    pad_ref[n + POOL_HALO:n + 2 * POOL_HALO, :] = zero_rows
    pad_ref[POOL_HALO:n + POOL_HALO, :] = pin_ref[...]
    ngrp = len(POOL_WINDOWS)
    unroll = min(MIX_UNROLL, n // CHUNK)
    row = lax.broadcasted_iota(jnp.int32, (CHUNK, width), 0)
    grp_row = lax.broadcasted_iota(jnp.int32, (1, width), 1) // POOL_GROUP
    left = jnp.zeros((1, width), jnp.int32)
    right1 = jnp.zeros((1, width), jnp.int32)
    for g, w in enumerate(POOL_WINDOWS):
        left = jnp.where(grp_row == g, w // 2, left)
        right1 = jnp.where(grp_row == g, w - w // 2, right1)

    def group_masks(nrows):
        grp = lax.broadcasted_iota(jnp.int32, (nrows, width), 1) // POOL_GROUP
        return [jnp.where(grp == g, 1.0, 0.0).astype(BF16) for g in range(ngrp)]

    masks_v = group_masks(CHUNK)
    masks_w = group_masks(CHUNK + 2 * POOL_HALO)

    def group_stack(x, masks):
        return jnp.concatenate([x * m for m in masks], axis=0)

    def first_dots(c):
        r0 = pl.multiple_of(c * CHUNK, CHUNK)
        v_stack = group_stack(va_ref[pl.ds(r0, CHUNK), :], masks_v)
        s = _dot(ws_ref[...], v_stack) + bs_ref[...]
        oa_ref[pl.ds(r0, CHUNK), :] = (u_ref[pl.ds(r0, CHUNK), :] * s).astype(BF16)
        win = pad_ref[pl.ds(r0, CHUNK + 2 * POOL_HALO), :]
        w_hi, w_lo = _split2(win)
        tot = (_dot(band_ref[...], group_stack(w_hi, masks_w))
               + _dot(band_ref[...], group_stack(w_lo, masks_w)))
        t = r0 + row
        cnt = jnp.minimum(t + right1, n) - jnp.maximum(t - left, 0)
        p = win[POOL_HALO:POOL_HALO + CHUNK, :]
        return r0, (tot / cnt.astype(F32) - p).astype(BF16)

    def body(j, carry):
        diffs = [first_dots(j * unroll + r) for r in range(unroll)]
        for r0, diff in diffs:
            ob_ref[pl.ds(r0, CHUNK), :] = (_dot(diff, wp_ref[...]) * ps_ref[...]).astype(BF16)
        return carry

    lax.fori_loop(0, n // (CHUNK * unroll), body, 0)


def _pool_bands():
    rows = CHUNK + 2 * POOL_HALO
    band = np.zeros((len(POOL_WINDOWS), CHUNK, rows), np.float32)
    i = np.arange(CHUNK)[:, None]
    j = np.arange(rows)[None, :]
    for g, w in enumerate(POOL_WINDOWS):
        left = w // 2
        right = w - 1 - left
        band[g] = ((j >= i + POOL_HALO - left) & (j <= i + POOL_HALO + right)).astype(np.float32)
    return jnp.asarray(band.transpose(1, 0, 2).reshape(CHUNK, -1), BF16)


def _mix_call(u, va, pin, lw, band, layer):
    nb, n, w = u.shape
    seq = lambda: pl.BlockSpec((None, n, w), lambda b: (b, 0, 0))
    const = lambda a: _layer_spec(a, layer)
    return pl.pallas_call(
        _mix_kernel,
        grid=(nb,),
        in_specs=[seq(), seq(), seq(), const(lw["w_s"]), const(lw["b_s"]),
                  pl.BlockSpec(band.shape, lambda b: (0, 0)),
                  const(lw["w_pool"]), const(lw["pool_scale"])],
        out_specs=[seq(), seq()],
        out_shape=[jax.ShapeDtypeStruct((nb, n, w), BF16)] * 2,
        scratch_shapes=[pltpu.VMEM((n + 2 * POOL_HALO, w), F32)],
        compiler_params=_cp(48), name="mixers_ab",
    )(u, va, pin, lw["w_s"], lw["b_s"], band, lw["w_pool"], lw["pool_scale"])


def _attn_kernel(*refs, seg_lens):
    nseg = len(seg_lens)
    qt_ref = refs[0]
    k_refs = refs[1:1 + nseg]
    v_refs = refs[1 + nseg:1 + 2 * nseg]
    o_ref, ot_ref = refs[1 + 2 * nseg:3 + 2 * nseg]
    s_bufs = refs[3 + 2 * nseg:]
    nsq, nqb, _, _, tq = qt_ref.shape
    nitem = nsq * nqb * C_HEADS
    head_bits = C_HEADS.bit_length() - 1
    qb_bits = nqb.bit_length() - 1
    assert C_HEADS == 1 << head_bits and nqb == 1 << qb_bits

    def unpack(it):
        blk = it >> head_bits
        return blk >> qb_bits, blk, blk & (nqb - 1), it & (C_HEADS - 1)

    def scores_into(it, s_ref):
        sq, _, qb, hd = unpack(it)
        q = qt_ref[sq, qb, hd]
        mx = None
        r0 = 0
        for k_ref, m in zip(k_refs, seg_lens):
            for c0 in range(0, m, KEY_CHUNK):
                kc = min(KEY_CHUNK, m - c0)
                s = _dot(k_ref[sq, hd, c0:c0 + kc, :], q)
                s_ref[r0:r0 + kc, :] = s
                cm = jnp.max(s, axis=0, keepdims=True)
                mx = cm if mx is None else jnp.maximum(mx, cm)
                r0 += kc
        return mx

    def finish(it, s_ref, mx):
        sq, blk, _, hd = unpack(it)
        acc = jnp.zeros((V_ROWS, tq), F32)
        r0 = 0
        for v_ref, m in zip(v_refs, seg_lens):
            for c0 in range(0, m, KEY_CHUNK):
                kc = min(KEY_CHUNK, m - c0)
                p = jnp.exp2(s_ref[r0:r0 + kc, :] - mx).astype(BF16)
                acc = acc + _dot(v_ref[sq, hd, :, c0:c0 + kc], p)
                r0 += kc
        rows = pl.ds(pl.multiple_of(hd * V_DIM, V_DIM), V_DIM)
        ot_ref[blk, rows, :] = acc[0:V_DIM, :] / acc[V_DIM:V_DIM + 1, :]

    nbuf = len(s_bufs)

    def step(t, r, mx):
        mx_next = scores_into(t + 1, s_bufs[(r + 1) % nbuf])
        finish(t, s_bufs[r], mx)
        return mx_next

    def rotation(j, mx):
        for r in range(nbuf):
            mx = step(nbuf * j + r, r, mx)
        return mx

    nsteps = nitem - 1
    mx = lax.fori_loop(0, nsteps // nbuf, rotation, scores_into(0, s_bufs[0]))
    for r in range(nsteps % nbuf):
        mx = step((nsteps // nbuf) * nbuf + r, r, mx)
    finish(nitem - 1, s_bufs[(nitem - 1) % nbuf], mx)
    for sq in range(nsq):
        for qb in range(nqb):
            o_ref[sq, qb * tq:(qb + 1) * tq, :] = ot_ref[sq * nqb + qb].T.astype(BF16)


def _attn_call(qt, segs):
    nb, nblk, _, _, tq = qt.shape
    nqb = min(nblk, ATTN_Q_BLOCKS)
    sb = ATTN_Q_BLOCKS // nqb
    seg_lens = tuple(k.shape[-2] for k, _, _ in segs)

    def seg_spec(a, layer):
        if layer is None:
            return pl.BlockSpec((sb,) + a.shape[1:], lambda b, i: (b, 0, 0, 0))
        return pl.BlockSpec((None, sb) + a.shape[2:], lambda b, i: (layer, b, 0, 0, 0))

    in_specs = [pl.BlockSpec((sb, nqb, C_HEADS, HEAD_PAD, tq), lambda b, i: (b, i, 0, 0, 0))]
    in_specs += [seg_spec(k, layer) for k, _, layer in segs]
    in_specs += [seg_spec(v, layer) for _, v, layer in segs]
    return pl.pallas_call(
        functools.partial(_attn_kernel, seg_lens=seg_lens),
        grid=(nb // sb, nblk // nqb), in_specs=in_specs,
        out_specs=pl.BlockSpec((sb, nqb * tq, C_WIDTH), lambda b, i: (b, i, 0)),
        out_shape=jax.ShapeDtypeStruct((nb, nblk * tq, C_WIDTH), BF16),
        scratch_shapes=([pltpu.VMEM((sb * nqb, C_WIDTH, tq), F32)]
                        + [pltpu.VMEM((sum(seg_lens), tq), F32)] * SCORE_BUFFERS),
        compiler_params=_cp(48), name="latent_attention",
    )(qt, *[k for k, _, _ in segs], *[v for _, v, _ in segs])


def _outproj_mix(rows, x_ref, oa_ref, ob_ref, oc_ref, wo_ref, g1_ref, sh_ref, sc_ref, ng_ref,
                 wr_ref, x1_ref, h_ref, aff_ref, asp_ref):
    mix = (_dot(oa_ref[rows], wo_ref[0:A_WIDTH, :])
           + _dot(ob_ref[rows], wo_ref[A_WIDTH:A_WIDTH + B_WIDTH, :])
           + _dot(oc_ref[rows], wo_ref[A_WIDTH + B_WIDTH:, :]))
    x1 = x_ref[rows] + g1_ref[...] * mix
    x1_ref[rows] = x1
    h = _rms(x1) * ng_ref[...] * (1.0 + sc_ref[...]) + sh_ref[...]
    h_hi, h_lo = _split2(h)
    h_ref[rows] = h_hi
    return h_hi, h_lo


def _outproj_route(rows, h_hi, h_lo, wr_ref, aff_ref, asp_ref):
    part = _dot(h_hi, wr_ref[...])
    logits = part + pltpu.roll(part, LANES - N_EXPERTS, axis=1) + _dot(h_lo, wr_ref[...])
    lane = lax.broadcasted_iota(jnp.int32, logits.shape, 1)
    logits = jnp.where(lane < N_EXPERTS, logits, -jnp.inf)
    ex = jnp.exp(logits - jnp.max(logits, axis=-1, keepdims=True))
    aff = ex / jnp.sum(ex, axis=-1, keepdims=True)
    aff_ref[rows] = aff
    a_hi = aff.astype(BF16).astype(F32)
    r1 = aff - a_hi
    a_lo = r1.astype(BF16).astype(F32)
    a_lo2 = r1 - a_lo
    packed = a_hi + pltpu.roll(a_lo, N_EXPERTS, axis=1) + pltpu.roll(a_lo2, 2 * N_EXPERTS, axis=1)
    asp_ref[rows] = packed.astype(BF16)


def _outproj_kernel(x_ref, *refs):
    nseq, tm = x_ref.shape[0], x_ref.shape[1]
    wr_ref, aff_ref, asp_ref = refs[8], refs[11], refs[12]
    sub = min(tm, SUB_ROWS)
    blocks = [(s, slice(r0, r0 + sub)) for s in range(nseq) for r0 in range(0, tm, sub)]
    splits = [_outproj_mix(rows, x_ref, *refs) for rows in blocks]
    for rows, (h_hi, h_lo) in zip(blocks, splits):
        _outproj_route(rows, h_hi, h_lo, wr_ref, aff_ref, asp_ref)


def _outproj_call(x, oa, ob, oc, modl, row0, row_step, lw, layer):
    nb, n, d = x.shape
    sb, tm = _seq_tiling(n, row_step)
    tok = lambda c: pl.BlockSpec((sb, tm, c), lambda b, i: (b, i, 0))
    const = lambda a: _layer_spec(a, layer)
    modspec = lambda j: pl.BlockSpec((None, None, None, 1, d),
                                     lambda b, i: (layer, row0 + b * sb * row_step, j, 0, 0))
    return pl.pallas_call(
        _outproj_kernel,
        grid=(nb // sb, n // tm),
        in_specs=[tok(d), tok(A_WIDTH), tok(B_WIDTH), tok(C_WIDTH), const(lw["w_out"]),
                  modspec(2), modspec(3), modspec(4), const(lw["norm2_g"]),
                  const(lw["w_router"])],
        out_specs=[tok(d), tok(d), tok(LANES), tok(LANES)],
        out_shape=[jax.ShapeDtypeStruct((nb, n, d), F32), jax.ShapeDtypeStruct((nb, n, d), BF16),
                   jax.ShapeDtypeStruct((nb, n, LANES), F32), jax.ShapeDtypeStruct((nb, n, LANES), BF16)],
        compiler_params=_cp(48), name="out_proj_router",
    )(x, oa, ob, oc, lw["w_out"], modl, modl, modl, lw["norm2_g"], lw["w_router"])


def _route_kernel(aff_ref, pos_ref, post_ref, offs_ref, *, cap):
    nb, n, _ = aff_ref.shape
    c = TOK_TILE
    nck = n // c
    aff = aff_ref[...]
    capf = jnp.float32(cap)

    def search(i, bits):
        cand = bits | jnp.left_shift(jnp.int32(1), jnp.int32(SEARCH_BITS - 1) - i)
        cnt = jnp.sum(jnp.where(aff >= lax.bitcast_convert_type(cand, F32), 1.0, 0.0),
                      axis=1, keepdims=True)
        return jnp.where(cnt >= capf, cand, bits)

    bits = lax.fori_loop(0, SEARCH_BITS, search, jnp.zeros((nb, 1, LANES), jnp.int32))
    thr_all = lax.bitcast_convert_type(bits, F32)
    n_gt = jnp.sum(jnp.where(aff > thr_all, 1.0, 0.0), axis=1, keepdims=True)
    n_tie_all = capf - n_gt
    ri = lax.broadcasted_iota(jnp.int32, (c, c), 0)
    ci = lax.broadcasted_iota(jnp.int32, (c, c), 1)
    tri = jnp.where(ri > ci, 1.0, 0.0).astype(BF16)
    lane_ok = lax.broadcasted_iota(jnp.int32, (c, LANES), 1) < N_EXPERTS
    offs_ref[...] = jnp.zeros(offs_ref.shape, jnp.int32)
    for s in range(nb):
        thr = thr_all[s]
        n_tie = n_tie_all[s]
        tie_seen = jnp.zeros((1, LANES), F32)
        sel_seen = jnp.zeros((1, LANES), F32)
        for k in range(nck):
            kk = aff_ref[s, k * c:(k + 1) * c, :]
            gt = kk > thr
            eq = kk == thr
            eqf = jnp.where(eq, 1.0, 0.0)
            tie_rank = _dot(tri, eqf.astype(BF16)) + tie_seen
            sel = (gt | (eq & (tie_rank < n_tie))) & lane_ok
            self_ = jnp.where(sel, 1.0, 0.0)
            slot = _dot(tri, self_.astype(BF16)) + sel_seen
            pos = jnp.where(sel, slot, -1.0)
            pos_ref[s, k * c:(k + 1) * c, :] = pos
            post_ref[s, :, k * c:(k + 1) * c] = pos.T[0:N_EXPERTS, :]
            offs_ref[s, k:k + 1, :] = sel_seen[:, 0:N_EXPERTS].astype(jnp.int32)
            tie_seen = tie_seen + jnp.sum(eqf, axis=0, keepdims=True)
            sel_seen = sel_seen + jnp.sum(self_, axis=0, keepdims=True)
        offs_ref[s, nck:nck + 1, :] = sel_seen[:, 0:N_EXPERTS].astype(jnp.int32)


def _offs_rows(n):
    return -(-(n // TOK_TILE + 1) // 8) * 8


def _route_call(aff, cap):
    nb, n, _ = aff.shape
    rows = _offs_rows(n)
    return pl.pallas_call(
        functools.partial(_route_kernel, cap=cap),
        grid=(1,),
        in_specs=[pl.BlockSpec((nb, n, LANES), lambda i: (0, 0, 0))],
        out_specs=[pl.BlockSpec((nb, n, LANES), lambda i: (0, 0, 0)),
                   pl.BlockSpec((nb, N_EXPERTS, n), lambda i: (0, 0, 0)),
                   pl.BlockSpec((nb, rows, N_EXPERTS), lambda i: (0, 0, 0))],
        out_shape=[jax.ShapeDtypeStruct((nb, n, LANES), F32),
                   jax.ShapeDtypeStruct((nb, N_EXPERTS, n), F32),
                   jax.ShapeDtypeStruct((nb, rows, N_EXPERTS), jnp.int32)],
        compiler_params=_cp(32), name="ec_route",
    )(aff, )


def _window(offs_ref, b, k, e, rows, win):
    off = offs_ref[(b * rows + k) * N_EXPERTS + e]
    nxt = offs_ref[(b * rows + k + 1) * N_EXPERTS + e]
    base = lax.shift_right_logical(off, SLOT_ALIGN.bit_length() - 1) * SLOT_ALIGN
    nwin = lax.shift_right_logical(nxt - base + (win - 1), win.bit_length() - 1)
    return base, nwin


def _slot_targets(ids, start, cap, win):
    cb = pl.multiple_of(jnp.minimum(start, cap - win), SLOT_ALIGN)
    slot = ids + cb.astype(F32)
    return jnp.where(slot >= start.astype(F32), slot, -2.0), cb


def _gather_kernel(offs_ref, post_ref, h_ref, asp_ref, xs_ref, gs_ref, *, cap, win, rows):
    nseq, tm = h_ref.shape[0], h_ref.shape[1]
    c = TOK_TILE
    b0 = pl.program_id(0) * nseq
    k0 = pl.program_id(1) * (tm // c)

    @pl.when(pl.program_id(1) == 0)
    def _():
        xs_ref[...] = jnp.zeros(xs_ref.shape, xs_ref.dtype)
        gs_ref[...] = jnp.zeros(gs_ref.shape, gs_ref.dtype)

    sub = lax.broadcasted_iota(jnp.int32, (win, 1), 0).astype(F32)

    def one_chunk(s, cc, overflow):
        b = b0 + s
        k = k0 + cc
        cols = slice(cc * c, (cc + 1) * c)
        hk = h_ref[s, cols, :]
        gk = asp_ref[s, cols, :]

        def sel_rows(e, start):
            tgt, cb = _slot_targets(sub, start, cap, win)
            return jnp.where(post_ref[s, e:e + 1, cols] == tgt, 1.0, 0.0).astype(BF16), s * cap + cb

        if overflow:
            for e in range(N_EXPERTS):
                base, nwin = _window(offs_ref, b, k, e, rows, win)

                def extra(w, carry, e=e, base=base):
                    p, cbw = sel_rows(e, base + w * win)
                    xs_ref[e, pl.ds(cbw, win), :] += _dot(p, hk).astype(BF16)
                    gs_ref[e, pl.ds(cbw, win), :] += _dot(p, gk)
                    return carry

                lax.fori_loop(1, nwin, extra, 0)
            return

        for g0 in range(0, N_EXPERTS, GATHER_GROUP):
            parts, cbs = [], []
            for e in range(g0, g0 + GATHER_GROUP):
                base, _ = _window(offs_ref, b, k, e, rows, win)
                p, cb = sel_rows(e, base)
                parts.append(p)
                cbs.append(cb)
            p_grp = jnp.concatenate(parts, axis=0)
            r_grp = _dot(p_grp, hk).astype(BF16)
            g_grp = _dot(p_grp, gk)
            for j, cb in enumerate(cbs):
                xs_ref[g0 + j, pl.ds(cb, win), :] += r_grp[j * win:(j + 1) * win, :]
                gs_ref[g0 + j, pl.ds(cb, win), :] += g_grp[j * win:(j + 1) * win, :]

    for overflow in (False, True):
        for s in range(nseq):
            for cc in range(tm // c):
                one_chunk(s, cc, overflow)


def _gather_call(offs, post, h, asp, cap, win):
    nb, n, d = h.shape
    sb, tm = _seq_tiling(n, 0, EC_ROWS)
    rows = _offs_rows(n)
    grid_spec = pltpu.PrefetchScalarGridSpec(
        num_scalar_prefetch=1, grid=(nb // sb, n // tm),
        in_specs=[pl.BlockSpec((sb, N_EXPERTS, tm), lambda b, k, o: (b, 0, k)),
                  pl.BlockSpec((sb, tm, d), lambda b, k, o: (b, k, 0)),
                  pl.BlockSpec((sb, tm, LANES), lambda b, k, o: (b, k, 0))],
        out_specs=[pl.BlockSpec((N_EXPERTS, sb * cap, d), lambda b, k, o: (0, b, 0)),
                   pl.BlockSpec((N_EXPERTS, sb * cap, LANES), lambda b, k, o: (0, b, 0))])
    return pl.pallas_call(
        functools.partial(_gather_kernel, cap=cap, win=win, rows=rows),
        grid_spec=grid_spec,
        out_shape=[jax.ShapeDtypeStruct((N_EXPERTS, nb * cap, d), BF16),
                   jax.ShapeDtypeStruct((N_EXPERTS, nb * cap, LANES), F32)],
        compiler_params=_cp(48), name="ec_gather",
    )(offs.reshape(-1), post, h, asp)


def _ffn_kernel(xa_ref, xb_ref, ga_ref, gb_ref, w1_ref, w3_ref, w2_ref, ya_ref, yb_ref,
                acca_ref, accb_ref):
    e = pl.program_id(0)
    f = pl.program_id(1)

    @pl.when((e == 0) & (f == 0))
    def _():
        acca_ref[...] = jnp.zeros(acca_ref.shape, F32)
        accb_ref[...] = jnp.zeros(accb_ref.shape, F32)

    w1 = w1_ref[...].astype(BF16)
    w3 = w3_ref[...].astype(BF16)
    w2 = w2_ref[...].astype(BF16)
    groups = ((xa_ref, ga_ref, ya_ref, acca_ref), (xb_ref, gb_ref, yb_ref, accb_ref))
    hids = [(jax.nn.silu(_dot(x_ref[...], w1)) * _dot(x_ref[...], w3)).astype(BF16)
            for x_ref, _, _, _ in groups]
    for (x_ref, g_ref, y_ref, acc_ref), hid in zip(groups, hids):
        acc = jnp.where(f > 0, acc_ref[...], 0.0) + _dot(hid, w2)
        acc_ref[...] = acc
        g = g_ref[...]
        lane = lax.broadcasted_iota(jnp.int32, g.shape, 1)
        keep = ((lane % N_EXPERTS) == e) & (lane < 3 * N_EXPERTS)
        gate = jnp.sum(jnp.where(keep, g, 0.0), axis=-1, keepdims=True)
        y_ref[...] = (acc * gate).astype(BF16)


def _ffn_call(xa, xb, ga, gb, w1, w3, w2, layer):
    ne, ra, d = xa.shape
    rb = xb.shape[1]
    ff = w1.shape[3]
    tf = FF_TILE
    rowsp = lambda r, c: pl.BlockSpec((None, r, c), lambda e, f: (e, 0, 0))
    return pl.pallas_call(
        _ffn_kernel,
        grid=(ne, ff // tf),
        in_specs=[rowsp(ra, d), rowsp(rb, d), rowsp(ra, LANES), rowsp(rb, LANES),
                  pl.BlockSpec((None, None, d, tf), lambda e, f: (layer, e, 0, f)),
                  pl.BlockSpec((None, None, d, tf), lambda e, f: (layer, e, 0, f)),
                  pl.BlockSpec((None, None, tf, d), lambda e, f: (layer, e, f, 0))],
        out_specs=[rowsp(ra, d), rowsp(rb, d)],
        out_shape=[jax.ShapeDtypeStruct((ne, ra, d), BF16), jax.ShapeDtypeStruct((ne, rb, d), BF16)],
        scratch_shapes=[pltpu.VMEM((ra, d), F32), pltpu.VMEM((rb, d), F32)],
        compiler_params=_cp(56), name="expert_swiglu",
    )(xa, xb, ga, gb, w1, w3, w2)


def _combine_kernel(offs_ref, pos_ref, y_ref, x_ref, g2_ref, ex_ref, *rest, cap, win, rows, last):
    if last:
        fg_ref, o_ref, on_ref = rest
    else:
        (o_ref,) = rest
    nseq, tm = x_ref.shape[0], x_ref.shape[1]
    c = TOK_TILE
    b0 = pl.program_id(0) * nseq
    k0 = pl.program_id(1) * (tm // c)
    per = MXU_DEPTH // win
    depth = per * win
    lane = lax.broadcasted_iota(jnp.int32, (1, depth), 1)
    lane_grp = lane // win
    lane_in = (lane % win).astype(F32)
    lane1 = lax.broadcasted_iota(jnp.int32, (1, win), 1).astype(F32)

    def one_chunk(s, cc, overflow, pos_wide=None):
        b = b0 + s
        k = k0 + cc
        rws = (s, slice(cc * c, (cc + 1) * c))
        if overflow:
            for e in range(N_EXPERTS):
                base, nwin = _window(offs_ref, b, k, e, rows, win)

                def extra(w, carry, e=e, base=base):
                    tgt_w, cbw = _slot_targets(lane1, base + w * win, cap, win)
                    pt = jnp.where(pos_ref[rws][:, e:e + 1] == tgt_w, 1.0, 0.0).astype(BF16)
                    o_ref[rws] += g2_ref[...] * _dot(pt, y_ref[e, pl.ds(s * cap + cbw, win), :])
                    return carry

                lax.fori_loop(1, nwin, extra, 0)
            if last:
                on_ref[rws] = _rms(o_ref[rws]) * fg_ref[...]
            return
        tot = None
        for gi, g0 in enumerate(range(0, N_EXPERTS, per)):
            tgt = jnp.full((1, depth), -2.0, F32)
            ys = []
            for j in range(per):
                base, _ = _window(offs_ref, b, k, g0 + j, rows, win)
                tgt_e, cb = _slot_targets(lane_in, base, cap, win)
                tgt = jnp.where(lane_grp == j, tgt_e, tgt)
                ys.append(y_ref[g0 + j, pl.ds(s * cap + cb, win), :])
            pt = jnp.where(pos_wide[:, gi * depth:(gi + 1) * depth] == tgt, 1.0, 0.0).astype(BF16)
            d = _dot(pt, jnp.concatenate(ys, axis=0))
            tot = d if tot is None else tot + d
        o_ref[rws] = x_ref[rws] + g2_ref[...] * tot

    chunks = [(s, cc) for s in range(nseq) for cc in range(tm // c)]
    wides = [_dot(pos_ref[s, cc * c:(cc + 1) * c, :].astype(BF16), ex_ref[...]) for s, cc in chunks]
    for (s, cc), pos_wide in zip(chunks, wides):
        one_chunk(s, cc, False, pos_wide)
    for s, cc in chunks:
        one_chunk(s, cc, True)


def _combine_call(offs, pos, y, x1, modl, row0, row_step, final_g, cap, win, layer):
    nb, n, d = x1.shape
    sb, tm = _seq_tiling(n, row_step, EC_ROWS)
    rows = _offs_rows(n)
    last = final_g is not None
    assert cap <= 256
    spread = np.zeros((LANES, N_EXPERTS * win), np.float32)
    for e in range(N_EXPERTS):
        spread[e, e * win:(e + 1) * win] = 1.0
    spread = jnp.asarray(spread, BF16)
    tok = pl.BlockSpec((sb, tm, d), lambda b, k, o: (b, k, 0))
    in_specs = [pl.BlockSpec((sb, tm, LANES), lambda b, k, o: (b, k, 0)),
                pl.BlockSpec((N_EXPERTS, sb * cap, d), lambda b, k, o: (0, b, 0)),
                tok,
                pl.BlockSpec((None, None, None, 1, d),
                             lambda b, k, o: (layer, row0 + b * sb * row_step, 5, 0, 0)),
                pl.BlockSpec(spread.shape, lambda b, k, o: (0, 0))]
    ins = [offs.reshape(-1), pos, y, x1, modl, spread]
    if last:
        in_specs.append(pl.BlockSpec((1, d), lambda b, k, o: (0, 0)))
        ins.append(final_g)
    nout = 2 if last else 1
    grid_spec = pltpu.PrefetchScalarGridSpec(
        num_scalar_prefetch=1, grid=(nb // sb, n // tm), in_specs=in_specs, out_specs=[tok] * nout)
    return pl.pallas_call(
        functools.partial(_combine_kernel, cap=cap, win=win, rows=rows, last=last),
        grid_spec=grid_spec,
        out_shape=[jax.ShapeDtypeStruct((nb, n, d), F32)] * nout,
        compiler_params=_cp(56), name="ec_combine_final" if last else "ec_combine",
    )(*ins)


def _prep_weights(w_in, w_s, b_s, w_pool, pool_scale, q_norm_g, w_qb, kv_norm_g, w_kvb, w_out,
                  w_router, norm1_g, norm2_g):
    depth, d, cols = w_in.shape
    row = lambda a: a.reshape(depth, 1, -1)
    w_in_p = jnp.pad(w_in, ((0, 0), (0, 0), (0, PROJ_PAD - cols))).astype(BF16)
    wq = w_qb.reshape(depth, Q_LORA, C_HEADS, QK_NOPE + QK_ROPE)
    wq = jnp.pad(wq, ((0, 0), (0, 0), (0, 0), (0, HEAD_PAD - QK_NOPE - QK_ROPE)))
    wq_t = wq.reshape(depth, Q_LORA, C_HEADS * HEAD_PAD).transpose(0, 2, 1).astype(BF16)
    wkv = w_kvb.reshape(depth, KV_LORA, C_HEADS, QK_NOPE + V_DIM)
    wkn = jnp.pad(wkv[..., :QK_NOPE], ((0, 0), (0, 0), (0, 0), (0, HEAD_PAD - QK_NOPE)))
    wkn = wkn.reshape(depth, KV_LORA, C_HEADS * HEAD_PAD).astype(BF16)
    wv_t = wkv[..., QK_NOPE:].reshape(depth, KV_LORA, C_WIDTH).transpose(0, 2, 1).astype(BF16)
    assert A_HEADS == len(POOL_WINDOWS) and A_HEAD_DIM == POOL_GROUP
    ws_cat = w_s.transpose(0, 2, 1, 3).reshape(depth, CHUNK, A_HEADS * CHUNK).astype(BF16)
    b_rep = jnp.repeat(b_s.transpose(0, 2, 1), A_HEAD_DIM, axis=2)
    eye = jnp.eye(len(POOL_WINDOWS), dtype=F32)
    wp_bd = jnp.einsum("gh,lgcd->lgchd", eye, w_pool).reshape(depth, B_WIDTH, B_WIDTH).astype(BF16)
    wr_hi = w_router.astype(BF16)
    wr_lo = (w_router - wr_hi.astype(F32)).astype(BF16)
    wr = jnp.pad(jnp.concatenate([wr_hi, wr_lo], axis=2), ((0, 0), (0, 0), (0, LANES - 2 * N_EXPERTS)))
    return dict(
        norm1_g=row(norm1_g), norm2_g=row(norm2_g), w_in=w_in_p, q_norm_g=row(q_norm_g),
        kv_norm_g=row(kv_norm_g), wq_t=wq_t, wkn=wkn, wv_t=wv_t, w_s=ws_cat, b_s=b_rep, w_pool=wp_bd,
        pool_scale=row(pool_scale), w_out=w_out.astype(BF16), w_router=wr)


def _rope_tables(n):
    rows = n // GRID_W
    row = jnp.repeat(jnp.arange(rows), GRID_W).astype(F32)
    col = jnp.broadcast_to(jnp.arange(GRID_W), (rows, GRID_W)).reshape(-1).astype(F32)
    inv = 1.0 / (ROPE_THETA ** (jnp.arange(0, ROPE_AXIS, 2, dtype=F32) / ROPE_AXIS))
    ang_r = row[:, None] * inv[None, :]
    ang_c = col[:, None] * inv[None, :]
    cos_r, sin_r, cos_c, sin_c = jnp.cos(ang_r), jnp.sin(ang_r), jnp.cos(ang_c), jnp.sin(ang_c)
    pad = jnp.zeros((n, LANES - QK_ROPE), F32)
    cos_k = jnp.concatenate([cos_r, cos_r, cos_c, cos_c, pad], axis=1)
    sin_k = jnp.concatenate([-sin_r, sin_r, -sin_c, sin_c, pad], axis=1)
    return cos_k, sin_k, cos_r.T, sin_r.T, cos_c.T, sin_c.T


def kernel(x_prompt, x_sample, cache_ckv, cache_krope, c, c_ctx, w_ada, b_ada, norm1_g, norm2_g, w_in, w_s, b_s, w_pool, pool_scale, q_norm_g, w_qb, kv_norm_g, w_kvb, w_out, w_router, w_e1, w_e3, w_e2, final_norm_g):
    depth, d, _ = w_ada.shape
    nb_c, n_c, _ = x_prompt.shape
    nb_s, n_s, _ = x_sample.shape
    assert 1 + nb_s <= 8
    cond8 = jnp.concatenate([c_ctx[None, :], c, jnp.zeros((8 - 1 - nb_s, d), F32)], axis=0)
    mod = _mod_call(cond8, w_ada, b_ada).reshape(depth, 8, 6, 1, d)
    lw = _prep_weights(w_in, w_s, b_s, w_pool, pool_scale, q_norm_g, w_qb, kv_norm_g, w_kvb, w_out,
                       w_router, norm1_g, norm2_g)
    k_cache, vt_cache = _kvexp_call(cache_ckv, cache_krope, lw["wkn"], lw["wv_t"])
    tabs = _rope_tables(n_s)
    band = _pool_bands()
    final_g = final_norm_g.reshape(1, d)
    cap_c = EC_CAPACITY * n_c // N_EXPERTS
    cap_s = EC_CAPACITY * n_s // N_EXPERTS
    win_c = min(cap_c, 64)
    win_s = min(cap_s, 64)

    xc, xs = x_prompt, x_sample
    yc = ys = None
    ckvs, kropes = [], []
    for l in range(depth):
        uc, vac, pinc, qtc, khc, vtc, ckv_c, kr_c = _proj_call(xc, mod, 0, 0, lw, None, l)
        us, vas, pins, qts, khs, vts = _proj_call(xs, mod, 1, 1, lw, tabs, l)
        ckvs.append(ckv_c)
        kropes.append(kr_c)
        oac, obc = _mix_call(uc, vac, pinc, lw, band, l)
        oas, obs = _mix_call(us, vas, pins, lw, band, l)
        occ = _attn_call(qtc, [(khc, vtc, None)])
        ocs = _attn_call(qts, [(k_cache, vt_cache, l), (khs, vts, None)])
        x1c, hc, affc, aspc = _outproj_call(xc, oac, obc, occ, mod, 0, 0, lw, l)
        x1s, hs, affs, asps = _outproj_call(xs, oas, obs, ocs, mod, 1, 1, lw, l)
        posc, postc, offc = _route_call(affc, cap_c)
        poss, posts, offs = _route_call(affs, cap_s)
        xgc, ggc = _gather_call(offc, postc, hc, aspc, cap_c, win_c)
        xgs, ggs = _gather_call(offs, posts, hs, asps, cap_s, win_s)
        ygs, ygc = _ffn_call(xgs, xgc, ggs, ggc, w_e1, w_e3, w_e2, l)
        fg = final_g if l == depth - 1 else None
        outc = _combine_call(offc, posc, ygc, x1c, mod, 0, 0, fg, cap_c, win_c, l)
        outs = _combine_call(offs, poss, ygs, x1s, mod, 1, 1, fg, cap_s, win_s, l)
        xc, xs = outc[0], outs[0]
    yc, ys = outc[1], outs[1]
    new_ckv = jnp.stack(ckvs, axis=1)
    new_krope = jnp.stack(kropes, axis=1)
    return (yc, ys, new_ckv, new_krope)
```

```python
import functools
import math

import numpy as np
import jax
import jax.numpy as jnp
from jax import lax
from jax.experimental import pallas as pl
from jax.experimental.pallas import tpu as pltpu

F32 = jnp.float32
BF16 = jnp.bfloat16

GRID_W = 64
A_HEADS = 4
A_HEAD_DIM = 64
A_WIDTH = A_HEADS * A_HEAD_DIM
CHUNK = 128
POOL_WINDOWS = (2, 4, 8, 16)
POOL_GROUP = 64
B_WIDTH = len(POOL_WINDOWS) * POOL_GROUP
C_HEADS = 8
QK_NOPE = 64
QK_ROPE = 32
V_DIM = 64
Q_LORA = 384
KV_LORA = 256
C_WIDTH = C_HEADS * V_DIM
ROPE_AXIS = QK_ROPE // 2
ROPE_THETA = 10000.0
ATTN_SCALE = (QK_NOPE + QK_ROPE) ** -0.5
N_EXPERTS = 16
EC_CAPACITY = 2
NORM_EPS = 1e-6

LANES = 128
HEAD_PAD = 128
PROJ_PAD = 1536
OFF_P = 2 * A_WIDTH
OFF_CQ = OFF_P + B_WIDTH
OFF_CKV = OFF_CQ + Q_LORA
OFF_KR = OFF_CKV + KV_LORA
MIX_UNROLL = 16
POOL_HALO = 8
TOK_TILE = 256
ATTN_TQ = 256
ROW_TILE = 1024
PROJ_SUB = 256
EC_ROWS = 1024
SUB_ROWS = 256
V_ROWS = V_DIM + 16
KEY_CHUNK = 512
SCORE_BUFFERS = 2
ATTN_Q_BLOCKS = 4
Q_PRESCALE = ATTN_SCALE * math.log2(math.e)
FF_TILE = 512
MOD_TILE = 1536
SEARCH_BITS = 31
SLOT_ALIGN = 16
GATHER_GROUP = 4
MXU_DEPTH = 256


def _cp(vmem_mb):
    return pltpu.CompilerParams(vmem_limit_bytes=vmem_mb * 1024 * 1024)


def _layer_spec(a, layer):
    return pl.BlockSpec((None,) + a.shape[1:], lambda *_: (layer,) + (0,) * (a.ndim - 1))


def _dot(a, b):
    return jnp.dot(a, b, preferred_element_type=F32)


def _dot_nt(a, b):
    return lax.dot_general(a, b, (((1,), (1,)), ((), ())), preferred_element_type=F32)


def _split2(x):
    hi = x.astype(BF16)
    lo = (x - hi.astype(F32)).astype(BF16)
    return hi, lo


def _rms(x):
    return x * lax.rsqrt(jnp.mean(x * x, axis=-1, keepdims=True) + NORM_EPS)


def _load_rows(ref):
    return jnp.concatenate([ref[s] for s in range(ref.shape[0])], axis=0)


def _store_rows(ref, val):
    tm = ref.shape[1]
    for s in range(ref.shape[0]):
        ref[s] = val[s * tm:(s + 1) * tm, :]


def _with_ones_row(vt):
    rows = lax.broadcasted_iota(jnp.int32, (V_ROWS - V_DIM, vt.shape[1]), 0)
    return jnp.concatenate([vt, jnp.where(rows == 0, 1.0, 0.0).astype(BF16)], axis=0)


def _mod_kernel(c_ref, w_ref, b_ref, o_ref):
    a = jax.nn.silu(c_ref[...])
    a_hi, a_lo = _split2(a)
    w_hi, w_lo = _split2(w_ref[...])
    o_ref[...] = _dot(a_hi, w_hi) + _dot(a_lo, w_hi) + _dot(a_hi, w_lo) + b_ref[...]


def _mod_call(cond8, w_ada, b_ada):
    depth, d, n = w_ada.shape
    return pl.pallas_call(
        _mod_kernel,
        grid=(depth, n // MOD_TILE),
        in_specs=[pl.BlockSpec((8, d), lambda l, j: (0, 0)),
                  pl.BlockSpec((None, d, MOD_TILE), lambda l, j: (l, 0, j)),
                  pl.BlockSpec((None, 1, MOD_TILE), lambda l, j: (l, 0, j))],
        out_specs=pl.BlockSpec((None, 8, MOD_TILE), lambda l, j: (l, 0, j)),
        out_shape=jax.ShapeDtypeStruct((depth, 8, n), F32),
        compiler_params=_cp(40),
        name="adaln_mod",
    )(cond8, w_ada, b_ada.reshape(depth, 1, n))


def _proj_kernel(*refs, rope):
    if rope:
        (x_ref, sh_ref, sc_ref, g_ref, win_ref, qg_ref, kvg_ref, wq_ref, wkn_ref, wvt_ref,
         ck_ref, sk_ref, cr_ref, sr_ref, cc_ref, scl_ref,
         u_ref, va_ref, pin_ref, qt_ref, kh_ref, vt_ref) = refs
    else:
        (x_ref, sh_ref, sc_ref, g_ref, win_ref, qg_ref, kvg_ref, wq_ref, wkn_ref, wvt_ref,
         u_ref, va_ref, pin_ref, qt_ref, kh_ref, vt_ref, ckv_ref, kr_ref) = refs
    nseq, tm = x_ref.shape[0], x_ref.shape[1]
    ps = min(tm, PROJ_SUB)
    blocks = [(s, t0) for s in range(nseq) for t0 in range(0, tm, ps)]

    projs = []
    for s, t0 in blocks:
        x = x_ref[s, t0:t0 + ps, :]
        h = _rms(x) * g_ref[...] * (1.0 + sc_ref[...]) + sh_ref[...]
        projs.append(_dot(h.astype(BF16), win_ref[...]))

    for (s, t0), proj in zip(blocks, projs):
        rows = (s, slice(t0, t0 + ps))
        u_ref[rows] = jax.nn.gelu(proj[:, 0:A_WIDTH])
        va_ref[rows] = jax.nn.gelu(proj[:, A_WIDTH:OFF_P]).astype(BF16)
        pin_ref[rows] = proj[:, OFF_P:OFF_CQ]
        cq = _rms(proj[:, OFF_CQ:OFF_CKV]) * qg_ref[...]
        ckv = _rms(proj[:, OFF_CKV:OFF_KR]) * kvg_ref[...]
        kr = proj[:, OFF_KR:OFF_KR + LANES]
        if not rope:
            ckv_ref[rows] = ckv
            kr_ref[rows] = kr[:, 0:QK_ROPE]

        qt = _dot_nt(wq_ref[...], cq.astype(BF16))
        ckv_b = ckv.astype(BF16)
        kn = _dot(ckv_b, wkn_ref[...])
        vt = _dot_nt(wvt_ref[...], ckv_b)

        if rope:
            lane = lax.broadcasted_iota(jnp.int32, kr.shape, 1)
            half = ROPE_AXIS // 2
            swapped = jnp.where((lane & half) == 0,
                                pltpu.roll(kr, LANES - half, axis=1), pltpu.roll(kr, half, axis=1))
            kr = kr * ck_ref[t0:t0 + ps, :] + swapped * sk_ref[t0:t0 + ps, :]
            cr, sr = cr_ref[:, t0:t0 + ps], sr_ref[:, t0:t0 + ps]
            cc, scl = cc_ref[:, t0:t0 + ps], scl_ref[:, t0:t0 + ps]
        kr_shift = pltpu.roll(kr, QK_NOPE, axis=1)

        for hd in range(C_HEADS):
            r0 = hd * HEAD_PAD
            q_h = qt[r0:r0 + HEAD_PAD, :]
            if rope:
                b0 = QK_NOPE
                a, b = q_h[b0:b0 + 8, :], q_h[b0 + 8:b0 + 16, :]
                c, d = q_h[b0 + 16:b0 + 24, :], q_h[b0 + 24:b0 + 32, :]
                q_h = jnp.concatenate(
                    [q_h[0:b0, :], a * cr - b * sr, a * sr + b * cr, c * cc - d * scl, c * scl + d * cc,
                     q_h[b0 + 32:, :]], axis=0)
            q_h = (q_h * Q_PRESCALE).astype(BF16)
            qw = qt_ref.shape[-1]
            for j in range(ps // qw):
                qt_ref[s, t0 // qw + j, hd] = q_h[:, j * qw:(j + 1) * qw]
            kh_ref[s, hd, t0:t0 + ps, :] = (kn[:, r0:r0 + HEAD_PAD] + kr_shift).astype(BF16)
            vt_ref[s, hd, :, t0:t0 + ps] = _with_ones_row(vt[hd * V_DIM:(hd + 1) * V_DIM, :].astype(BF16))


def _seq_tiling(n, row_step, rows=None):
    rows = ROW_TILE if rows is None else rows
    tm = min(n, rows)
    return (rows // tm if row_step == 0 else 1), tm


def _proj_call(x, modl, row0, row_step, lw, tabs, layer):
    nb, n, d = x.shape
    sb, tm = _seq_tiling(n, row_step)
    rope = tabs is not None
    assert not rope or sb == 1
    qw = min(n, ATTN_TQ)
    tok = lambda c: pl.BlockSpec((sb, tm, c), lambda b, i: (b, i, 0))
    const = lambda a: _layer_spec(a, layer)
    modspec = lambda j: pl.BlockSpec((None, None, None, 1, d),
                                     lambda b, i: (layer, row0 + b * sb * row_step, j, 0, 0))
    ins = [x, modl, modl, lw["norm1_g"], lw["w_in"], lw["q_norm_g"], lw["kv_norm_g"],
           lw["wq_t"], lw["wkn"], lw["wv_t"]]
    in_specs = [tok(d), modspec(0), modspec(1), const(lw["norm1_g"]), const(lw["w_in"]),
                const(lw["q_norm_g"]), const(lw["kv_norm_g"]), const(lw["wq_t"]), const(lw["wkn"]),
                const(lw["wv_t"])]
    if rope:
        cos_k, sin_k, cos_r, sin_r, cos_c, sin_c = tabs
        ins += [cos_k, sin_k, cos_r, sin_r, cos_c, sin_c]
        in_specs += [pl.BlockSpec((tm, LANES), lambda b, i: (i, 0))] * 2
        in_specs += [pl.BlockSpec((8, tm), lambda b, i: (0, i))] * 4
    out_shape = [jax.ShapeDtypeStruct((nb, n, A_WIDTH), F32),
                 jax.ShapeDtypeStruct((nb, n, A_WIDTH), BF16),
                 jax.ShapeDtypeStruct((nb, n, B_WIDTH), F32),
                 jax.ShapeDtypeStruct((nb, n // qw, C_HEADS, HEAD_PAD, qw), BF16),
                 jax.ShapeDtypeStruct((nb, C_HEADS, n, HEAD_PAD), BF16),
                 jax.ShapeDtypeStruct((nb, C_HEADS, V_ROWS,n), BF16),
                 jax.ShapeDtypeStruct((nb, n, KV_LORA), F32),
                 jax.ShapeDtypeStruct((nb, n, QK_ROPE), F32)]
    out_specs = [tok(A_WIDTH), tok(A_WIDTH), tok(B_WIDTH),
                 pl.BlockSpec((sb, tm // qw, C_HEADS, HEAD_PAD, qw), lambda b, i: (b, i, 0, 0, 0)),
                 pl.BlockSpec((sb, C_HEADS, tm, HEAD_PAD), lambda b, i: (b, 0, i, 0)),
                 pl.BlockSpec((sb, C_HEADS, V_ROWS, tm), lambda b, i: (b, 0, 0, i)),
                 tok(KV_LORA), tok(QK_ROPE)]
    if rope:
        out_shape, out_specs = out_shape[:-2], out_specs[:-2]
    return pl.pallas_call(
        functools.partial(_proj_kernel, rope=rope),
        grid=(nb // sb, n // tm), in_specs=in_specs, out_specs=out_specs, out_shape=out_shape,
        compiler_params=_cp(48), name="in_proj_rope" if rope else "in_proj",
    )(*ins)


def _kvexp_kernel(ckv_ref, kr_ref, wkn_ref, wvt_ref, place_ref, kh_ref, vt_ref):
    ckv_b = ckv_ref[...].astype(BF16)
    kn = _dot(ckv_b, wkn_ref[...])
    vt = _dot_nt(wvt_ref[...], ckv_b)
    kr_placed = _dot(kr_ref[...].astype(BF16), place_ref[...])
    for hd in range(C_HEADS):
        r0 = hd * HEAD_PAD
        kh_ref[hd] = (kn[:, r0:r0 + HEAD_PAD] + kr_placed).astype(BF16)
        vt_ref[hd] = _with_ones_row(vt[hd * V_DIM:(hd + 1) * V_DIM, :].astype(BF16))


def _kvexp_call(cache_ckv, cache_krope, wkn, wv_t):
    nb, depth, m, r = cache_ckv.shape
    place = np.zeros((QK_ROPE, HEAD_PAD), np.float32)
    place[np.arange(QK_ROPE), QK_NOPE + np.arange(QK_ROPE)] = 1.0
    place = jnp.asarray(place, BF16)
    return pl.pallas_call(
        _kvexp_kernel,
        grid=(depth, nb),
        in_specs=[pl.BlockSpec((None, None, m, r), lambda l, b: (b, l, 0, 0)),
                  pl.BlockSpec((None, None, m, QK_ROPE), lambda l, b: (b, l, 0, 0)),
                  pl.BlockSpec((None,) + wkn.shape[1:], lambda l, b: (l, 0, 0)),
                  pl.BlockSpec((None,) + wv_t.shape[1:], lambda l, b: (l, 0, 0)),
                  pl.BlockSpec(place.shape, lambda l, b: (0, 0))],
        out_specs=[pl.BlockSpec((None, None, C_HEADS, m, HEAD_PAD), lambda l, b: (l, b, 0, 0, 0)),
                   pl.BlockSpec((None, None, C_HEADS, V_ROWS,m), lambda l, b: (l, b, 0, 0, 0))],
        out_shape=[jax.ShapeDtypeStruct((depth, nb, C_HEADS, m, HEAD_PAD), BF16),
                   jax.ShapeDtypeStruct((depth, nb, C_HEADS, V_ROWS,m), BF16)],
        compiler_params=_cp(32), name="cache_kv_expand",
    )(cache_ckv, cache_krope, wkn, wv_t, place)


def _mix_kernel(u_ref, va_ref, pin_ref, ws_ref, bs_ref, band_ref, wp_ref, ps_ref,
                oa_ref, ob_ref, pad_ref):
    n = u_ref.shape[0]
    width = u_ref.shape[1]
    zero_rows = jnp.zeros((POOL_HALO, width), F32)
    pad_ref[0:POOL_HALO, :] = zero_rows
    pad_ref[n + POOL_HALO:n + 2 * POOL_HALO, :] = zero_rows
    pad_ref[POOL_HALO:n + POOL_HALO, :] = pin_ref[...]
    ngrp = len(POOL_WINDOWS)
    unroll = min(MIX_UNROLL, n // CHUNK)
    row = lax.broadcasted_iota(jnp.int32, (CHUNK, width), 0)
    grp_row = lax.broadcasted_iota(jnp.int32, (1, width), 1) // POOL_GROUP
    left = jnp.zeros((1, width), jnp.int32)
    right1 = jnp.zeros((1, width), jnp.int32)
    for g, w in enumerate(POOL_WINDOWS):
        left = jnp.where(grp_row == g, w // 2, left)
        right1 = jnp.where(grp_row == g, w - w // 2, right1)

    def group_masks(nrows):
        grp = lax.broadcasted_iota(jnp.int32, (nrows, width), 1) // POOL_GROUP
        return [jnp.where(grp == g, 1.0, 0.0).astype(BF16) for g in range(ngrp)]

    masks_v = group_masks(CHUNK)
    masks_w = group_masks(CHUNK + 2 * POOL_HALO)

    def group_stack(x, masks):
        return jnp.concatenate([x * m for m in masks], axis=0)

    def first_dots(c):
        r0 = pl.multiple_of(c * CHUNK, CHUNK)
        v_stack = group_stack(va_ref[pl.ds(r0, CHUNK), :], masks_v)
        s = _dot(ws_ref[...], v_stack) + bs_ref[...]
        oa_ref[pl.ds(r0, CHUNK), :] = (u_ref[pl.ds(r0, CHUNK), :] * s).astype(BF16)
        win = pad_ref[pl.ds(r0, CHUNK + 2 * POOL_HALO), :]
        w_hi, w_lo = _split2(win)
        tot = (_dot(band_ref[...], group_stack(w_hi, masks_w))
               + _dot(band_ref[...], group_stack(w_lo, masks_w)))
        t = r0 + row
        cnt = jnp.minimum(t + right1, n) - jnp.maximum(t - left, 0)
        p = win[POOL_HALO:POOL_HALO + CHUNK, :]
        return r0, (tot / cnt.astype(F32) - p).astype(BF16)

    def body(j, carry):
        diffs = [first_dots(j * unroll + r) for r in range(unroll)]
        for r0, diff in diffs:
            ob_ref[pl.ds(r0, CHUNK), :] = (_dot(diff, wp_ref[...]) * ps_ref[...]).astype(BF16)
        return carry

    lax.fori_loop(0, n // (CHUNK * unroll), body, 0)


def _pool_bands():
    rows = CHUNK + 2 * POOL_HALO
    band = np.zeros((len(POOL_WINDOWS), CHUNK, rows), np.float32)
    i = np.arange(CHUNK)[:, None]
    j = np.arange(rows)[None, :]
    for g, w in enumerate(POOL_WINDOWS):
        left = w // 2
        right = w - 1 - left
        band[g] = ((j >= i + POOL_HALO - left) & (j <= i + POOL_HALO + right)).astype(np.float32)
    return jnp.asarray(band.transpose(1, 0, 2).reshape(CHUNK, -1), BF16)


def _mix_call(u, va, pin, lw, band, layer):
    nb, n, w = u.shape
    seq = lambda: pl.BlockSpec((None, n, w), lambda b: (b, 0, 0))
    const = lambda a: _layer_spec(a, layer)
    return pl.pallas_call(
        _mix_kernel,
        grid=(nb,),
        in_specs=[seq(), seq(), seq(), const(lw["w_s"]), const(lw["b_s"]),
                  pl.BlockSpec(band.shape, lambda b: (0, 0)),
                  const(lw["w_pool"]), const(lw["pool_scale"])],
        out_specs=[seq(), seq()],
        out_shape=[jax.ShapeDtypeStruct((nb, n, w), BF16)] * 2,
        scratch_shapes=[pltpu.VMEM((n + 2 * POOL_HALO, w), F32)],
        compiler_params=_cp(48), name="mixers_ab",
    )(u, va, pin, lw["w_s"], lw["b_s"], band, lw["w_pool"], lw["pool_scale"])


def _attn_kernel(*refs, seg_lens):
    nseg = len(seg_lens)
    qt_ref = refs[0]
    k_refs = refs[1:1 + nseg]
    v_refs = refs[1 + nseg:1 + 2 * nseg]
    o_ref, ot_ref = refs[1 + 2 * nseg:3 + 2 * nseg]
    s_bufs = refs[3 + 2 * nseg:]
    nsq, nqb, _, _, tq = qt_ref.shape
    nitem = nsq * nqb * C_HEADS
    head_bits = C_HEADS.bit_length() - 1
    qb_bits = nqb.bit_length() - 1
    assert C_HEADS == 1 << head_bits and nqb == 1 << qb_bits

    def unpack(it):
        blk = it >> head_bits
        return blk >> qb_bits, blk, blk & (nqb - 1), it & (C_HEADS - 1)

    def scores_into(it, s_ref):
        sq, _, qb, hd = unpack(it)
        q = qt_ref[sq, qb, hd]
        mx = None
        r0 = 0
        for k_ref, m in zip(k_refs, seg_lens):
            for c0 in range(0, m, KEY_CHUNK):
                kc = min(KEY_CHUNK, m - c0)
                s = _dot(k_ref[sq, hd, c0:c0 + kc, :], q)
                s_ref[r0:r0 + kc, :] = s
                cm = jnp.max(s, axis=0, keepdims=True)
                mx = cm if mx is None else jnp.maximum(mx, cm)
                r0 += kc
        return mx

    def finish(it, s_ref, mx):
        sq, blk, _, hd = unpack(it)
        acc = jnp.zeros((V_ROWS, tq), F32)
        r0 = 0
        for v_ref, m in zip(v_refs, seg_lens):
            for c0 in range(0, m, KEY_CHUNK):
                kc = min(KEY_CHUNK, m - c0)
                p = jnp.exp2(s_ref[r0:r0 + kc, :] - mx).astype(BF16)
                acc = acc + _dot(v_ref[sq, hd, :, c0:c0 + kc], p)
                r0 += kc
        rows = pl.ds(pl.multiple_of(hd * V_DIM, V_DIM), V_DIM)
        ot_ref[blk, rows, :] = acc[0:V_DIM, :] / acc[V_DIM:V_DIM + 1, :]

    nbuf = len(s_bufs)

    def step(t, r, mx):
        mx_next = scores_into(t + 1, s_bufs[(r + 1) % nbuf])
        finish(t, s_bufs[r], mx)
        return mx_next

    def rotation(j, mx):
        for r in range(nbuf):
            mx = step(nbuf * j + r, r, mx)
        return mx

    nsteps = nitem - 1
    mx = lax.fori_loop(0, nsteps // nbuf, rotation, scores_into(0, s_bufs[0]))
    for r in range(nsteps % nbuf):
        mx = step((nsteps // nbuf) * nbuf + r, r, mx)
    finish(nitem - 1, s_bufs[(nitem - 1) % nbuf], mx)
    for sq in range(nsq):
        for qb in range(nqb):
            o_ref[sq, qb * tq:(qb + 1) * tq, :] = ot_ref[sq * nqb + qb].T.astype(BF16)


def _attn_call(qt, segs):
    nb, nblk, _, _, tq = qt.shape
    nqb = min(nblk, ATTN_Q_BLOCKS)
    sb = ATTN_Q_BLOCKS // nqb
    seg_lens = tuple(k.shape[-2] for k, _, _ in segs)

    def seg_spec(a, layer):
        if layer is None:
            return pl.BlockSpec((sb,) + a.shape[1:], lambda b, i: (b, 0, 0, 0))
        return pl.BlockSpec((None, sb) + a.shape[2:], lambda b, i: (layer, b, 0, 0, 0))

    in_specs = [pl.BlockSpec((sb, nqb, C_HEADS, HEAD_PAD, tq), lambda b, i: (b, i, 0, 0, 0))]
    in_specs += [seg_spec(k, layer) for k, _, layer in segs]
    in_specs += [seg_spec(v, layer) for _, v, layer in segs]
    return pl.pallas_call(
        functools.partial(_attn_kernel, seg_lens=seg_lens),
        grid=(nb // sb, nblk // nqb), in_specs=in_specs,
        out_specs=pl.BlockSpec((sb, nqb * tq, C_WIDTH), lambda b, i: (b, i, 0)),
        out_shape=jax.ShapeDtypeStruct((nb, nblk * tq, C_WIDTH), BF16),
        scratch_shapes=([pltpu.VMEM((sb * nqb, C_WIDTH, tq), F32)]
                        + [pltpu.VMEM((sum(seg_lens), tq), F32)] * SCORE_BUFFERS),
        compiler_params=_cp(48), name="latent_attention",
    )(qt, *[k for k, _, _ in segs], *[v for _, v, _ in segs])


def _outproj_mix(rows, x_ref, oa_ref, ob_ref, oc_ref, wo_ref, g1_ref, sh_ref, sc_ref, ng_ref,
                 wr_ref, x1_ref, h_ref, aff_ref, asp_ref):
    mix = (_dot(oa_ref[rows], wo_ref[0:A_WIDTH, :])
           + _dot(ob_ref[rows], wo_ref[A_WIDTH:A_WIDTH + B_WIDTH, :])
           + _dot(oc_ref[rows], wo_ref[A_WIDTH + B_WIDTH:, :]))
    x1 = x_ref[rows] + g1_ref[...] * mix
    x1_ref[rows] = x1
    h = _rms(x1) * ng_ref[...] * (1.0 + sc_ref[...]) + sh_ref[...]
    h_hi, h_lo = _split2(h)
    h_ref[rows] = h_hi
    return h_hi, h_lo


def _outproj_route(rows, h_hi, h_lo, wr_ref, aff_ref, asp_ref):
    part = _dot(h_hi, wr_ref[...])
    logits = part + pltpu.roll(part, LANES - N_EXPERTS, axis=1) + _dot(h_lo, wr_ref[...])
    lane = lax.broadcasted_iota(jnp.int32, logits.shape, 1)
    logits = jnp.where(lane < N_EXPERTS, logits, -jnp.inf)
    ex = jnp.exp(logits - jnp.max(logits, axis=-1, keepdims=True))
    aff = ex / jnp.sum(ex, axis=-1, keepdims=True)
    aff_ref[rows] = aff
    a_hi = aff.astype(BF16).astype(F32)
    r1 = aff - a_hi
    a_lo = r1.astype(BF16).astype(F32)
    a_lo2 = r1 - a_lo
    packed = a_hi + pltpu.roll(a_lo, N_EXPERTS, axis=1) + pltpu.roll(a_lo2, 2 * N_EXPERTS, axis=1)
    asp_ref[rows] = packed.astype(BF16)


def _outproj_kernel(x_ref, *refs):
    nseq, tm = x_ref.shape[0], x_ref.shape[1]
    wr_ref, aff_ref, asp_ref = refs[8], refs[11], refs[12]
    sub = min(tm, SUB_ROWS)
    blocks = [(s, slice(r0, r0 + sub)) for s in range(nseq) for r0 in range(0, tm, sub)]
    splits = [_outproj_mix(rows, x_ref, *refs) for rows in blocks]
    for rows, (h_hi, h_lo) in zip(blocks, splits):
        _outproj_route(rows, h_hi, h_lo, wr_ref, aff_ref, asp_ref)


def _outproj_call(x, oa, ob, oc, modl, row0, row_step, lw, layer):
    nb, n, d = x.shape
    sb, tm = _seq_tiling(n, row_step)
    tok = lambda c: pl.BlockSpec((sb, tm, c), lambda b, i: (b, i, 0))
    const = lambda a: _layer_spec(a, layer)
    modspec = lambda j: pl.BlockSpec((None, None, None, 1, d),
                                     lambda b, i: (layer, row0 + b * sb * row_step, j, 0, 0))
    return pl.pallas_call(
        _outproj_kernel,
        grid=(nb // sb, n // tm),
        in_specs=[tok(d), tok(A_WIDTH), tok(B_WIDTH), tok(C_WIDTH), const(lw["w_out"]),
                  modspec(2), modspec(3), modspec(4), const(lw["norm2_g"]),
                  const(lw["w_router"])],
        out_specs=[tok(d), tok(d), tok(LANES), tok(LANES)],
        out_shape=[jax.ShapeDtypeStruct((nb, n, d), F32), jax.ShapeDtypeStruct((nb, n, d), BF16),
                   jax.ShapeDtypeStruct((nb, n, LANES), F32), jax.ShapeDtypeStruct((nb, n, LANES), BF16)],
        compiler_params=_cp(48), name="out_proj_router",
    )(x, oa, ob, oc, lw["w_out"], modl, modl, modl, lw["norm2_g"], lw["w_router"])


def _route_kernel(aff_ref, pos_ref, post_ref, offs_ref, *, cap):
    nb, n, _ = aff_ref.shape
    c = TOK_TILE
    nck = n // c
    aff = aff_ref[...]
    capf = jnp.float32(cap)

    def search(i, bits):
        cand = bits | jnp.left_shift(jnp.int32(1), jnp.int32(SEARCH_BITS - 1) - i)
        cnt = jnp.sum(jnp.where(aff >= lax.bitcast_convert_type(cand, F32), 1.0, 0.0),
                      axis=1, keepdims=True)
        return jnp.where(cnt >= capf, cand, bits)

    bits = lax.fori_loop(0, SEARCH_BITS, search, jnp.zeros((nb, 1, LANES), jnp.int32))
    thr_all = lax.bitcast_convert_type(bits, F32)
    n_gt = jnp.sum(jnp.where(aff > thr_all, 1.0, 0.0), axis=1, keepdims=True)
    n_tie_all = capf - n_gt
    ri = lax.broadcasted_iota(jnp.int32, (c, c), 0)
    ci = lax.broadcasted_iota(jnp.int32, (c, c), 1)
    tri = jnp.where(ri > ci, 1.0, 0.0).astype(BF16)
    lane_ok = lax.broadcasted_iota(jnp.int32, (c, LANES), 1) < N_EXPERTS
    offs_ref[...] = jnp.zeros(offs_ref.shape, jnp.int32)
    for s in range(nb):
        thr = thr_all[s]
        n_tie = n_tie_all[s]
        tie_seen = jnp.zeros((1, LANES), F32)
        sel_seen = jnp.zeros((1, LANES), F32)
        for k in range(nck):
            kk = aff_ref[s, k * c:(k + 1) * c, :]
            gt = kk > thr
            eq = kk == thr
            eqf = jnp.where(eq, 1.0, 0.0)
            tie_rank = _dot(tri, eqf.astype(BF16)) + tie_seen
            sel = (gt | (eq & (tie_rank < n_tie))) & lane_ok
            self_ = jnp.where(sel, 1.0, 0.0)
            slot = _dot(tri, self_.astype(BF16)) + sel_seen
            pos = jnp.where(sel, slot, -1.0)
            pos_ref[s, k * c:(k + 1) * c, :] = pos
            post_ref[s, :, k * c:(k + 1) * c] = pos.T[0:N_EXPERTS, :]
            offs_ref[s, k:k + 1, :] = sel_seen[:, 0:N_EXPERTS].astype(jnp.int32)
            tie_seen = tie_seen + jnp.sum(eqf, axis=0, keepdims=True)
            sel_seen = sel_seen + jnp.sum(self_, axis=0, keepdims=True)
        offs_ref[s, nck:nck + 1, :] = sel_seen[:, 0:N_EXPERTS].astype(jnp.int32)


def _offs_rows(n):
    return -(-(n // TOK_TILE + 1) // 8) * 8


def _route_call(aff, cap):
    nb, n, _ = aff.shape
    rows = _offs_rows(n)
    return pl.pallas_call(
        functools.partial(_route_kernel, cap=cap),
        grid=(1,),
        in_specs=[pl.BlockSpec((nb, n, LANES), lambda i: (0, 0, 0))],
        out_specs=[pl.BlockSpec((nb, n, LANES), lambda i: (0, 0, 0)),
                   pl.BlockSpec((nb, N_EXPERTS, n), lambda i: (0, 0, 0)),
                   pl.BlockSpec((nb, rows, N_EXPERTS), lambda i: (0, 0, 0))],
        out_shape=[jax.ShapeDtypeStruct((nb, n, LANES), F32),
                   jax.ShapeDtypeStruct((nb, N_EXPERTS, n), F32),
                   jax.ShapeDtypeStruct((nb, rows, N_EXPERTS), jnp.int32)],
        compiler_params=_cp(32), name="ec_route",
    )(aff, )


def _window(offs_ref, b, k, e, rows, win):
    off = offs_ref[(b * rows + k) * N_EXPERTS + e]
    nxt = offs_ref[(b * rows + k + 1) * N_EXPERTS + e]
    base = lax.shift_right_logical(off, SLOT_ALIGN.bit_length() - 1) * SLOT_ALIGN
    nwin = lax.shift_right_logical(nxt - base + (win - 1), win.bit_length() - 1)
    return base, nwin


def _slot_targets(ids, start, cap, win):
    cb = pl.multiple_of(jnp.minimum(start, cap - win), SLOT_ALIGN)
    slot = ids + cb.astype(F32)
    return jnp.where(slot >= start.astype(F32), slot, -2.0), cb


def _gather_kernel(offs_ref, post_ref, h_ref, asp_ref, xs_ref, gs_ref, *, cap, win, rows):
    nseq, tm = h_ref.shape[0], h_ref.shape[1]
    c = TOK_TILE
    b0 = pl.program_id(0) * nseq
    k0 = pl.program_id(1) * (tm // c)

    @pl.when(pl.program_id(1) == 0)
    def _():
        xs_ref[...] = jnp.zeros(xs_ref.shape, xs_ref.dtype)
        gs_ref[...] = jnp.zeros(gs_ref.shape, gs_ref.dtype)

    sub = lax.broadcasted_iota(jnp.int32, (win, 1), 0).astype(F32)

    def one_chunk(s, cc, overflow):
        b = b0 + s
        k = k0 + cc
        cols = slice(cc * c, (cc + 1) * c)
        hk = h_ref[s, cols, :]
        gk = asp_ref[s, cols, :]

        def sel_rows(e, start):
            tgt, cb = _slot_targets(sub, start, cap, win)
            return jnp.where(post_ref[s, e:e + 1, cols] == tgt, 1.0, 0.0).astype(BF16), s * cap + cb

        if overflow:
            for e in range(N_EXPERTS):
                base, nwin = _window(offs_ref, b, k, e, rows, win)

                def extra(w, carry, e=e, base=base):
                    p, cbw = sel_rows(e, base + w * win)
                    xs_ref[e, pl.ds(cbw, win), :] += _dot(p, hk).astype(BF16)
                    gs_ref[e, pl.ds(cbw, win), :] += _dot(p, gk)
                    return carry

                lax.fori_loop(1, nwin, extra, 0)
            return

        for g0 in range(0, N_EXPERTS, GATHER_GROUP):
            parts, cbs = [], []
            for e in range(g0, g0 + GATHER_GROUP):
                base, _ = _window(offs_ref, b, k, e, rows, win)
                p, cb = sel_rows(e, base)
                parts.append(p)
                cbs.append(cb)
            p_grp = jnp.concatenate(parts, axis=0)
            r_grp = _dot(p_grp, hk).astype(BF16)
            g_grp = _dot(p_grp, gk)
            for j, cb in enumerate(cbs):
                xs_ref[g0 + j, pl.ds(cb, win), :] += r_grp[j * win:(j + 1) * win, :]
                gs_ref[g0 + j, pl.ds(cb, win), :] += g_grp[j * win:(j + 1) * win, :]

    for overflow in (False, True):
        for s in range(nseq):
            for cc in range(tm // c):
                one_chunk(s, cc, overflow)


def _gather_call(offs, post, h, asp, cap, win):
    nb, n, d = h.shape
    sb, tm = _seq_tiling(n, 0, EC_ROWS)
    rows = _offs_rows(n)
    grid_spec = pltpu.PrefetchScalarGridSpec(
        num_scalar_prefetch=1, grid=(nb // sb, n // tm),
        in_specs=[pl.BlockSpec((sb, N_EXPERTS, tm), lambda b, k, o: (b, 0, k)),
                  pl.BlockSpec((sb, tm, d), lambda b, k, o: (b, k, 0)),
                  pl.BlockSpec((sb, tm, LANES), lambda b, k, o: (b, k, 0))],
        out_specs=[pl.BlockSpec((N_EXPERTS, sb * cap, d), lambda b, k, o: (0, b, 0)),
                   pl.BlockSpec((N_EXPERTS, sb * cap, LANES), lambda b, k, o: (0, b, 0))])
    return pl.pallas_call(
        functools.partial(_gather_kernel, cap=cap, win=win, rows=rows),
        grid_spec=grid_spec,
        out_shape=[jax.ShapeDtypeStruct((N_EXPERTS, nb * cap, d), BF16),
                   jax.ShapeDtypeStruct((N_EXPERTS, nb * cap, LANES), F32)],
        compiler_params=_cp(48), name="ec_gather",
    )(offs.reshape(-1), post, h, asp)


def _ffn_kernel(xa_ref, xb_ref, ga_ref, gb_ref, w1_ref, w3_ref, w2_ref, ya_ref, yb_ref,
                acca_ref, accb_ref):
    e = pl.program_id(0)
    f = pl.program_id(1)

    @pl.when((e == 0) & (f == 0))
    def _():
        acca_ref[...] = jnp.zeros(acca_ref.shape, F32)
        accb_ref[...] = jnp.zeros(accb_ref.shape, F32)

    w1 = w1_ref[...].astype(BF16)
    w3 = w3_ref[...].astype(BF16)
    w2 = w2_ref[...].astype(BF16)
    groups = ((xa_ref, ga_ref, ya_ref, acca_ref), (xb_ref, gb_ref, yb_ref, accb_ref))
    hids = [(jax.nn.silu(_dot(x_ref[...], w1)) * _dot(x_ref[...], w3)).astype(BF16)
            for x_ref, _, _, _ in groups]
    for (x_ref, g_ref, y_ref, acc_ref), hid in zip(groups, hids):
        acc = jnp.where(f > 0, acc_ref[...], 0.0) + _dot(hid, w2)
        acc_ref[...] = acc
        g = g_ref[...]
        lane = lax.broadcasted_iota(jnp.int32, g.shape, 1)
        keep = ((lane % N_EXPERTS) == e) & (lane < 3 * N_EXPERTS)
        gate = jnp.sum(jnp.where(keep, g, 0.0), axis=-1, keepdims=True)
        y_ref[...] = (acc * gate).astype(BF16)


def _ffn_call(xa, xb, ga, gb, w1, w3, w2, layer):
    ne, ra, d = xa.shape
    rb = xb.shape[1]
    ff = w1.shape[3]
    tf = FF_TILE
    rowsp = lambda r, c: pl.BlockSpec((None, r, c), lambda e, f: (e, 0, 0))
    return pl.pallas_call(
        _ffn_kernel,
        grid=(ne, ff // tf),
        in_specs=[rowsp(ra, d), rowsp(rb, d), rowsp(ra, LANES), rowsp(rb, LANES),
                  pl.BlockSpec((None, None, d, tf), lambda e, f: (layer, e, 0, f)),
                  pl.BlockSpec((None, None, d, tf), lambda e, f: (layer, e, 0, f)),
                  pl.BlockSpec((None, None, tf, d), lambda e, f: (layer, e, f, 0))],
        out_specs=[rowsp(ra, d), rowsp(rb, d)],
        out_shape=[jax.ShapeDtypeStruct((ne, ra, d), BF16), jax.ShapeDtypeStruct((ne, rb, d), BF16)],
        scratch_shapes=[pltpu.VMEM((ra, d), F32), pltpu.VMEM((rb, d), F32)],
        compiler_params=_cp(56), name="expert_swiglu",
    )(xa, xb, ga, gb, w1, w3, w2)


def _combine_kernel(offs_ref, pos_ref, y_ref, x_ref, g2_ref, ex_ref, *rest, cap, win, rows, last):
    if last:
        fg_ref, o_ref, on_ref = rest
    else:
        (o_ref,) = rest
    nseq, tm = x_ref.shape[0], x_ref.shape[1]
    c = TOK_TILE
    b0 = pl.program_id(0) * nseq
    k0 = pl.program_id(1) * (tm // c)
    per = MXU_DEPTH // win
    depth = per * win
    lane = lax.broadcasted_iota(jnp.int32, (1, depth), 1)
    lane_grp = lane // win
    lane_in = (lane % win).astype(F32)
    lane1 = lax.broadcasted_iota(jnp.int32, (1, win), 1).astype(F32)

    def one_chunk(s, cc, overflow, pos_wide=None):
        b = b0 + s
        k = k0 + cc
        rws = (s, slice(cc * c, (cc + 1) * c))
        if overflow:
            for e in range(N_EXPERTS):
                base, nwin = _window(offs_ref, b, k, e, rows, win)

                def extra(w, carry, e=e, base=base):
                    tgt_w, cbw = _slot_targets(lane1, base + w * win, cap, win)
                    pt = jnp.where(pos_ref[rws][:, e:e + 1] == tgt_w, 1.0, 0.0).astype(BF16)
                    o_ref[rws] += g2_ref[...] * _dot(pt, y_ref[e, pl.ds(s * cap + cbw, win), :])
                    return carry

                lax.fori_loop(1, nwin, extra, 0)
            if last:
                on_ref[rws] = _rms(o_ref[rws]) * fg_ref[...]
            return
        tot = None
        for gi, g0 in enumerate(range(0, N_EXPERTS, per)):
            tgt = jnp.full((1, depth), -2.0, F32)
            ys = []
            for j in range(per):
                base, _ = _window(offs_ref, b, k, g0 + j, rows, win)
                tgt_e, cb = _slot_targets(lane_in, base, cap, win)
                tgt = jnp.where(lane_grp == j, tgt_e, tgt)
                ys.append(y_ref[g0 + j, pl.ds(s * cap + cb, win), :])
            pt = jnp.where(pos_wide[:, gi * depth:(gi + 1) * depth] == tgt, 1.0, 0.0).astype(BF16)
            d = _dot(pt, jnp.concatenate(ys, axis=0))
            tot = d if tot is None else tot + d
        o_ref[rws] = x_ref[rws] + g2_ref[...] * tot

    chunks = [(s, cc) for s in range(nseq) for cc in range(tm // c)]
    wides = [_dot(pos_ref[s, cc * c:(cc + 1) * c, :].astype(BF16), ex_ref[...]) for s, cc in chunks]
    for (s, cc), pos_wide in zip(chunks, wides):
        one_chunk(s, cc, False, pos_wide)
    for s, cc in chunks:
        one_chunk(s, cc, True)


def _combine_call(offs, pos, y, x1, modl, row0, row_step, final_g, cap, win, layer):
    nb, n, d = x1.shape
    sb, tm = _seq_tiling(n, row_step, EC_ROWS)
    rows = _offs_rows(n)
    last = final_g is not None
    assert cap <= 256
    spread = np.zeros((LANES, N_EXPERTS * win), np.float32)
    for e in range(N_EXPERTS):
        spread[e, e * win:(e + 1) * win] = 1.0
    spread = jnp.asarray(spread, BF16)
    tok = pl.BlockSpec((sb, tm, d), lambda b, k, o: (b, k, 0))
    in_specs = [pl.BlockSpec((sb, tm, LANES), lambda b, k, o: (b, k, 0)),
                pl.BlockSpec((N_EXPERTS, sb * cap, d), lambda b, k, o: (0, b, 0)),
                tok,
                pl.BlockSpec((None, None, None, 1, d),
                             lambda b, k, o: (layer, row0 + b * sb * row_step, 5, 0, 0)),
                pl.BlockSpec(spread.shape, lambda b, k, o: (0, 0))]
    ins = [offs.reshape(-1), pos, y, x1, modl, spread]
    if last:
        in_specs.append(pl.BlockSpec((1, d), lambda b, k, o: (0, 0)))
        ins.append(final_g)
    nout = 2 if last else 1
    grid_spec = pltpu.PrefetchScalarGridSpec(
        num_scalar_prefetch=1, grid=(nb // sb, n // tm), in_specs=in_specs, out_specs=[tok] * nout)
    return pl.pallas_call(
        functools.partial(_combine_kernel, cap=cap, win=win, rows=rows, last=last),
        grid_spec=grid_spec,
        out_shape=[jax.ShapeDtypeStruct((nb, n, d), F32)] * nout,
        compiler_params=_cp(56), name="ec_combine_final" if last else "ec_combine",
    )(*ins)


def _prep_weights(w_in, w_s, b_s, w_pool, pool_scale, q_norm_g, w_qb, kv_norm_g, w_kvb, w_out,
                  w_router, norm1_g, norm2_g):
    depth, d, cols = w_in.shape
    row = lambda a: a.reshape(depth, 1, -1)
    w_in_p = jnp.pad(w_in, ((0, 0), (0, 0), (0, PROJ_PAD - cols))).astype(BF16)
    wq = w_qb.reshape(depth, Q_LORA, C_HEADS, QK_NOPE + QK_ROPE)
    wq = jnp.pad(wq, ((0, 0), (0, 0), (0, 0), (0, HEAD_PAD - QK_NOPE - QK_ROPE)))
    wq_t = wq.reshape(depth, Q_LORA, C_HEADS * HEAD_PAD).transpose(0, 2, 1).astype(BF16)
    wkv = w_kvb.reshape(depth, KV_LORA, C_HEADS, QK_NOPE + V_DIM)
    wkn = jnp.pad(wkv[..., :QK_NOPE], ((0, 0), (0, 0), (0, 0), (0, HEAD_PAD - QK_NOPE)))
    wkn = wkn.reshape(depth, KV_LORA, C_HEADS * HEAD_PAD).astype(BF16)
    wv_t = wkv[..., QK_NOPE:].reshape(depth, KV_LORA, C_WIDTH).transpose(0, 2, 1).astype(BF16)
    assert A_HEADS == len(POOL_WINDOWS) and A_HEAD_DIM == POOL_GROUP
    ws_cat = w_s.transpose(0, 2, 1, 3).reshape(depth, CHUNK, A_HEADS * CHUNK).astype(BF16)
    b_rep = jnp.repeat(b_s.transpose(0, 2, 1), A_HEAD_DIM, axis=2)
    eye = jnp.eye(len(POOL_WINDOWS), dtype=F32)
    wp_bd = jnp.einsum("gh,lgcd->lgchd", eye, w_pool).reshape(depth, B_WIDTH, B_WIDTH).astype(BF16)
    wr_hi = w_router.astype(BF16)
    wr_lo = (w_router - wr_hi.astype(F32)).astype(BF16)
    wr = jnp.pad(jnp.concatenate([wr_hi, wr_lo], axis=2), ((0, 0), (0, 0), (0, LANES - 2 * N_EXPERTS)))
    return dict(
        norm1_g=row(norm1_g), norm2_g=row(norm2_g), w_in=w_in_p, q_norm_g=row(q_norm_g),
        kv_norm_g=row(kv_norm_g), wq_t=wq_t, wkn=wkn, wv_t=wv_t, w_s=ws_cat, b_s=b_rep, w_pool=wp_bd,
        pool_scale=row(pool_scale), w_out=w_out.astype(BF16), w_router=wr)


def _rope_tables(n):
    rows = n // GRID_W
    row = jnp.repeat(jnp.arange(rows), GRID_W).astype(F32)
    col = jnp.broadcast_to(jnp.arange(GRID_W), (rows, GRID_W)).reshape(-1).astype(F32)
    inv = 1.0 / (ROPE_THETA ** (jnp.arange(0, ROPE_AXIS, 2, dtype=F32) / ROPE_AXIS))
    ang_r = row[:, None] * inv[None, :]
    ang_c = col[:, None] * inv[None, :]
    cos_r, sin_r, cos_c, sin_c = jnp.cos(ang_r), jnp.sin(ang_r), jnp.cos(ang_c), jnp.sin(ang_c)
    pad = jnp.zeros((n, LANES - QK_ROPE), F32)
    cos_k = jnp.concatenate([cos_r, cos_r, cos_c, cos_c, pad], axis=1)
    sin_k = jnp.concatenate([-sin_r, sin_r, -sin_c, sin_c, pad], axis=1)
    return cos_k, sin_k, cos_r.T, sin_r.T, cos_c.T, sin_c.T


def kernel(x_prompt, x_sample, cache_ckv, cache_krope, c, c_ctx, w_ada, b_ada, norm1_g, norm2_g, w_in, w_s, b_s, w_pool, pool_scale, q_norm_g, w_qb, kv_norm_g, w_kvb, w_out, w_router, w_e1, w_e3, w_e2, final_norm_g):
    depth, d, _ = w_ada.shape
    nb_c, n_c, _ = x_prompt.shape
    nb_s, n_s, _ = x_sample.shape
    assert 1 + nb_s <= 8
    cond8 = jnp.concatenate([c_ctx[None, :], c, jnp.zeros((8 - 1 - nb_s, d), F32)], axis=0)
    mod = _mod_call(cond8, w_ada, b_ada).reshape(depth, 8, 6, 1, d)
    lw = _prep_weights(w_in, w_s, b_s, w_pool, pool_scale, q_norm_g, w_qb, kv_norm_g, w_kvb, w_out,
                       w_router, norm1_g, norm2_g)
    k_cache, vt_cache = _kvexp_call(cache_ckv, cache_krope, lw["wkn"], lw["wv_t"])
    tabs = _rope_tables(n_s)
    band = _pool_bands()
    final_g = final_norm_g.reshape(1, d)
    cap_c = EC_CAPACITY * n_c // N_EXPERTS
    cap_s = EC_CAPACITY * n_s // N_EXPERTS
    win_c = min(cap_c, 64)
    win_s = min(cap_s, 64)

    xc, xs = x_prompt, x_sample
    yc = ys = None
    ckvs, kropes = [], []
    for l in range(depth):
        uc, vac, pinc, qtc, khc, vtc, ckv_c, kr_c = _proj_call(xc, mod, 0, 0, lw, None, l)
        us, vas, pins, qts, khs, vts = _proj_call(xs, mod, 1, 1, lw, tabs, l)
        ckvs.append(ckv_c)
        kropes.append(kr_c)
        oac, obc = _mix_call(uc, vac, pinc, lw, band, l)
        oas, obs = _mix_call(us, vas, pins, lw, band, l)
        occ = _attn_call(qtc, [(khc, vtc, None)])
        ocs = _attn_call(qts, [(k_cache, vt_cache, l), (khs, vts, None)])
        x1c, hc, affc, aspc = _outproj_call(xc, oac, obc, occ, mod, 0, 0, lw, l)
        x1s, hs, affs, asps = _outproj_call(xs, oas, obs, ocs, mod, 1, 1, lw, l)
        posc, postc, offc = _route_call(affc, cap_c)
        poss, posts, offs = _route_call(affs, cap_s)
        xgc, ggc = _gather_call(offc, postc, hc, aspc, cap_c, win_c)
        xgs, ggs = _gather_call(offs, posts, hs, asps, cap_s, win_s)
        ygs, ygc = _ffn_call(xgs, xgc, ggs, ggc, w_e1, w_e3, w_e2, l)
        fg = final_g if l == depth - 1 else None
        outc = _combine_call(offc, posc, ygc, x1c, mod, 0, 0, fg, cap_c, win_c, l)
        outs = _combine_call(offs, poss, ygs, x1s, mod, 1, 1, fg, cap_s, win_s, l)
        xc, xs = outc[0], outs[0]
    yc, ys = outc[1], outs[1]
    new_ckv = jnp.stack(ckvs, axis=1)
    new_krope = jnp.stack(kropes, axis=1)
    return (yc, ys, new_ckv, new_krope)
```

```python
import functools
import math

import numpy as np
import jax
import jax.numpy as jnp
from jax import lax
from jax.experimental import pallas as pl
from jax.experimental.pallas import tpu as pltpu

F32 = jnp.float32
BF16 = jnp.bfloat16

GRID_W = 64
A_HEADS = 4
A_HEAD_DIM = 64
A_WIDTH = A_HEADS * A_HEAD_DIM
CHUNK = 128
POOL_WINDOWS = (2, 4, 8, 16)
POOL_GROUP = 64
B_WIDTH = len(POOL_WINDOWS) * POOL_GROUP
C_HEADS = 8
QK_NOPE = 64
QK_ROPE = 32
V_DIM = 64
Q_LORA = 384
KV_LORA = 256
C_WIDTH = C_HEADS * V_DIM
ROPE_AXIS = QK_ROPE // 2
ROPE_THETA = 10000.0
ATTN_SCALE = (QK_NOPE + QK_ROPE) ** -0.5
N_EXPERTS = 16
EC_CAPACITY = 2
NORM_EPS = 1e-6

LANES = 128
HEAD_PAD = 128
PROJ_PAD = 1536
OFF_P = 2 * A_WIDTH
OFF_CQ = OFF_P + B_WIDTH
OFF_CKV = OFF_CQ + Q_LORA
OFF_KR = OFF_CKV + KV_LORA
MIX_UNROLL = 16
POOL_HALO = 8
TOK_TILE = 256
ATTN_TQ = 256
ROW_TILE = 1024
PROJ_SUB = 256
EC_ROWS = 1024
SUB_ROWS = 256
V_ROWS = V_DIM + 16
KEY_CHUNK = 512
SCORE_BUFFERS = 2
ATTN_Q_BLOCKS = 4
Q_PRESCALE = ATTN_SCALE * math.log2(math.e)
FF_TILE = 512
MOD_TILE = 1536
SEARCH_BITS = 31
SLOT_ALIGN = 16
GATHER_GROUP = 4
MXU_DEPTH = 256


def _cp(vmem_mb):
    return pltpu.CompilerParams(vmem_limit_bytes=vmem_mb * 1024 * 1024)


def _layer_spec(a, layer):
    return pl.BlockSpec((None,) + a.shape[1:], lambda *_: (layer,) + (0,) * (a.ndim - 1))


def _dot(a, b):
    return jnp.dot(a, b, preferred_element_type=F32)


def _dot_nt(a, b):
    return lax.dot_general(a, b, (((1,), (1,)), ((), ())), preferred_element_type=F32)


def _split2(x):
    hi = x.astype(BF16)
    lo = (x - hi.astype(F32)).astype(BF16)
    return hi, lo


def _rms(x):
    return x * lax.rsqrt(jnp.mean(x * x, axis=-1, keepdims=True) + NORM_EPS)


def _load_rows(ref):
    return jnp.concatenate([ref[s] for s in range(ref.shape[0])], axis=0)


def _store_rows(ref, val):
    tm = ref.shape[1]
    for s in range(ref.shape[0]):
        ref[s] = val[s * tm:(s + 1) * tm, :]


def _with_ones_row(vt):
    rows = lax.broadcasted_iota(jnp.int32, (V_ROWS - V_DIM, vt.shape[1]), 0)
    return jnp.concatenate([vt, jnp.where(rows == 0, 1.0, 0.0).astype(BF16)], axis=0)


def _mod_kernel(c_ref, w_ref, b_ref, o_ref):
    a = jax.nn.silu(c_ref[...])
    a_hi, a_lo = _split2(a)
    w_hi, w_lo = _split2(w_ref[...])
    o_ref[...] = _dot(a_hi, w_hi) + _dot(a_lo, w_hi) + _dot(a_hi, w_lo) + b_ref[...]


def _mod_call(cond8, w_ada, b_ada):
    depth, d, n = w_ada.shape
    return pl.pallas_call(
        _mod_kernel,
        grid=(depth, n // MOD_TILE),
        in_specs=[pl.BlockSpec((8, d), lambda l, j: (0, 0)),
                  pl.BlockSpec((None, d, MOD_TILE), lambda l, j: (l, 0, j)),
                  pl.BlockSpec((None, 1, MOD_TILE), lambda l, j: (l, 0, j))],
        out_specs=pl.BlockSpec((None, 8, MOD_TILE), lambda l, j: (l, 0, j)),
        out_shape=jax.ShapeDtypeStruct((depth, 8, n), F32),
        compiler_params=_cp(40),
        name="adaln_mod",
    )(cond8, w_ada, b_ada.reshape(depth, 1, n))


def _proj_kernel(*refs, rope):
    if rope:
        (x_ref, sh_ref, sc_ref, g_ref, win_ref, qg_ref, kvg_ref, wq_ref, wkn_ref, wvt_ref,
         ck_ref, sk_ref, cr_ref, sr_ref, cc_ref, scl_ref,
         u_ref, va_ref, pin_ref, qt_ref, kh_ref, vt_ref) = refs
    else:
        (x_ref, sh_ref, sc_ref, g_ref, win_ref, qg_ref, kvg_ref, wq_ref, wkn_ref, wvt_ref,
         u_ref, va_ref, pin_ref, qt_ref, kh_ref, vt_ref, ckv_ref, kr_ref) = refs
    nseq, tm = x_ref.shape[0], x_ref.shape[1]
    ps = min(tm, PROJ_SUB)
    blocks = [(s, t0) for s in range(nseq) for t0 in range(0, tm, ps)]

    projs = []
    for s, t0 in blocks:
        x = x_ref[s, t0:t0 + ps, :]
        h = _rms(x) * g_ref[...] * (1.0 + sc_ref[...]) + sh_ref[...]
        projs.append(_dot(h.astype(BF16), win_ref[...]))

    for (s, t0), proj in zip(blocks, projs):
        rows = (s, slice(t0, t0 + ps))
        u_ref[rows] = jax.nn.gelu(proj[:, 0:A_WIDTH])
        va_ref[rows] = jax.nn.gelu(proj[:, A_WIDTH:OFF_P]).astype(BF16)
        pin_ref[rows] = proj[:, OFF_P:OFF_CQ]
        cq = _rms(proj[:, OFF_CQ:OFF_CKV]) * qg_ref[...]
        ckv = _rms(proj[:, OFF_CKV:OFF_KR]) * kvg_ref[...]
        kr = proj[:, OFF_KR:OFF_KR + LANES]
        if not rope:
            ckv_ref[rows] = ckv
            kr_ref[rows] = kr[:, 0:QK_ROPE]

        qt = _dot_nt(wq_ref[...], cq.astype(BF16))
        ckv_b = ckv.astype(BF16)
        kn = _dot(ckv_b, wkn_ref[...])
        vt = _dot_nt(wvt_ref[...], ckv_b)

        if rope:
            lane = lax.broadcasted_iota(jnp.int32, kr.shape, 1)
            half = ROPE_AXIS // 2
            swapped = jnp.where((lane & half) == 0,
                                pltpu.roll(kr, LANES - half, axis=1), pltpu.roll(kr, half, axis=1))
            kr = kr * ck_ref[t0:t0 + ps, :] + swapped * sk_ref[t0:t0 + ps, :]
            cr, sr = cr_ref[:, t0:t0 + ps], sr_ref[:, t0:t0 + ps]
            cc, scl = cc_ref[:, t0:t0 + ps], scl_ref[:, t0:t0 + ps]
        kr_shift = pltpu.roll(kr, QK_NOPE, axis=1)

        for hd in range(C_HEADS):
            r0 = hd * HEAD_PAD
            q_h = qt[r0:r0 + HEAD_PAD, :]
            if rope:
                b0 = QK_NOPE
                a, b = q_h[b0:b0 + 8, :], q_h[b0 + 8:b0 + 16, :]
                c, d = q_h[b0 + 16:b0 + 24, :], q_h[b0 + 24:b0 + 32, :]
                q_h = jnp.concatenate(
                    [q_h[0:b0, :], a * cr - b * sr, a * sr + b * cr, c * cc - d * scl, c * scl + d * cc,
                     q_h[b0 + 32:, :]], axis=0)
            q_h = (q_h * Q_PRESCALE).astype(BF16)
            qw = qt_ref.shape[-1]
            for j in range(ps // qw):
                qt_ref[s, t0 // qw + j, hd] = q_h[:, j * qw:(j + 1) * qw]
            kh_ref[s, hd, t0:t0 + ps, :] = (kn[:, r0:r0 + HEAD_PAD] + kr_shift).astype(BF16)
            vt_ref[s, hd, :, t0:t0 + ps] = _with_ones_row(vt[hd * V_DIM:(hd + 1) * V_DIM, :].astype(BF16))


def _seq_tiling(n, row_step, rows=None):
    rows = ROW_TILE if rows is None else rows
    tm = min(n, rows)
    return (rows // tm if row_step == 0 else 1), tm


def _proj_call(x, modl, row0, row_step, lw, tabs, layer):
    nb, n, d = x.shape
    sb, tm = _seq_tiling(n, row_step)
    rope = tabs is not None
    assert not rope or sb == 1
    qw = min(n, ATTN_TQ)
    tok = lambda c: pl.BlockSpec((sb, tm, c), lambda b, i: (b, i, 0))
    const = lambda a: _layer_spec(a, layer)
    modspec = lambda j: pl.BlockSpec((None, None, None, 1, d),
                                     lambda b, i: (layer, row0 + b * sb * row_step, j, 0, 0))
    ins = [x, modl, modl, lw["norm1_g"], lw["w_in"], lw["q_norm_g"], lw["kv_norm_g"],
           lw["wq_t"], lw["wkn"], lw["wv_t"]]
    in_specs = [tok(d), modspec(0), modspec(1), const(lw["norm1_g"]), const(lw["w_in"]),
                const(lw["q_norm_g"]), const(lw["kv_norm_g"]), const(lw["wq_t"]), const(lw["wkn"]),
                const(lw["wv_t"])]
    if rope:
        cos_k, sin_k, cos_r, sin_r, cos_c, sin_c = tabs
        ins += [cos_k, sin_k, cos_r, sin_r, cos_c, sin_c]
        in_specs += [pl.BlockSpec((tm, LANES), lambda b, i: (i, 0))] * 2
        in_specs += [pl.BlockSpec((8, tm), lambda b, i: (0, i))] * 4
    out_shape = [jax.ShapeDtypeStruct((nb, n, A_WIDTH), F32),
                 jax.ShapeDtypeStruct((nb, n, A_WIDTH), BF16),
                 jax.ShapeDtypeStruct((nb, n, B_WIDTH), F32),
                 jax.ShapeDtypeStruct((nb, n // qw, C_HEADS, HEAD_PAD, qw), BF16),
                 jax.ShapeDtypeStruct((nb, C_HEADS, n, HEAD_PAD), BF16),
                 jax.ShapeDtypeStruct((nb, C_HEADS, V_ROWS,n), BF16),
                 jax.ShapeDtypeStruct((nb, n, KV_LORA), F32),
                 jax.ShapeDtypeStruct((nb, n, QK_ROPE), F32)]
    out_specs = [tok(A_WIDTH), tok(A_WIDTH), tok(B_WIDTH),
                 pl.BlockSpec((sb, tm // qw, C_HEADS, HEAD_PAD, qw), lambda b, i: (b, i, 0, 0, 0)),
                 pl.BlockSpec((sb, C_HEADS, tm, HEAD_PAD), lambda b, i: (b, 0, i, 0)),
                 pl.BlockSpec((sb, C_HEADS, V_ROWS, tm), lambda b, i: (b, 0, 0, i)),
                 tok(KV_LORA), tok(QK_ROPE)]
    if rope:
        out_shape, out_specs = out_shape[:-2], out_specs[:-2]
    return pl.pallas_call(
        functools.partial(_proj_kernel, rope=rope),
        grid=(nb // sb, n // tm), in_specs=in_specs, out_specs=out_specs, out_shape=out_shape,
        compiler_params=_cp(48), name="in_proj_rope" if rope else "in_proj",
    )(*ins)


def _kvexp_kernel(ckv_ref, kr_ref, wkn_ref, wvt_ref, place_ref, kh_ref, vt_ref):
    ckv_b = ckv_ref[...].astype(BF16)
    kn = _dot(ckv_b, wkn_ref[...])
    vt = _dot_nt(wvt_ref[...], ckv_b)
    kr_placed = _dot(kr_ref[...].astype(BF16), place_ref[...])
    for hd in range(C_HEADS):
        r0 = hd * HEAD_PAD
        kh_ref[hd] = (kn[:, r0:r0 + HEAD_PAD] + kr_placed).astype(BF16)
        vt_ref[hd] = _with_ones_row(vt[hd * V_DIM:(hd + 1) * V_DIM, :].astype(BF16))


def _kvexp_call(cache_ckv, cache_krope, wkn, wv_t):
    nb, depth, m, r = cache_ckv.shape
    place = np.zeros((QK_ROPE, HEAD_PAD), np.float32)
    place[np.arange(QK_ROPE), QK_NOPE + np.arange(QK_ROPE)] = 1.0
    place = jnp.asarray(place, BF16)
    return pl.pallas_call(
        _kvexp_kernel,
        grid=(depth, nb),
        in_specs=[pl.BlockSpec((None, None, m, r), lambda l, b: (b, l, 0, 0)),
                  pl.BlockSpec((None, None, m, QK_ROPE), lambda l, b: (b, l, 0, 0)),
                  pl.BlockSpec((None,) + wkn.shape[1:], lambda l, b: (l, 0, 0)),
                  pl.BlockSpec((None,) + wv_t.shape[1:], lambda l, b: (l, 0, 0)),
                  pl.BlockSpec(place.shape, lambda l, b: (0, 0))],
        out_specs=[pl.BlockSpec((None, None, C_HEADS, m, HEAD_PAD), lambda l, b: (l, b, 0, 0, 0)),
                   pl.BlockSpec((None, None, C_HEADS, V_ROWS,m), lambda l, b: (l, b, 0, 0, 0))],
        out_shape=[jax.ShapeDtypeStruct((depth, nb, C_HEADS, m, HEAD_PAD), BF16),
                   jax.ShapeDtypeStruct((depth, nb, C_HEADS, V_ROWS,m), BF16)],
        compiler_params=_cp(32), name="cache_kv_expand",
    )(cache_ckv, cache_krope, wkn, wv_t, place)


def _mix_kernel(u_ref, va_ref, pin_ref, ws_ref, bs_ref, band_ref, wp_ref, ps_ref,
                oa_ref, ob_ref, pad_ref):
    n = u_ref.shape[0]
    width = u_ref.shape[1]
    zero_rows = jnp.zeros((POOL_HALO, width), F32)
    pad_ref[0:POOL_HALO, :] = zero_rows
    pad_ref[n + POOL_HALO:n + 2 * POOL_HALO, :] = zero_rows
    pad_ref[POOL_HALO:n + POOL_HALO, :] = pin_ref[...]
    ngrp = len(POOL_WINDOWS)
    unroll = min(MIX_UNROLL, n // CHUNK)
    row = lax.broadcasted_iota(jnp.int32, (CHUNK, width), 0)
    grp_row = lax.broadcasted_iota(jnp.int32, (1, width), 1) // POOL_GROUP
    left = jnp.zeros((1, width), jnp.int32)
    right1 = jnp.zeros((1, width), jnp.int32)
    for g, w in enumerate(POOL_WINDOWS):
        left = jnp.where(grp_row == g, w // 2, left)
        right1 = jnp.where(grp_row == g, w - w // 2, right1)

    def group_masks(nrows):
        grp = lax.broadcasted_iota(jnp.int32, (nrows, width), 1) // POOL_GROUP
        return [jnp.where(grp == g, 1.0, 0.0).astype(BF16) for g in range(ngrp)]

    masks_v = group_masks(CHUNK)
    masks_w = group_masks(CHUNK + 2 * POOL_HALO)

    def group_stack(x, masks):
        return jnp.concatenate([x * m for m in masks], axis=0)

    def first_dots(c):
        r0 = pl.multiple_of(c * CHUNK, CHUNK)
        v_stack = group_stack(va_ref[pl.ds(r0, CHUNK), :], masks_v)
        s = _dot(ws_ref[...], v_stack) + bs_ref[...]
        oa_ref[pl.ds(r0, CHUNK), :] = (u_ref[pl.ds(r0, CHUNK), :] * s).astype(BF16)
        win = pad_ref[pl.ds(r0, CHUNK + 2 * POOL_HALO), :]
        w_hi, w_lo = _split2(win)
        tot = (_dot(band_ref[...], group_stack(w_hi, masks_w))
               + _dot(band_ref[...], group_stack(w_lo, masks_w)))
        t = r0 + row
        cnt = jnp.minimum(t + right1, n) - jnp.maximum(t - left, 0)
        p = win[POOL_HALO:POOL_HALO + CHUNK, :]
        return r0, (tot / cnt.astype(F32) - p).astype(BF16)

    def body(j, carry):
        diffs = [first_dots(j * unroll + r) for r in range(unroll)]
        for r0, diff in diffs:
            ob_ref[pl.ds(r0, CHUNK), :] = (_dot(diff, wp_ref[...]) * ps_ref[...]).astype(BF16)
        return carry

    lax.fori_loop(0, n // (CHUNK * unroll), body, 0)


def _pool_bands():
    rows = CHUNK + 2 * POOL_HALO
    band = np.zeros((len(POOL_WINDOWS), CHUNK, rows), np.float32)
    i = np.arange(CHUNK)[:, None]
    j = np.arange(rows)[None, :]
    for g, w in enumerate(POOL_WINDOWS):
        left = w // 2
        right = w - 1 - left
        band[g] = ((j >= i + POOL_HALO - left) & (j <= i + POOL_HALO + right)).astype(np.float32)
    return jnp.asarray(band.transpose(1, 0, 2).reshape(CHUNK, -1), BF16)


def _mix_call(u, va, pin, lw, band, layer):
    nb, n, w = u.shape
    seq = lambda: pl.BlockSpec((None, n, w), lambda b: (b, 0, 0))
    const = lambda a: _layer_spec(a, layer)
    return pl.pallas_call(
        _mix_kernel,
        grid=(nb,),
        in_specs=[seq(), seq(), seq(), const(lw["w_s"]), const(lw["b_s"]),
                  pl.BlockSpec(band.shape, lambda b: (0, 0)),
                  const(lw["w_pool"]), const(lw["pool_scale"])],
        out_specs=[seq(), seq()],
        out_shape=[jax.ShapeDtypeStruct((nb, n, w), BF16)] * 2,
        scratch_shapes=[pltpu.VMEM((n + 2 * POOL_HALO, w), F32)],
        compiler_params=_cp(48), name="mixers_ab",
    )(u, va, pin, lw["w_s"], lw["b_s"], band, lw["w_pool"], lw["pool_scale"])


def _attn_kernel(*refs, seg_lens):
    nseg = len(seg_lens)
    qt_ref = refs[0]
    k_refs = refs[1:1 + nseg]
    v_refs = refs[1 + nseg:1 + 2 * nseg]
    o_ref, ot_ref = refs[1 + 2 * nseg:3 + 2 * nseg]
    s_bufs = refs[3 + 2 * nseg:]
    nsq, nqb, _, _, tq = qt_ref.shape
    nitem = nsq * nqb * C_HEADS
    head_bits = C_HEADS.bit_length() - 1
    qb_bits = nqb.bit_length() - 1
    assert C_HEADS == 1 << head_bits and nqb == 1 << qb_bits

    def unpack(it):
        blk = it >> head_bits
        return blk >> qb_bits, blk, blk & (nqb - 1), it & (C_HEADS - 1)

    def scores_into(it, s_ref):
        sq, _, qb, hd = unpack(it)
        q = qt_ref[sq, qb, hd]
        mx = None
        r0 = 0
        for k_ref, m in zip(k_refs, seg_lens):
            for c0 in range(0, m, KEY_CHUNK):
                kc = min(KEY_CHUNK, m - c0)
                s = _dot(k_ref[sq, hd, c0:c0 + kc, :], q)
                s_ref[r0:r0 + kc, :] = s
                cm = jnp.max(s, axis=0, keepdims=True)
                mx = cm if mx is None else jnp.maximum(mx, cm)
                r0 += kc
        return mx

    def finish(it, s_ref, mx):
        sq, blk, _, hd = unpack(it)
        acc = jnp.zeros((V_ROWS, tq), F32)
        r0 = 0
        for v_ref, m in zip(v_refs, seg_lens):
            for c0 in range(0, m, KEY_CHUNK):
                kc = min(KEY_CHUNK, m - c0)
                p = jnp.exp2(s_ref[r0:r0 + kc, :] - mx).astype(BF16)
                acc = acc + _dot(v_ref[sq, hd, :, c0:c0 + kc], p)
                r0 += kc
        rows = pl.ds(pl.multiple_of(hd * V_DIM, V_DIM), V_DIM)
        ot_ref[blk, rows, :] = acc[0:V_DIM, :] / acc[V_DIM:V_DIM + 1, :]

    nbuf = len(s_bufs)

    def step(t, r, mx):
        mx_next = scores_into(t + 1, s_bufs[(r + 1) % nbuf])
        finish(t, s_bufs[r], mx)
        return mx_next

    def rotation(j, mx):
        for r in range(nbuf):
            mx = step(nbuf * j + r, r, mx)
        return mx

    nsteps = nitem - 1
    mx = lax.fori_loop(0, nsteps // nbuf, rotation, scores_into(0, s_bufs[0]))
    for r in range(nsteps % nbuf):
        mx = step((nsteps // nbuf) * nbuf + r, r, mx)
    finish(nitem - 1, s_bufs[(nitem - 1) % nbuf], mx)
    for sq in range(nsq):
        for qb in range(nqb):
            o_ref[sq, qb * tq:(qb + 1) * tq, :] = ot_ref[sq * nqb + qb].T.astype(BF16)


def _attn_call(qt, segs):
    nb, nblk, _, _, tq = qt.shape
    nqb = min(nblk, ATTN_Q_BLOCKS)
    sb = ATTN_Q_BLOCKS // nqb
    seg_lens = tuple(k.shape[-2] for k, _, _ in segs)

    def seg_spec(a, layer):
        if layer is None:
            return pl.BlockSpec((sb,) + a.shape[1:], lambda b, i: (b, 0, 0, 0))
        return pl.BlockSpec((None, sb) + a.shape[2:], lambda b, i: (layer, b, 0, 0, 0))

    in_specs = [pl.BlockSpec((sb, nqb, C_HEADS, HEAD_PAD, tq), lambda b, i: (b, i, 0, 0, 0))]
    in_specs += [seg_spec(k, layer) for k, _, layer in segs]
    in_specs += [seg_spec(v, layer) for _, v, layer in segs]
    return pl.pallas_call(
        functools.partial(_attn_kernel, seg_lens=seg_lens),
        grid=(nb // sb, nblk // nqb), in_specs=in_specs,
        out_specs=pl.BlockSpec((sb, nqb * tq, C_WIDTH), lambda b, i: (b, i, 0)),
        out_shape=jax.ShapeDtypeStruct((nb, nblk * tq, C_WIDTH), BF16),
        scratch_shapes=([pltpu.VMEM((sb * nqb, C_WIDTH, tq), F32)]
                        + [pltpu.VMEM((sum(seg_lens), tq), F32)] * SCORE_BUFFERS),
        compiler_params=_cp(48), name="latent_attention",
    )(qt, *[k for k, _, _ in segs], *[v for _, v, _ in segs])


def _outproj_mix(rows, x_ref, oa_ref, ob_ref, oc_ref, wo_ref, g1_ref, sh_ref, sc_ref, ng_ref,
                 wr_ref, x1_ref, h_ref, aff_ref, asp_ref):
    mix = (_dot(oa_ref[rows], wo_ref[0:A_WIDTH, :])
           + _dot(ob_ref[rows], wo_ref[A_WIDTH:A_WIDTH + B_WIDTH, :])
           + _dot(oc_ref[rows], wo_ref[A_WIDTH + B_WIDTH:, :]))
    x1 = x_ref[rows] + g1_ref[...] * mix
    x1_ref[rows] = x1
    h = _rms(x1) * ng_ref[...] * (1.0 + sc_ref[...]) + sh_ref[...]
    h_hi, h_lo = _split2(h)
    h_ref[rows] = h_hi
    return h_hi, h_lo


def _outproj_route(rows, h_hi, h_lo, wr_ref, aff_ref, asp_ref):
    part = _dot(h_hi, wr_ref[...])
    logits = part + pltpu.roll(part, LANES - N_EXPERTS, axis=1) + _dot(h_lo, wr_ref[...])
    lane = lax.broadcasted_iota(jnp.int32, logits.shape, 1)
    logits = jnp.where(lane < N_EXPERTS, logits, -jnp.inf)
    ex = jnp.exp(logits - jnp.max(logits, axis=-1, keepdims=True))
    aff = ex / jnp.sum(ex, axis=-1, keepdims=True)
    aff_ref[rows] = aff
    a_hi = aff.astype(BF16).astype(F32)
    r1 = aff - a_hi
    a_lo = r1.astype(BF16).astype(F32)
    a_lo2 = r1 - a_lo
    packed = a_hi + pltpu.roll(a_lo, N_EXPERTS, axis=1) + pltpu.roll(a_lo2, 2 * N_EXPERTS, axis=1)
    asp_ref[rows] = packed.astype(BF16)


def _outproj_kernel(x_ref, *refs):
    nseq, tm = x_ref.shape[0], x_ref.shape[1]
    wr_ref, aff_ref, asp_ref = refs[8], refs[11], refs[12]
    sub = min(tm, SUB_ROWS)
    blocks = [(s, slice(r0, r0 + sub)) for s in range(nseq) for r0 in range(0, tm, sub)]
    splits = [_outproj_mix(rows, x_ref, *refs) for rows in blocks]
    for rows, (h_hi, h_lo) in zip(blocks, splits):
        _outproj_route(rows, h_hi, h_lo, wr_ref, aff_ref, asp_ref)


def _outproj_call(x, oa, ob, oc, modl, row0, row_step, lw, layer):
    nb, n, d = x.shape
    sb, tm = _seq_tiling(n, row_step)
    tok = lambda c: pl.BlockSpec((sb, tm, c), lambda b, i: (b, i, 0))
    const = lambda a: _layer_spec(a, layer)
    modspec = lambda j: pl.BlockSpec((None, None, None, 1, d),
                                     lambda b, i: (layer, row0 + b * sb * row_step, j, 0, 0))
    return pl.pallas_call(
        _outproj_kernel,
        grid=(nb // sb, n // tm),
        in_specs=[tok(d), tok(A_WIDTH), tok(B_WIDTH), tok(C_WIDTH), const(lw["w_out"]),
                  modspec(2), modspec(3), modspec(4), const(lw["norm2_g"]),
                  const(lw["w_router"])],
        out_specs=[tok(d), tok(d), tok(LANES), tok(LANES)],
        out_shape=[jax.ShapeDtypeStruct((nb, n, d), F32), jax.ShapeDtypeStruct((nb, n, d), BF16),
                   jax.ShapeDtypeStruct((nb, n, LANES), F32), jax.ShapeDtypeStruct((nb, n, LANES), BF16)],
        compiler_params=_cp(48), name="out_proj_router",
    )(x, oa, ob, oc, lw["w_out"], modl, modl, modl, lw["norm2_g"], lw["w_router"])


def _route_kernel(aff_ref, pos_ref, post_ref, offs_ref, *, cap):
    nb, n, _ = aff_ref.shape
    c = TOK_TILE
    nck = n // c
    aff = aff_ref[...]
    capf = jnp.float32(cap)

    def search(i, bits):
        cand = bits | jnp.left_shift(jnp.int32(1), jnp.int32(SEARCH_BITS - 1) - i)
        cnt = jnp.sum(jnp.where(aff >= lax.bitcast_convert_type(cand, F32), 1.0, 0.0),
                      axis=1, keepdims=True)
        return jnp.where(cnt >= capf, cand, bits)

    bits = lax.fori_loop(0, SEARCH_BITS, search, jnp.zeros((nb, 1, LANES), jnp.int32))
    thr_all = lax.bitcast_convert_type(bits, F32)
    n_gt = jnp.sum(jnp.where(aff > thr_all, 1.0, 0.0), axis=1, keepdims=True)
    n_tie_all = capf - n_gt
    ri = lax.broadcasted_iota(jnp.int32, (c, c), 0)
    ci = lax.broadcasted_iota(jnp.int32, (c, c), 1)
    tri = jnp.where(ri > ci, 1.0, 0.0).astype(BF16)
    lane_ok = lax.broadcasted_iota(jnp.int32, (c, LANES), 1) < N_EXPERTS
    offs_ref[...] = jnp.zeros(offs_ref.shape, jnp.int32)
    for s in range(nb):
        thr = thr_all[s]
        n_tie = n_tie_all[s]
        tie_seen = jnp.zeros((1, LANES), F32)
        sel_seen = jnp.zeros((1, LANES), F32)
        for k in range(nck):
            kk = aff_ref[s, k * c:(k + 1) * c, :]
            gt = kk > thr
            eq = kk == thr
            eqf = jnp.where(eq, 1.0, 0.0)
            tie_rank = _dot(tri, eqf.astype(BF16)) + tie_seen
            sel = (gt | (eq & (tie_rank < n_tie))) & lane_ok
            self_ = jnp.where(sel, 1.0, 0.0)
            slot = _dot(tri, self_.astype(BF16)) + sel_seen
            pos = jnp.where(sel, slot, -1.0)
            pos_ref[s, k * c:(k + 1) * c, :] = pos
            post_ref[s, :, k * c:(k + 1) * c] = pos.T[0:N_EXPERTS, :]
            offs_ref[s, k:k + 1, :] = sel_seen[:, 0:N_EXPERTS].astype(jnp.int32)
            tie_seen = tie_seen + jnp.sum(eqf, axis=0, keepdims=True)
            sel_seen = sel_seen + jnp.sum(self_, axis=0, keepdims=True)
        offs_ref[s, nck:nck + 1, :] = sel_seen[:, 0:N_EXPERTS].astype(jnp.int32)


def _offs_rows(n):
    return -(-(n // TOK_TILE + 1) // 8) * 8


def _route_call(aff, cap):
    nb, n, _ = aff.shape
    rows = _offs_rows(n)
    return pl.pallas_call(
        functools.partial(_route_kernel, cap=cap),
        grid=(1,),
        in_specs=[pl.BlockSpec((nb, n, LANES), lambda i: (0, 0, 0))],
        out_specs=[pl.BlockSpec((nb, n, LANES), lambda i: (0, 0, 0)),
                   pl.BlockSpec((nb, N_EXPERTS, n), lambda i: (0, 0, 0)),
                   pl.BlockSpec((nb, rows, N_EXPERTS), lambda i: (0, 0, 0))],
        out_shape=[jax.ShapeDtypeStruct((nb, n, LANES), F32),
                   jax.ShapeDtypeStruct((nb, N_EXPERTS, n), F32),
                   jax.ShapeDtypeStruct((nb, rows, N_EXPERTS), jnp.int32)],
        compiler_params=_cp(32), name="ec_route",
    )(aff, )


def _window(offs_ref, b, k, e, rows, win):
    off = offs_ref[(b * rows + k) * N_EXPERTS + e]
    nxt = offs_ref[(b * rows + k + 1) * N_EXPERTS + e]
    base = lax.shift_right_logical(off, SLOT_ALIGN.bit_length() - 1) * SLOT_ALIGN
    nwin = lax.shift_right_logical(nxt - base + (win - 1), win.bit_length() - 1)
    return base, nwin


def _slot_targets(ids, start, cap, win):
    cb = pl.multiple_of(jnp.minimum(start, cap - win), SLOT_ALIGN)
    slot = ids + cb.astype(F32)
    return jnp.where(slot >= start.astype(F32), slot, -2.0), cb


def _gather_kernel(offs_ref, post_ref, h_ref, asp_ref, xs_ref, gs_ref, *, cap, win, rows):
    nseq, tm = h_ref.shape[0], h_ref.shape[1]
    c = TOK_TILE
    b0 = pl.program_id(0) * nseq
    k0 = pl.program_id(1) * (tm // c)

    @pl.when(pl.program_id(1) == 0)
    def _():
        xs_ref[...] = jnp.zeros(xs_ref.shape, xs_ref.dtype)
        gs_ref[...] = jnp.zeros(gs_ref.shape, gs_ref.dtype)

    sub = lax.broadcasted_iota(jnp.int32, (win, 1), 0).astype(F32)

    def one_chunk(s, cc, overflow):
        b = b0 + s
        k = k0 + cc
        cols = slice(cc * c, (cc + 1) * c)
        hk = h_ref[s, cols, :]
        gk = asp_ref[s, cols, :]

        def sel_rows(e, start):
            tgt, cb = _slot_targets(sub, start, cap, win)
            return jnp.where(post_ref[s, e:e + 1, cols] == tgt, 1.0, 0.0).astype(BF16), s * cap + cb

        if overflow:
            for e in range(N_EXPERTS):
                base, nwin = _window(offs_ref, b, k, e, rows, win)

                def extra(w, carry, e=e, base=base):
                    p, cbw = sel_rows(e, base + w * win)
                    xs_ref[e, pl.ds(cbw, win), :] += _dot(p, hk).astype(BF16)
                    gs_ref[e, pl.ds(cbw, win), :] += _dot(p, gk)
                    return carry

                lax.fori_loop(1, nwin, extra, 0)
            return

        for g0 in range(0, N_EXPERTS, GATHER_GROUP):
            parts, cbs = [], []
            for e in range(g0, g0 + GATHER_GROUP):
                base, _ = _window(offs_ref, b, k, e, rows, win)
                p, cb = sel_rows(e, base)
                parts.append(p)
                cbs.append(cb)
            p_grp = jnp.concatenate(parts, axis=0)
            r_grp = _dot(p_grp, hk).astype(BF16)
            g_grp = _dot(p_grp, gk)
            for j, cb in enumerate(cbs):
                xs_ref[g0 + j, pl.ds(cb, win), :] += r_grp[j * win:(j + 1) * win, :]
                gs_ref[g0 + j, pl.ds(cb, win), :] += g_grp[j * win:(j + 1) * win, :]

    for overflow in (False, True):
        for s in range(nseq):
            for cc in range(tm // c):
                one_chunk(s, cc, overflow)


def _gather_call(offs, post, h, asp, cap, win):
    nb, n, d = h.shape
    sb, tm = _seq_tiling(n, 0, EC_ROWS)
    rows = _offs_rows(n)
    grid_spec = pltpu.PrefetchScalarGridSpec(
        num_scalar_prefetch=1, grid=(nb // sb, n // tm),
        in_specs=[pl.BlockSpec((sb, N_EXPERTS, tm), lambda b, k, o: (b, 0, k)),
                  pl.BlockSpec((sb, tm, d), lambda b, k, o: (b, k, 0)),
                  pl.BlockSpec((sb, tm, LANES), lambda b, k, o: (b, k, 0))],
        out_specs=[pl.BlockSpec((N_EXPERTS, sb * cap, d), lambda b, k, o: (0, b, 0)),
                   pl.BlockSpec((N_EXPERTS, sb * cap, LANES), lambda b, k, o: (0, b, 0))])
    return pl.pallas_call(
        functools.partial(_gather_kernel, cap=cap, win=win, rows=rows),
        grid_spec=grid_spec,
        out_shape=[jax.ShapeDtypeStruct((N_EXPERTS, nb * cap, d), BF16),
                   jax.ShapeDtypeStruct((N_EXPERTS, nb * cap, LANES), F32)],
        compiler_params=_cp(48), name="ec_gather",
    )(offs.reshape(-1), post, h, asp)


def _ffn_kernel(xa_ref, xb_ref, ga_ref, gb_ref, w1_ref, w3_ref, w2_ref, ya_ref, yb_ref,
                w1b_ref, w3b_ref, w2b_ref, *, na):
    e = pl.program_id(0)
    r = pl.program_id(1)

    def block(x_ref, g_ref, y_ref, w1, w3, w2):
        x = x_ref[...]
        hid = (jax.nn.silu(_dot(x, w1)) * _dot(x, w3)).astype(BF16)
        y = _dot(hid, w2)
        g = g_ref[...]
        lane = lax.broadcasted_iota(jnp.int32, g.shape, 1)
        keep = ((lane % N_EXPERTS) == e) & (lane < 3 * N_EXPERTS)
        gate = jnp.sum(jnp.where(keep, g, 0.0), axis=-1, keepdims=True)
        y_ref[...] = (y * gate).astype(BF16)

    @pl.when(r == 0)
    def _():
        w1, w3, w2 = (w_ref[...].astype(BF16) for w_ref in (w1_ref, w3_ref, w2_ref))
        w1b_ref[...] = w1
        w3b_ref[...] = w3
        w2b_ref[...] = w2
        block(xa_ref, ga_ref, ya_ref, w1, w3, w2)

    @pl.when((r > 0) & (r < na))
    def _():
        block(xa_ref, ga_ref, ya_ref, w1b_ref[...], w3b_ref[...], w2b_ref[...])

    @pl.when(r == na)
    def _():
        block(xb_ref, gb_ref, yb_ref, w1b_ref[...], w3b_ref[...], w2b_ref[...])


def _ffn_call(xa, xb, ga, gb, w1, w3, w2, layer):
    ne, ra, d = xa.shape
    rb = xb.shape[1]
    ff = w1.shape[3]
    na = ra // rb
    assert ra == na * rb and na >= 1
    a_spec = lambda c: pl.BlockSpec((None, rb, c), lambda e, r: (e, jnp.minimum(r, na - 1), 0))
    b_spec = lambda c: pl.BlockSpec((None, rb, c), lambda e, r: (e, 0, 0))
    return pl.pallas_call(
        functools.partial(_ffn_kernel, na=na),
        grid=(ne, na + 1),
        in_specs=[a_spec(d), b_spec(d), a_spec(LANES), b_spec(LANES),
                  pl.BlockSpec((None, None, d, ff), lambda e, r: (layer, e, 0, 0)),
                  pl.BlockSpec((None, None, d, ff), lambda e, r: (layer, e, 0, 0)),
                  pl.BlockSpec((None, None, ff, d), lambda e, r: (layer, e, 0, 0))],
        out_specs=[a_spec(d), b_spec(d)],
        out_shape=[jax.ShapeDtypeStruct((ne, ra, d), BF16), jax.ShapeDtypeStruct((ne, rb, d), BF16)],
        scratch_shapes=[pltpu.VMEM((d, ff), BF16), pltpu.VMEM((d, ff), BF16), pltpu.VMEM((ff, d), BF16)],
        compiler_params=_cp(56), name="expert_swiglu",
    )(xa, xb, ga, gb, w1, w3, w2)


def _combine_kernel(offs_ref, pos_ref, y_ref, x_ref, g2_ref, ex_ref, *rest, cap, win, rows, last):
    if last:
        fg_ref, o_ref, on_ref = rest
    else:
        (o_ref,) = rest
    nseq, tm = x_ref.shape[0], x_ref.shape[1]
    c = TOK_TILE
    b0 = pl.program_id(0) * nseq
    k0 = pl.program_id(1) * (tm // c)
    per = MXU_DEPTH // win
    depth = per * win
    lane = lax.broadcasted_iota(jnp.int32, (1, depth), 1)
    lane_grp = lane // win
    lane_in = (lane % win).astype(F32)
    lane1 = lax.broadcasted_iota(jnp.int32, (1, win), 1).astype(F32)

    def one_chunk(s, cc, overflow, pos_wide=None):
        b = b0 + s
        k = k0 + cc
        rws = (s, slice(cc * c, (cc + 1) * c))
        if overflow:
            for e in range(N_EXPERTS):
                base, nwin = _window(offs_ref, b, k, e, rows, win)

                def extra(w, carry, e=e, base=base):
                    tgt_w, cbw = _slot_targets(lane1, base + w * win, cap, win)
                    pt = jnp.where(pos_ref[rws][:, e:e + 1] == tgt_w, 1.0, 0.0).astype(BF16)
                    o_ref[rws] += g2_ref[...] * _dot(pt, y_ref[e, pl.ds(s * cap + cbw, win), :])
                    return carry

                lax.fori_loop(1, nwin, extra, 0)
            if last:
                on_ref[rws] = _rms(o_ref[rws]) * fg_ref[...]
            return
        tot = None
        for gi, g0 in enumerate(range(0, N_EXPERTS, per)):
            tgt = jnp.full((1, depth), -2.0, F32)
            ys = []
            for j in range(per):
                base, _ = _window(offs_ref, b, k, g0 + j, rows, win)
                tgt_e, cb = _slot_targets(lane_in, base, cap, win)
                tgt = jnp.where(lane_grp == j, tgt_e, tgt)
                ys.append(y_ref[g0 + j, pl.ds(s * cap + cb, win), :])
            pt = jnp.where(pos_wide[:, gi * depth:(gi + 1) * depth] == tgt, 1.0, 0.0).astype(BF16)
            d = _dot(pt, jnp.concatenate(ys, axis=0))
            tot = d if tot is None else tot + d
        o_ref[rws] = x_ref[rws] + g2_ref[...] * tot

    chunks = [(s, cc) for s in range(nseq) for cc in range(tm // c)]
    wides = [_dot(pos_ref[s, cc * c:(cc + 1) * c, :].astype(BF16), ex_ref[...]) for s, cc in chunks]
    for (s, cc), pos_wide in zip(chunks, wides):
        one_chunk(s, cc, False, pos_wide)
    for s, cc in chunks:
        one_chunk(s, cc, True)


def _combine_call(offs, pos, y, x1, modl, row0, row_step, final_g, cap, win, layer):
    nb, n, d = x1.shape
    sb, tm = _seq_tiling(n, row_step, EC_ROWS)
    rows = _offs_rows(n)
    last = final_g is not None
    assert cap <= 256
    spread = np.zeros((LANES, N_EXPERTS * win), np.float32)
    for e in range(N_EXPERTS):
        spread[e, e * win:(e + 1) * win] = 1.0
    spread = jnp.asarray(spread, BF16)
    tok = pl.BlockSpec((sb, tm, d), lambda b, k, o: (b, k, 0))
    in_specs = [pl.BlockSpec((sb, tm, LANES), lambda b, k, o: (b, k, 0)),
                pl.BlockSpec((N_EXPERTS, sb * cap, d), lambda b, k, o: (0, b, 0)),
                tok,
                pl.BlockSpec((None, None, None, 1, d),
                             lambda b, k, o: (layer, row0 + b * sb * row_step, 5, 0, 0)),
                pl.BlockSpec(spread.shape, lambda b, k, o: (0, 0))]
    ins = [offs.reshape(-1), pos, y, x1, modl, spread]
    if last:
        in_specs.append(pl.BlockSpec((1, d), lambda b, k, o: (0, 0)))
        ins.append(final_g)
    nout = 2 if last else 1
    grid_spec = pltpu.PrefetchScalarGridSpec(
        num_scalar_prefetch=1, grid=(nb // sb, n // tm), in_specs=in_specs, out_specs=[tok] * nout)
    return pl.pallas_call(
        functools.partial(_combine_kernel, cap=cap, win=win, rows=rows, last=last),
        grid_spec=grid_spec,
        out_shape=[jax.ShapeDtypeStruct((nb, n, d), F32)] * nout,
        compiler_params=_cp(56), name="ec_combine_final" if last else "ec_combine",
    )(*ins)


def _prep_weights(w_in, w_s, b_s, w_pool, pool_scale, q_norm_g, w_qb, kv_norm_g, w_kvb, w_out,
                  w_router, norm1_g, norm2_g):
    depth, d, cols = w_in.shape
    row = lambda a: a.reshape(depth, 1, -1)
    w_in_p = jnp.pad(w_in, ((0, 0), (0, 0), (0, PROJ_PAD - cols))).astype(BF16)
    wq = w_qb.reshape(depth, Q_LORA, C_HEADS, QK_NOPE + QK_ROPE)
    wq = jnp.pad(wq, ((0, 0), (0, 0), (0, 0), (0, HEAD_PAD - QK_NOPE - QK_ROPE)))
    wq_t = wq.reshape(depth, Q_LORA, C_HEADS * HEAD_PAD).transpose(0, 2, 1).astype(BF16)
    wkv = w_kvb.reshape(depth, KV_LORA, C_HEADS, QK_NOPE + V_DIM)
    wkn = jnp.pad(wkv[..., :QK_NOPE], ((0, 0), (0, 0), (0, 0), (0, HEAD_PAD - QK_NOPE)))
    wkn = wkn.reshape(depth, KV_LORA, C_HEADS * HEAD_PAD).astype(BF16)
    wv_t = wkv[..., QK_NOPE:].reshape(depth, KV_LORA, C_WIDTH).transpose(0, 2, 1).astype(BF16)
    assert A_HEADS == len(POOL_WINDOWS) and A_HEAD_DIM == POOL_GROUP
    ws_cat = w_s.transpose(0, 2, 1, 3).reshape(depth, CHUNK, A_HEADS * CHUNK).astype(BF16)
    b_rep = jnp.repeat(b_s.transpose(0, 2, 1), A_HEAD_DIM, axis=2)
    eye = jnp.eye(len(POOL_WINDOWS), dtype=F32)
    wp_bd = jnp.einsum("gh,lgcd->lgchd", eye, w_pool).reshape(depth, B_WIDTH, B_WIDTH).astype(BF16)
    wr_hi = w_router.astype(BF16)
    wr_lo = (w_router - wr_hi.astype(F32)).astype(BF16)
    wr = jnp.pad(jnp.concatenate([wr_hi, wr_lo], axis=2), ((0, 0), (0, 0), (0, LANES - 2 * N_EXPERTS)))
    return dict(
        norm1_g=row(norm1_g), norm2_g=row(norm2_g), w_in=w_in_p, q_norm_g=row(q_norm_g),
        kv_norm_g=row(kv_norm_g), wq_t=wq_t, wkn=wkn, wv_t=wv_t, w_s=ws_cat, b_s=b_rep, w_pool=wp_bd,
        pool_scale=row(pool_scale), w_out=w_out.astype(BF16), w_router=wr)


def _rope_tables(n):
    rows = n // GRID_W
    row = jnp.repeat(jnp.arange(rows), GRID_W).astype(F32)
    col = jnp.broadcast_to(jnp.arange(GRID_W), (rows, GRID_W)).reshape(-1).astype(F32)
    inv = 1.0 / (ROPE_THETA ** (jnp.arange(0, ROPE_AXIS, 2, dtype=F32) / ROPE_AXIS))
    ang_r = row[:, None] * inv[None, :]
    ang_c = col[:, None] * inv[None, :]
    cos_r, sin_r, cos_c, sin_c = jnp.cos(ang_r), jnp.sin(ang_r), jnp.cos(ang_c), jnp.sin(ang_c)
    pad = jnp.zeros((n, LANES - QK_ROPE), F32)
    cos_k = jnp.concatenate([cos_r, cos_r, cos_c, cos_c, pad], axis=1)
    sin_k = jnp.concatenate([-sin_r, sin_r, -sin_c, sin_c, pad], axis=1)
    return cos_k, sin_k, cos_r.T, sin_r.T, cos_c.T, sin_c.T


def kernel(x_prompt, x_sample, cache_ckv, cache_krope, c, c_ctx, w_ada, b_ada, norm1_g, norm2_g, w_in, w_s, b_s, w_pool, pool_scale, q_norm_g, w_qb, kv_norm_g, w_kvb, w_out, w_router, w_e1, w_e3, w_e2, final_norm_g):
    depth, d, _ = w_ada.shape
    nb_c, n_c, _ = x_prompt.shape
    nb_s, n_s, _ = x_sample.shape
    assert 1 + nb_s <= 8
    cond8 = jnp.concatenate([c_ctx[None, :], c, jnp.zeros((8 - 1 - nb_s, d), F32)], axis=0)
    mod = _mod_call(cond8, w_ada, b_ada).reshape(depth, 8, 6, 1, d)
    lw = _prep_weights(w_in, w_s, b_s, w_pool, pool_scale, q_norm_g, w_qb, kv_norm_g, w_kvb, w_out,
                       w_router, norm1_g, norm2_g)
    k_cache, vt_cache = _kvexp_call(cache_ckv, cache_krope, lw["wkn"], lw["wv_t"])
    tabs = _rope_tables(n_s)
    band = _pool_bands()
    final_g = final_norm_g.reshape(1, d)
    cap_c = EC_CAPACITY * n_c // N_EXPERTS
    cap_s = EC_CAPACITY * n_s // N_EXPERTS
    win_c = min(cap_c, 64)
    win_s = min(cap_s, 64)

    xc, xs = x_prompt, x_sample
    yc = ys = None
    ckvs, kropes = [], []
    for l in range(depth):
        uc, vac, pinc, qtc, khc, vtc, ckv_c, kr_c = _proj_call(xc, mod, 0, 0, lw, None, l)
        us, vas, pins, qts, khs, vts = _proj_call(xs, mod, 1, 1, lw, tabs, l)
        ckvs.append(ckv_c)
        kropes.append(kr_c)
        oac, obc = _mix_call(uc, vac, pinc, lw, band, l)
        oas, obs = _mix_call(us, vas, pins, lw, band, l)
        occ = _attn_call(qtc, [(khc, vtc, None)])
        ocs = _attn_call(qts, [(k_cache, vt_cache, l), (khs, vts, None)])
        x1c, hc, affc, aspc = _outproj_call(xc, oac, obc, occ, mod, 0, 0, lw, l)
        x1s, hs, affs, asps = _outproj_call(xs, oas, obs, ocs, mod, 1, 1, lw, l)
        posc, postc, offc = _route_call(affc, cap_c)
        poss, posts, offs = _route_call(affs, cap_s)
        xgc, ggc = _gather_call(offc, postc, hc, aspc, cap_c, win_c)
        xgs, ggs = _gather_call(offs, posts, hs, asps, cap_s, win_s)
        ygs, ygc = _ffn_call(xgs, xgc, ggs, ggc, w_e1, w_e3, w_e2, l)
        fg = final_g if l == depth - 1 else None
        outc = _combine_call(offc, posc, ygc, x1c, mod, 0, 0, fg, cap_c, win_c, l)
        outs = _combine_call(offs, poss, ygs, x1s, mod, 1, 1, fg, cap_s, win_s, l)
        xc, xs = outc[0], outs[0]
    yc, ys = outc[1], outs[1]
    new_ckv = jnp.stack(ckvs, axis=1)
    new_krope = jnp.stack(kropes, axis=1)
    return (yc, ys, new_ckv, new_krope)
```
